```python
import jax, jax.numpy as jnp
from jax import lax
import numpy as np

D_MODEL = 1024
BATCH = 8
SEQ = 8192
DEPTH = 2
DEC_BATCH = 16
DEC_SEQ = 16
PAST_LEN = 4096

CHUNK = 64
LEFT_CHUNKS = 8
BAND = LEFT_CHUNKS * CHUNK
A_WIDTH = D_MODEL // 2
A_HEAD_DIM = 64
A_HEADS = A_WIDTH // A_HEAD_DIM
REL_CLIP = 128
B_WIDTH = D_MODEL // 4
B_GROUPS = 4
B_KERNEL = 31
C_WIDTH = D_MODEL // 4
C_GROUPS = 4
C_KERNEL = 3
MIX_WIDTH = A_WIDTH + B_WIDTH + C_WIDTH
SPLITS = [A_WIDTH, 2 * A_WIDTH, 3 * A_WIDTH, 3 * A_WIDTH + 2 * B_WIDTH,
          3 * A_WIDTH + 2 * B_WIDTH + C_WIDTH, 3 * A_WIDTH + 2 * B_WIDTH + 2 * C_WIDTH]
IN_WIDTH = 3 * A_WIDTH + 2 * B_WIDTH + 3 * C_WIDTH
D_FF = 4 * D_MODEL
N_EXPERTS = 8
TOP_K = 2
D_FF_EXPERT = 2 * D_MODEL
PLE_DIM = 256
N_DENSE = (DEPTH + 1) // 2
N_MOE = DEPTH // 2
EPS = 1e-6

kernel_name = 'hybrid_chunk_stream_encoder_step'


def rms_norm(x, g):
    xf = x.astype(jnp.float32)
    y = xf * lax.rsqrt(jnp.mean(xf * xf, axis=-1, keepdims=True) + EPS)
    return (y * g.astype(jnp.float32)).astype(x.dtype)


def layer_norm(x, g, b):
    xf = x.astype(jnp.float32)
    mu = jnp.mean(xf, axis=-1, keepdims=True)
    xc = xf - mu
    var = jnp.mean(xc * xc, axis=-1, keepdims=True)
    return (xc * lax.rsqrt(var + EPS) * g.astype(jnp.float32) + b.astype(jnp.float32)).astype(x.dtype)


def causal_dwconv(x_hist, w):
    c = x_hist.shape[-1]
    return lax.conv_general_dilated(x_hist, w.astype(x_hist.dtype)[:, None, :], window_strides=(1,),
                                    padding='VALID', dimension_numbers=('NWC', 'WIO', 'NWC'),
                                    feature_group_count=c)


def band_attention(q, k, v, q_pos, k_pos, k_valid, rel_bias):
    rel = jnp.clip(k_pos[None, :] - q_pos[:, None], -REL_CLIP, REL_CLIP) + REL_CLIP
    bias = rel_bias[:, rel].astype(jnp.float32)
    s = jnp.einsum('bqhd,bkhd->bhqk', q, k).astype(jnp.float32) * (A_HEAD_DIM ** -0.5) + bias[None]
    s = jnp.where(k_valid[None, None, None, :], s, jnp.finfo(jnp.float32).min)
    p = jax.nn.softmax(s, axis=-1)
    return jnp.einsum('bhqk,bkhd->bqhd', p.astype(v.dtype), v)


def prompt_chunk_attention(q, k, v, rel_bias):
    b, s, h, d = q.shape
    n_chunks = s // CHUNK
    pad = jnp.zeros((b, BAND, h, d), k.dtype)
    kp = jnp.concatenate([pad, k], axis=1)
    vp = jnp.concatenate([pad, v], axis=1)
    qc = q.reshape(b, n_chunks, CHUNK, h, d).transpose(1, 0, 2, 3, 4)

    def one_chunk(args):
        c, qb = args
        start = c * CHUNK
        kb = lax.dynamic_slice_in_dim(kp, start, BAND + CHUNK, axis=1)
        vb = lax.dynamic_slice_in_dim(vp, start, BAND + CHUNK, axis=1)
        q_pos = start + jnp.arange(CHUNK, dtype=jnp.int32)
        k_pos = start - BAND + jnp.arange(BAND + CHUNK, dtype=jnp.int32)
        return band_attention(qb, kb, vb, q_pos, k_pos, k_pos >= 0, rel_bias)

    out = lax.map(one_chunk, (jnp.arange(n_chunks, dtype=jnp.int32), qc))
    return out.transpose(1, 0, 2, 3, 4).reshape(b, s, h, d)


def sample_chunk_attention(q, k, v, past_k, past_v, rel_bias):
    n_past = past_k.shape[1]
    t = q.shape[1]
    k_all = jnp.concatenate([past_k.astype(k.dtype), k], axis=1)
    v_all = jnp.concatenate([past_v.astype(v.dtype), v], axis=1)
    q_pos = PAST_LEN + jnp.arange(t, dtype=jnp.int32)
    k_pos = jnp.concatenate([PAST_LEN - n_past + jnp.arange(n_past, dtype=jnp.int32), q_pos])
    return band_attention(q, k_all, v_all, q_pos, k_pos, jnp.ones((n_past + t,), bool), rel_bias)


def swiglu(x, wg, wu, wd):
    return (jax.nn.silu(x @ wg) * (x @ wu)) @ wd


def moe_swiglu(x, w_router, b_router, wg, wu, wd):
    logits = (x @ w_router).astype(jnp.float32) + b_router.astype(jnp.float32)
    top_vals, top_idx = lax.top_k(logits, TOP_K)
    gates = jax.nn.softmax(top_vals, axis=-1)
    combine = jnp.sum(jax.nn.one_hot(top_idx, N_EXPERTS, dtype=jnp.float32) * gates[..., None], axis=-2)
    combine = combine.astype(x.dtype)
    out = jnp.zeros(x.shape[:-1] + (wd.shape[-1],), x.dtype)
    for e in range(N_EXPERTS):
        out = out + combine[..., e:e + 1] * swiglu(x, wg[e], wu[e], wd[e])
    return out


def trunk(x, p, past_k, past_v, past_b, past_c, params):
    (w_in, rel_bias, w_dw_b, b_dw_b, ln_g_b, ln_b_b, w_dw_c, g_out, w_out,
     norm_mix, norm_ffn, w_ff_gate, w_ff_up, w_ff_down, w_router, b_router,
     w_ex_gate, w_ex_up, w_ex_down, norm_ple, w_ple_gate, w_ple_proj, norm_final) = params
    batch, t, _ = x.shape
    dt = x.dtype
    h = x
    new_k, new_v, new_b, new_c = [], [], [], []
    for i in range(DEPTH):
        hn = rms_norm(h, norm_mix[i])
        u = hn @ w_in[i]
        q, k, v, ab, gate_c, gate_b, h_c = jnp.split(u, SPLITS, axis=-1)
        q = q.reshape(batch, t, A_HEADS, A_HEAD_DIM)
        k = k.reshape(batch, t, A_HEADS, A_HEAD_DIM)
        v = v.reshape(batch, t, A_HEADS, A_HEAD_DIM)
        if past_k is None:
            y_a = prompt_chunk_attention(q, k, v, rel_bias[i])
            keep = min(BAND, t)
            new_k.append(k[:, t - keep:])
            new_v.append(v[:, t - keep:])
        else:
            y_a = sample_chunk_attention(q, k, v, past_k[i], past_v[i], rel_bias[i])
            new_k.append(k)
            new_v.append(v)
        y_a = y_a.reshape(batch, t, A_WIDTH)
        val, gl = jnp.split(ab, 2, axis=-1)
        glu = val * jax.nn.sigmoid(gl)
        hist_b = jnp.zeros((batch, B_KERNEL - 1, B_WIDTH), dt) if past_b is None else past_b[i].astype(dt)
        cb_in = jnp.concatenate([hist_b, glu], axis=1)
        z = causal_dwconv(cb_in, w_dw_b[i]) + b_dw_b[i].astype(dt)
        y_b = jax.nn.silu(layer_norm(z, ln_g_b[i], ln_b_b[i]))
        new_b.append(cb_in[:, cb_in.shape[1] - (B_KERNEL - 1):])
        pre = gate_c * h_c
        hist_c = jnp.zeros((batch, C_KERNEL - 1, C_WIDTH), dt) if past_c is None else past_c[i].astype(dt)
        cc_in = jnp.concatenate([hist_c, pre], axis=1)
        y_c = gate_b * causal_dwconv(cc_in, w_dw_c[i])
        new_c.append(cc_in[:, cc_in.shape[1] - (C_KERNEL - 1):])
        g = g_out[i]
        y = jnp.concatenate([rms_norm(y_a, g[:A_WIDTH]),
                             rms_norm(y_b, g[A_WIDTH:A_WIDTH + B_WIDTH]),
                             rms_norm(y_c, g[A_WIDTH + B_WIDTH:])], axis=-1)
        h = h + y @ w_out[i]
        hn = rms_norm(h, norm_ffn[i])
        j = i // 2
        if i % 2 == 0:
            h = h + swiglu(hn, w_ff_gate[j], w_ff_up[j], w_ff_down[j])
        else:
            h = h + moe_swiglu(hn, w_router[j], b_router[j], w_ex_gate[j], w_ex_up[j], w_ex_down[j])
        gate = jax.nn.sigmoid(rms_norm(h, norm_ple[i]) @ w_ple_gate[i])
        h = h + gate * (p[i].astype(dt) @ w_ple_proj[i])
    return (rms_norm(h, norm_final), jnp.stack(new_k), jnp.stack(new_v), jnp.stack(new_b), jnp.stack(new_c))


def _normal(k, shape, scale):
    return jax.random.normal(k, shape, jnp.float32) * scale


def setup_inputs(seed: int = 0) -> dict:
    key = jax.random.key(seed)
    ks = jax.random.split(key, 40)
    a_cache = min(BAND, PAST_LEN)
    return {
        'x_prompt': _normal(ks[0], (BATCH, SEQ, D_MODEL), 1.0),
        'x_sample': _normal(ks[1], (DEC_BATCH, DEC_SEQ, D_MODEL), 1.0),
        'p_prompt': _normal(ks[2], (DEPTH, BATCH, SEQ, PLE_DIM), 1.0),
        'p_sample': _normal(ks[3], (DEPTH, DEC_BATCH, DEC_SEQ, PLE_DIM), 1.0),
        'cache_attn_k': _normal(ks[4], (DEPTH, DEC_BATCH, a_cache, A_HEADS, A_HEAD_DIM), 1.0),
        'cache_attn_v': _normal(ks[5], (DEPTH, DEC_BATCH, a_cache, A_HEADS, A_HEAD_DIM), 1.0),
        'state_conv_b': _normal(ks[6], (DEPTH, DEC_BATCH, B_KERNEL - 1, B_WIDTH), 0.5),
        'state_conv_c': _normal(ks[7], (DEPTH, DEC_BATCH, C_KERNEL - 1, C_WIDTH), 0.5),
        'w_in': _normal(ks[8], (DEPTH, D_MODEL, IN_WIDTH), D_MODEL ** -0.5),
        'rel_bias': _normal(ks[9], (DEPTH, A_HEADS, 2 * REL_CLIP + 1), 0.1),
        'w_dw_b': _normal(ks[10], (DEPTH, B_KERNEL, B_WIDTH), B_KERNEL ** -0.5),
        'b_dw_b': _normal(ks[11], (DEPTH, B_WIDTH), 0.01),
        'ln_g_b': 1.0 + _normal(ks[12], (DEPTH, B_WIDTH), 0.01),
        'ln_b_b': _normal(ks[13], (DEPTH, B_WIDTH), 0.01),
        'w_dw_c': _normal(ks[14], (DEPTH, C_KERNEL, C_WIDTH), C_KERNEL ** -0.5),
        'g_out': 1.0 + _normal(ks[15], (DEPTH, MIX_WIDTH), 0.01),
        'w_out': _normal(ks[16], (DEPTH, MIX_WIDTH, D_MODEL), MIX_WIDTH ** -0.5),
        'norm_mix': 1.0 + _normal(ks[17], (DEPTH, D_MODEL), 0.01),
        'norm_ffn': 1.0 + _normal(ks[18], (DEPTH, D_MODEL), 0.01),
        'w_ff_gate': _normal(ks[19], (N_DENSE, D_MODEL, D_FF), D_MODEL ** -0.5),
        'w_ff_up': _normal(ks[20], (N_DENSE, D_MODEL, D_FF), D_MODEL ** -0.5),
        'w_ff_down': _normal(ks[21], (N_DENSE, D_FF, D_MODEL), D_FF ** -0.5),
        'w_router': _normal(ks[22], (N_MOE, D_MODEL, N_EXPERTS), D_MODEL ** -0.5),
        'b_router': _normal(ks[23], (N_MOE, N_EXPERTS), 0.01),
        'w_ex_gate': _normal(ks[24], (N_MOE, N_EXPERTS, D_MODEL, D_FF_EXPERT), D_MODEL ** -0.5),
        'w_ex_up': _normal(ks[25], (N_MOE, N_EXPERTS, D_MODEL, D_FF_EXPERT), D_MODEL ** -0.5),
        'w_ex_down': _normal(ks[26], (N_MOE, N_EXPERTS, D_FF_EXPERT, D_MODEL), D_FF_EXPERT ** -0.5),
        'norm_ple': 1.0 + _normal(ks[27], (DEPTH, D_MODEL), 0.01),
        'w_ple_gate': _normal(ks[28], (DEPTH, D_MODEL, D_MODEL), D_MODEL ** -0.5),
        'w_ple_proj': _normal(ks[29], (DEPTH, PLE_DIM, D_MODEL), PLE_DIM ** -0.5),
        'norm_final': 1.0 + _normal(ks[30], (D_MODEL,), 0.01),
    }


def reference(x_prompt, x_sample, p_prompt, p_sample, cache_attn_k, cache_attn_v, state_conv_b,
              state_conv_c, w_in, rel_bias, w_dw_b, b_dw_b, ln_g_b, ln_b_b, w_dw_c, g_out, w_out,
              norm_mix, norm_ffn, w_ff_gate, w_ff_up, w_ff_down, w_router, b_router, w_ex_gate,
              w_ex_up, w_ex_down, norm_ple, w_ple_gate, w_ple_proj, norm_final):
    params = (w_in, rel_bias, w_dw_b, b_dw_b, ln_g_b, ln_b_b, w_dw_c, g_out, w_out,
              norm_mix, norm_ffn, w_ff_gate, w_ff_up, w_ff_down, w_router, b_router,
              w_ex_gate, w_ex_up, w_ex_down, norm_ple, w_ple_gate, w_ple_proj, norm_final)
    y_prompt, k_p, v_p, b_p, c_p = trunk(x_prompt, p_prompt, None, None, None, None, params)
    y_sample, k_s, v_s, b_s, c_s = trunk(x_sample, p_sample, cache_attn_k, cache_attn_v,
                                         state_conv_b, state_conv_c, params)
    return (y_prompt, y_sample, k_p, v_p, b_p, c_p, k_s, v_s, b_s, c_s)
```

```python
import functools

import jax
import jax.numpy as jnp
from jax import lax
from jax.experimental import pallas as pl
from jax.experimental.pallas import tpu as pltpu

F32 = jnp.float32
BF16 = jnp.bfloat16

CHUNK = 64
BAND = 512
HEADS = 8
HEAD_DIM = 64
A_WIDTH = HEADS * HEAD_DIM
B_WIDTH = 256
C_WIDTH = 256
B_KERNEL = 31
C_KERNEL = 3
REL_CLIP = 128
N_EXPERTS = 8
EPS = 1e-6
NEG = -1e30

LANES = 128
ROW_SPLIT = 8
QTILE = 512
PAIR_ROWS = 2 * CHUNK
WIN = BAND + PAIR_ROWS
HIST_B = 32
HIST_C = 8
EXPERT_TILE = 512
COPY_TILE = 512
VMEM_LIMIT = 56 * 1024 * 1024


def _params(*sem):
    return pltpu.CompilerParams(dimension_semantics=sem, vmem_limit_bytes=VMEM_LIMIT)


def _rms(x, g):
    return x * lax.rsqrt(jnp.mean(x * x, axis=-1, keepdims=True) + EPS) * g


def _sigmoid(x):
    return 1.0 / (1.0 + jnp.exp(-x))


def _inproj_kernel(x_ref, g_ref, w_ref, u_ref, kv_ref):
    hn = _rms(x_ref[...], g_ref[...]).astype(BF16)
    q = jnp.dot(hn, w_ref[:, :A_WIDTH], preferred_element_type=F32)
    u_ref[:, :A_WIDTH] = (q * (HEAD_DIM ** -0.5)).astype(BF16)
    kv = jnp.dot(hn, w_ref[:, A_WIDTH:3 * A_WIDTH], preferred_element_type=F32)
    kv_ref[...] = kv
    u_ref[:, A_WIDTH:3 * A_WIDTH] = kv.astype(BF16)
    rest = jnp.dot(hn, w_ref[:, 3 * A_WIDTH:], preferred_element_type=F32)
    u_ref[:, 3 * A_WIDTH:] = rest.astype(BF16)


def _inproj(x, g, w, tm):
    n, d = x.shape
    wi = w.shape[1]
    return pl.pallas_call(
        _inproj_kernel,
        grid=(n // tm,),
        in_specs=[pl.BlockSpec((tm, d), lambda i: (i, 0)),
                  pl.BlockSpec((1, d), lambda i: (0, 0)),
                  pl.BlockSpec((d, wi), lambda i: (0, 0))],
        out_specs=[pl.BlockSpec((tm, wi), lambda i: (i, 0)),
                   pl.BlockSpec((tm, 2 * A_WIDTH), lambda i: (i, 0))],
        out_shape=[jax.ShapeDtypeStruct((n, wi), BF16),
                   jax.ShapeDtypeStruct((n, 2 * A_WIDTH), F32)],
        compiler_params=_params("arbitrary"),
    )(x, g, w)


def _attn_kernel(q_ref, kp_ref, kc_ref, vp_ref, vc_ref, bias_ref, o_ref, k_scr, vt_scr):
    i = pl.program_id(1)
    p = pl.program_id(2)
    k_scr[0:QTILE, :] = kp_ref[...]
    k_scr[QTILE:2 * QTILE, :] = kc_ref[...]
    vt_scr[:, 0:QTILE] = vp_ref[...].astype(F32).T.astype(BF16)
    vt_scr[:, QTILE:2 * QTILE] = vc_ref[...].astype(F32).T.astype(BF16)
    lane = lax.broadcasted_iota(jnp.int32, (PAIR_ROWS, LANES), 1)
    row = lax.broadcasted_iota(jnp.int32, (WIN, LANES), 0)

    def tile(first):
        for c2 in range(QTILE // PAIR_ROWS):
            r0 = c2 * PAIR_ROWS
            kwin = k_scr[r0:r0 + WIN, :]
            vwin = vt_scr[:, r0:r0 + WIN]
            q2 = q_ref[r0:r0 + PAIR_ROWS, :]
            halves = []
            for hh in range(2):
                qm = jnp.where((lane >= HEAD_DIM) == bool(hh), q2, jnp.zeros_like(q2))
                s = lax.dot_general(kwin, qm, (((1,), (1,)), ((), ())),
                                    preferred_element_type=F32)
                s = s + bias_ref[2 * p + hh]
                if first:
                    s = jnp.where(row < BAND - r0, NEG, s)
                m = jnp.max(s, axis=0, keepdims=True)
                e = jnp.exp(s - m)
                l = jnp.sum(e, axis=0, keepdims=True)
                ot = jnp.dot(vwin, e.astype(BF16), preferred_element_type=F32)
                ot = ot * (1.0 / l)
                halves.append(ot[hh * HEAD_DIM:(hh + 1) * HEAD_DIM, :])
            o = jnp.concatenate(halves, axis=0).T
            o_ref[r0:r0 + PAIR_ROWS, :] = o.astype(o_ref.dtype)

    @pl.when(i == 0)
    def _():
        tile(True)

    @pl.when(i > 0)
    def _():
        tile(False)


def _prompt_attention(u, bias_t, batch, seq):
    nt = seq // QTILE
    npair = HEADS // 2
    kblk = A_WIDTH // LANES
    vblk = 2 * A_WIDTH // LANES
    return pl.pallas_call(
        _attn_kernel,
        grid=(batch, nt, npair),
        in_specs=[
            pl.BlockSpec((QTILE, LANES), lambda b, i, p: (b * nt + i, p)),
            pl.BlockSpec((QTILE, LANES), lambda b, i, p: (b * nt + jnp.maximum(i - 1, 0), kblk + p)),
            pl.BlockSpec((QTILE, LANES), lambda b, i, p: (b * nt + i, kblk + p)),
            pl.BlockSpec((QTILE, LANES), lambda b, i, p: (b * nt + jnp.maximum(i - 1, 0), vblk + p)),
            pl.BlockSpec((QTILE, LANES), lambda b, i, p: (b * nt + i, vblk + p)),
            pl.BlockSpec((HEADS, WIN, LANES), lambda b, i, p: (0, 0, 0)),
        ],
        out_specs=pl.BlockSpec((QTILE, LANES), lambda b, i, p: (b * nt + i, p)),
        out_shape=jax.ShapeDtypeStruct((batch * seq, A_WIDTH), BF16),
        scratch_shapes=[pltpu.VMEM((2 * QTILE, LANES), BF16),
                        pltpu.VMEM((LANES, 2 * QTILE), BF16)],
        compiler_params=_params("arbitrary", "arbitrary", "arbitrary"),
    )(u, u, u, u, u, bias_t)


def _prompt_bias_table(rel_bias):
    j = jnp.arange(WIN, dtype=jnp.int32)[:, None]
    q = jnp.arange(PAIR_ROWS, dtype=jnp.int32)[None, :]
    rel = jnp.clip(j - BAND - q, -REL_CLIP, REL_CLIP) + REL_CLIP
    lo = (q // CHUNK) * CHUNK
    valid = (j >= lo) & (j < lo + BAND + CHUNK)
    return jnp.where(valid[None], rel_bias[:, rel].astype(F32), NEG)


def _sample_attn_kernel(q_ref, kn_ref, vn_ref, ck_ref, cv_ref, bc_ref, bn_ref, o_ref):
    t = q_ref.shape[0]
    lane = lax.broadcasted_iota(jnp.int32, (t, LANES), 1)
    kc = ck_ref[0].astype(BF16)
    vc = cv_ref[0].astype(BF16)
    for p in range(HEADS // 2):
        cols = slice(p * LANES, (p + 1) * LANES)
        q2 = q_ref[:, cols]
        acc = None
        for hh in range(2):
            h = 2 * p + hh
            sel = (lane >= HEAD_DIM) == bool(hh)
            qm = jnp.where(sel, q2, jnp.zeros_like(q2))
            nt_dims = (((1,), (1,)), ((), ()))
            s_c = lax.dot_general(qm, kc[:, cols], nt_dims, preferred_element_type=F32) + bc_ref[h]
            s_n = lax.dot_general(qm, kn_ref[:, cols], nt_dims, preferred_element_type=F32) + bn_ref[h]
            m = jnp.maximum(jnp.max(s_c, axis=-1, keepdims=True), jnp.max(s_n, axis=-1, keepdims=True))
            e_c = jnp.exp(s_c - m)
            e_n = jnp.exp(s_n - m)
            l = jnp.sum(e_c, axis=-1, keepdims=True) + jnp.sum(e_n, axis=-1, keepdims=True)
            pv = (jnp.dot(e_c.astype(BF16), vc[:, cols], preferred_element_type=F32)
                  + jnp.dot(e_n.astype(BF16), vn_ref[:, cols], preferred_element_type=F32)) * (1.0 / l)
            acc = pv if acc is None else jnp.where(sel, pv, acc)
        o_ref[:, cols] = acc.astype(o_ref.dtype)


def _sample_attention(u, cache_k, cache_v, bias_c, bias_n, batch, t):
    n_past = cache_k.shape[1]
    return pl.pallas_call(
        _sample_attn_kernel,
        grid=(batch,),
        in_specs=[
            pl.BlockSpec((t, A_WIDTH), lambda b: (b, 0)),
            pl.BlockSpec((t, A_WIDTH), lambda b: (b, 1)),
            pl.BlockSpec((t, A_WIDTH), lambda b: (b, 2)),
            pl.BlockSpec((1, n_past, A_WIDTH), lambda b: (b, 0, 0)),
            pl.BlockSpec((1, n_past, A_WIDTH), lambda b: (b, 0, 0)),
            pl.BlockSpec((HEADS, t, n_past), lambda b: (0, 0, 0)),
            pl.BlockSpec((HEADS, t, t), lambda b: (0, 0, 0)),
        ],
        out_specs=pl.BlockSpec((t, A_WIDTH), lambda b: (b, 0)),
        out_shape=jax.ShapeDtypeStruct((batch * t, A_WIDTH), BF16),
        compiler_params=_params("arbitrary"),
    )(u, u, u, cache_k, cache_v, bias_c, bias_n)


def _sample_bias_tables(rel_bias, t, n_past):
    i = jnp.arange(t, dtype=jnp.int32)[:, None]
    jc = jnp.arange(n_past, dtype=jnp.int32)[None, :]
    jn = jnp.arange(t, dtype=jnp.int32)[None, :]
    rel_c = jnp.clip(jc - n_past - i, -REL_CLIP, REL_CLIP) + REL_CLIP
    rel_n = jnp.clip(jn - i, -REL_CLIP, REL_CLIP) + REL_CLIP
    return rel_bias[:, rel_c].astype(F32), rel_bias[:, rel_n].astype(F32)


def _mix_kernel(x_ref, ya_ref, ab_ref, gc_ref, gb_ref, hc_ref, hb0_ref, hc0_ref,
                wb_ref, bb_ref, lg_ref, lb_ref, wc_ref, go_ref, wo_ref,
                h_ref, nb_ref, nc_ref, cb, cc):
    tm = x_ref.shape[0]

    @pl.when(pl.program_id(1) == 0)
    def _():
        cb[0:HIST_B, :] = hb0_ref[0]
        cc[0:HIST_C, :] = hc0_ref[0]

    ab = ab_ref[...].astype(F32)
    cb[HIST_B:HIST_B + tm, :] = ab[:, :B_WIDTH] * _sigmoid(ab[:, B_WIDTH:])
    z = jnp.zeros((tm, B_WIDTH), F32) + bb_ref[...]
    for j in range(B_KERNEL):
        z = z + wb_ref[j:j + 1, :] * cb[pl.ds(j + HIST_B - (B_KERNEL - 1), tm), :]
    mu = jnp.mean(z, axis=-1, keepdims=True)
    zc = z - mu
    var = jnp.mean(zc * zc, axis=-1, keepdims=True)
    yb = zc * lax.rsqrt(var + EPS) * lg_ref[...] + lb_ref[...]
    yb = yb * _sigmoid(yb)

    cc[HIST_C:HIST_C + tm, :] = gc_ref[...].astype(F32) * hc_ref[...].astype(F32)
    conv = jnp.zeros((tm, C_WIDTH), F32)
    for j in range(C_KERNEL):
        conv = conv + wc_ref[j:j + 1, :] * cc[pl.ds(j + HIST_C - (C_KERNEL - 1), tm), :]
    yc = gb_ref[...].astype(F32) * conv

    go = go_ref[...]
    na = _rms(ya_ref[...].astype(F32), go[:, :A_WIDTH]).astype(BF16)
    nb = _rms(yb, go[:, A_WIDTH:A_WIDTH + B_WIDTH]).astype(BF16)
    nc = _rms(yc, go[:, A_WIDTH + B_WIDTH:]).astype(BF16)
    h_ref[...] = (x_ref[...]
                  + jnp.dot(na, wo_ref[0:A_WIDTH, :], preferred_element_type=F32)
                  + jnp.dot(nb, wo_ref[A_WIDTH:A_WIDTH + B_WIDTH, :], preferred_element_type=F32)
                  + jnp.dot(nc, wo_ref[A_WIDTH + B_WIDTH:, :], preferred_element_type=F32))

    tail_b = cb[tm:tm + HIST_B, :]
    tail_c = cc[tm:tm + HIST_C, :]
    nb_ref[0] = tail_b
    nc_ref[0] = tail_c
    cb[0:HIST_B, :] = tail_b
    cc[0:HIST_C, :] = tail_c


def _mix(x, ya, u, hist_b, hist_c, wb, bb, lg, lb, wc, go, wo, batch, t, tm):
    n, d = x.shape
    nt = t // tm
    row = lambda b, i: b * nt + i
    ab_blk = 3 * A_WIDTH // (2 * B_WIDTH)
    c_blk = (3 * A_WIDTH + 2 * B_WIDTH) // C_WIDTH
    const = lambda b, i: (0, 0)
    return pl.pallas_call(
        _mix_kernel,
        grid=(batch, nt),
        in_specs=[
            pl.BlockSpec((tm, d), lambda b, i: (row(b, i), 0)),
            pl.BlockSpec((tm, A_WIDTH), lambda b, i: (row(b, i), 0)),
            pl.BlockSpec((tm, 2 * B_WIDTH), lambda b, i: (row(b, i), ab_blk)),
            pl.BlockSpec((tm, C_WIDTH), lambda b, i: (row(b, i), c_blk)),
            pl.BlockSpec((tm, C_WIDTH), lambda b, i: (row(b, i), c_blk + 1)),
            pl.BlockSpec((tm, C_WIDTH), lambda b, i: (row(b, i), c_blk + 2)),
            pl.BlockSpec((1, HIST_B, B_WIDTH), lambda b, i: (b, 0, 0)),
            pl.BlockSpec((1, HIST_C, C_WIDTH), lambda b, i: (b, 0, 0)),
            pl.BlockSpec(wb.shape, const), pl.BlockSpec(bb.shape, const),
            pl.BlockSpec(lg.shape, const), pl.BlockSpec(lb.shape, const),
            pl.BlockSpec(wc.shape, const), pl.BlockSpec(go.shape, const),
            pl.BlockSpec(wo.shape, const),
        ],
        out_specs=[pl.BlockSpec((tm, d), lambda b, i: (row(b, i), 0)),
                   pl.BlockSpec((1, HIST_B, B_WIDTH), lambda b, i: (b, 0, 0)),
                   pl.BlockSpec((1, HIST_C, C_WIDTH), lambda b, i: (b, 0, 0))],
        out_shape=[jax.ShapeDtypeStruct((n, d), F32),
                   jax.ShapeDtypeStruct((batch, HIST_B, B_WIDTH), F32),
                   jax.ShapeDtypeStruct((batch, HIST_C, C_WIDTH), F32)],
        scratch_shapes=[pltpu.VMEM((HIST_B + tm, B_WIDTH), F32),
                        pltpu.VMEM((HIST_C + tm, C_WIDTH), F32)],
        compiler_params=_params("arbitrary", "arbitrary"),
    )(x, ya, u, u, u, u, hist_b, hist_c, wb, bb, lg, lb, wc, go, wo)


def _ple_epilogue(h, p, gp, wpg, wpp, gf):
    gate = _sigmoid(jnp.dot(_rms(h, gp).astype(BF16), wpg, preferred_element_type=F32))
    h = h + gate * jnp.dot(p.astype(BF16), wpp, preferred_element_type=F32)
    return h if gf is None else _rms(h, gf)


def _ffn_kernel(final, h_ref, g_ref, wg_ref, wu_ref, wd_ref, p_ref, gp_ref, wpg_ref, wpp_ref, gf_ref,
                o_ref, xn, acc):
    j = pl.program_id(1)

    @pl.when(j == 0)
    def _():
        xn[...] = _rms(h_ref[...], g_ref[...]).astype(BF16)
        acc[...] = jnp.zeros_like(acc)

    a = jnp.dot(xn[...], wg_ref[...], preferred_element_type=F32)
    b = jnp.dot(xn[...], wu_ref[...], preferred_element_type=F32)
    mid = (a * _sigmoid(a) * b).astype(BF16)
    acc[...] += jnp.dot(mid, wd_ref[...], preferred_element_type=F32)

    @pl.when(j == pl.num_programs(1) - 1)
    def _():
        h = h_ref[...] + acc[...]
        o_ref[...] = _ple_epilogue(h, p_ref[...], gp_ref[...], wpg_ref[...], wpp_ref[...],
                                   gf_ref[...] if final else None)


def _ffn(h, g, wg, wu, wd, p, gp, wpg, wpp, gf, final, tm, tf):
    n, d = h.shape
    f = wg.shape[1]
    const = lambda i, j: (0, 0)
    return pl.pallas_call(
        functools.partial(_ffn_kernel, final),
        grid=(n // tm, f // tf),
        in_specs=[
            pl.BlockSpec((tm, d), lambda i, j: (i, 0)),
            pl.BlockSpec((1, d), const),
            pl.BlockSpec((d, tf), lambda i, j: (0, j)),
            pl.BlockSpec((d, tf), lambda i, j: (0, j)),
            pl.BlockSpec((tf, d), lambda i, j: (j, 0)),
            pl.BlockSpec((tm, p.shape[1]), lambda i, j: (i, 0)),
            pl.BlockSpec((1, d), const),
            pl.BlockSpec(wpg.shape, const),
            pl.BlockSpec(wpp.shape, const),
            pl.BlockSpec((1, d), const),
        ],
        out_specs=pl.BlockSpec((tm, d), lambda i, j: (i, 0)),
        out_shape=jax.ShapeDtypeStruct((n, d), F32),
        scratch_shapes=[pltpu.VMEM((tm, d), BF16), pltpu.VMEM((tm, d), F32)],
        compiler_params=_params("arbitrary", "arbitrary"),
    )(h, g, wg, wu, wd, p, gp, wpg, wpp, gf)


def _router_kernel(h_ref, g_ref, wr_ref, br_ref, x3_ref, meta_ref, cnt_ref, carry):
    tm = h_ref.shape[0]

    @pl.when(pl.program_id(0) == 0)
    def _():
        carry[...] = jnp.zeros_like(carry)

    xn = _rms(h_ref[...], g_ref[...])
    for s in range(ROW_SPLIT):
        x3_ref[:, s, :] = xn[:, s * LANES:(s + 1) * LANES]
    logits = lax.dot_general(wr_ref[...], xn, (((1,), (1,)), ((), ())),
                             precision=lax.Precision.HIGHEST,
                             preferred_element_type=F32) + br_ref[...]
    eidx = lax.broadcasted_iota(jnp.int32, (N_EXPERTS, tm), 0)
    m1 = jnp.max(logits, axis=0, keepdims=True)
    i1 = jnp.min(jnp.where(logits == m1, eidx, N_EXPERTS), axis=0, keepdims=True)
    rest = jnp.where(eidx == i1, -jnp.inf, logits)
    m2 = jnp.max(rest, axis=0, keepdims=True)
    i2 = jnp.min(jnp.where(rest == m2, eidx, N_EXPERTS), axis=0, keepdims=True)
    e2 = jnp.exp(m2 - m1)
    g1 = 1.0 / (1.0 + e2)
    g2 = e2 / (1.0 + e2)
    hit1 = eidx == i1
    hit2 = eidx == i2
    chosen = jnp.where(hit1 | hit2, 1.0, 0.0)
    src = lax.broadcasted_iota(jnp.int32, (tm, tm), 0)
    dst = lax.broadcasted_iota(jnp.int32, (tm, tm), 1)
    before = jnp.where(src < dst, 1.0, 0.0).astype(BF16)
    rank = jnp.dot(chosen.astype(BF16), before, preferred_element_type=F32) + carry[:, 0:1]
    r1 = jnp.sum(jnp.where(hit1, rank, 0.0), axis=0, keepdims=True)
    r2 = jnp.sum(jnp.where(hit2, rank, 0.0), axis=0, keepdims=True)
    zero = jnp.zeros_like(g1)
    meta_ref[...] = jnp.concatenate(
        [i1.astype(F32), i2.astype(F32), g1, g2, r1, r2, zero, zero], axis=0)
    carry[...] = carry[...] + jnp.sum(chosen, axis=1, keepdims=True)
    cnt_ref[...] = carry[...]


def _router(h, g, wr_t, br, tm):
    n, d = h.shape
    return pl.pallas_call(
        _router_kernel,
        grid=(n // tm,),
        in_specs=[pl.BlockSpec((tm, d), lambda i: (i, 0)),
                  pl.BlockSpec((1, d), lambda i: (0, 0)),
                  pl.BlockSpec((N_EXPERTS, d), lambda i: (0, 0)),
                  pl.BlockSpec((N_EXPERTS, 1), lambda i: (0, 0))],
        out_specs=[pl.BlockSpec((tm, ROW_SPLIT, LANES), lambda i: (i, 0, 0)),
                   pl.BlockSpec((8, tm), lambda i: (0, i)),
                   pl.BlockSpec((N_EXPERTS, LANES), lambda i: (0, 0))],
        out_shape=[jax.ShapeDtypeStruct((n, ROW_SPLIT, LANES), F32),
                   jax.ShapeDtypeStruct((8, n), F32),
                   jax.ShapeDtypeStruct((N_EXPERTS, LANES), F32)],
        scratch_shapes=[pltpu.VMEM((N_EXPERTS, LANES), F32)],
        compiler_params=_params("arbitrary"),
    )(h, g, wr_t, br)


def _dispatch_kernel(src_ref, dst_ref, x_hbm, o_hbm, sem):
    def issue(m, carry):
        pltpu.make_async_copy(x_hbm.at[src_ref[0, 0, m]], o_hbm.at[dst_ref[0, 0, m]], sem).start()
        return carry

    lax.fori_loop(0, COPY_TILE, issue, 0)
    pltpu.make_async_copy(x_hbm.at[pl.ds(0, COPY_TILE)], o_hbm.at[pl.ds(0, COPY_TILE)], sem).wait()


def _dispatch(x3, src, dst, rows_out):
    m = src.shape[0]
    steps = m // COPY_TILE
    idx_spec = pl.BlockSpec((1, 1, COPY_TILE), lambda i: (i, 0, 0), memory_space=pltpu.SMEM)
    return pl.pallas_call(
        _dispatch_kernel,
        grid=(steps,),
        in_specs=[idx_spec, idx_spec, pl.BlockSpec(memory_space=pl.ANY)],
        out_specs=pl.BlockSpec(memory_space=pl.ANY),
        out_shape=jax.ShapeDtypeStruct((rows_out, ROW_SPLIT, LANES), x3.dtype),
        scratch_shapes=[pltpu.SemaphoreType.DMA(())],
        compiler_params=_params("arbitrary"),
    )(src.reshape(steps, 1, COPY_TILE), dst.reshape(steps, 1, COPY_TILE), x3)


def _expert_kernel(texp_ref, nact_ref, x_ref, wg_ref, wu_ref, wd_ref, y_ref, x2, acc):
    del texp_ref

    @pl.when(pl.program_id(0) < nact_ref[0])
    def _():
        for s in range(ROW_SPLIT):
            x2[:, s * LANES:(s + 1) * LANES] = x_ref[:, s, :].astype(BF16)
        f = wg_ref.shape[2]
        fc = 512
        for c in range(f // fc):
            a = jnp.dot(x2[...], wg_ref[0, :, c * fc:(c + 1) * fc], preferred_element_type=F32)
            b = jnp.dot(x2[...], wu_ref[0, :, c * fc:(c + 1) * fc], preferred_element_type=F32)
            mid = (a * _sigmoid(a) * b).astype(BF16)
            part = jnp.dot(mid, wd_ref[0, c * fc:(c + 1) * fc, :], preferred_element_type=F32)
            if c == 0:
                acc[...] = part
            else:
                acc[...] += part
        for s in range(ROW_SPLIT):
            y_ref[:, s, :] = acc[:, s * LANES:(s + 1) * LANES]

    @pl.when(pl.program_id(0) >= nact_ref[0])
    def _():
        y_ref[...] = jnp.zeros_like(y_ref)


def _experts(xs, tile_exp, n_active, wg, wu, wd):
    rows = xs.shape[0]
    ntile = tile_exp.shape[0]
    d, f = wg.shape[1], wg.shape[2]
    grid_spec = pltpu.PrefetchScalarGridSpec(
        num_scalar_prefetch=2,
        grid=(ntile,),
        in_specs=[
            pl.BlockSpec((EXPERT_TILE, ROW_SPLIT, LANES), lambda t, te, na: (t, 0, 0)),
            pl.BlockSpec((1, d, f), lambda t, te, na: (te[t], 0, 0)),
            pl.BlockSpec((1, d, f), lambda t, te, na: (te[t], 0, 0)),
            pl.BlockSpec((1, f, d), lambda t, te, na: (te[t], 0, 0)),
        ],
        out_specs=pl.BlockSpec((EXPERT_TILE, ROW_SPLIT, LANES), lambda t, te, na: (t, 0, 0)),
        scratch_shapes=[pltpu.VMEM((EXPERT_TILE, d), BF16), pltpu.VMEM((EXPERT_TILE, d), F32)],
    )
    return pl.pallas_call(
        _expert_kernel,
        grid_spec=grid_spec,
        out_shape=jax.ShapeDtypeStruct((rows, ROW_SPLIT, LANES), F32),
        compiler_params=_params("arbitrary"),
    )(tile_exp, n_active, xs, wg, wu, wd)


def _combine_kernel(final, d1_ref, d2_ref, ye_hbm, h_ref, gate_ref, p_ref, gp_ref, wpg_ref, wpp_ref, gf_ref,
                    o_ref, y1, y2, moe, sem):
    tm = h_ref.shape[0]

    def issue(t, carry):
        pltpu.make_async_copy(ye_hbm.at[d1_ref[0, 0, t]], y1.at[t], sem).start()
        pltpu.make_async_copy(ye_hbm.at[d2_ref[0, 0, t]], y2.at[t], sem).start()
        return carry

    lax.fori_loop(0, tm, issue, 0)
    pltpu.make_async_copy(ye_hbm.at[pl.ds(0, tm)], y1, sem).wait()
    pltpu.make_async_copy(ye_hbm.at[pl.ds(0, tm)], y2, sem).wait()
    g1 = gate_ref[:, 0:1]
    g2 = gate_ref[:, 1:2]
    for s in range(ROW_SPLIT):
        moe[:, s * LANES:(s + 1) * LANES] = g1 * y1[:, s, :] + g2 * y2[:, s, :]
    h = h_ref[...] + moe[...]
    o_ref[...] = _ple_epilogue(h, p_ref[...], gp_ref[...], wpg_ref[...], wpp_ref[...],
                               gf_ref[...] if final else None)


def _combine(ye, d1, d2, h, gates, p, gp, wpg, wpp, gf, final, tm):
    n, d = h.shape
    steps = n // tm
    idx_spec = pl.BlockSpec((1, 1, tm), lambda i: (i, 0, 0), memory_space=pltpu.SMEM)
    const = lambda i: (0, 0)
    return pl.pallas_call(
        functools.partial(_combine_kernel, final),
        grid=(steps,),
        in_specs=[idx_spec, idx_spec,
                  pl.BlockSpec(memory_space=pl.ANY),
                  pl.BlockSpec((tm, d), lambda i: (i, 0)),
                  pl.BlockSpec((tm, 2), lambda i: (i, 0)),
                  pl.BlockSpec((tm, p.shape[1]), lambda i: (i, 0)),
                  pl.BlockSpec((1, d), const),
                  pl.BlockSpec(wpg.shape, const),
                  pl.BlockSpec(wpp.shape, const),
                  pl.BlockSpec((1, d), const)],
        out_specs=pl.BlockSpec((tm, d), lambda i: (i, 0)),
        out_shape=jax.ShapeDtypeStruct((n, d), F32),
        scratch_shapes=[pltpu.VMEM((tm, ROW_SPLIT, LANES), F32),
                        pltpu.VMEM((tm, ROW_SPLIT, LANES), F32),
                        pltpu.VMEM((tm, d), F32),
                        pltpu.SemaphoreType.DMA(())],
        compiler_params=_params("arbitrary"),
    )(d1.reshape(steps, 1, tm), d2.reshape(steps, 1, tm), ye, h, gates, p, gp, wpg, wpp, gf)


def _moe(h, g, wr_t, br, wg, wu, wd, p, gp, wpg, wpp, gf, final, tm):
    n = h.shape[0]
    x3, meta, cnt = _router(h, g, wr_t, br, tm)
    counts = cnt[:, 0].astype(jnp.int32)
    tiles = (counts + EXPERT_TILE - 1) // EXPERT_TILE
    padded = tiles * EXPERT_TILE
    ends = jnp.cumsum(padded)
    offs = ends - padded
    i1 = meta[0].astype(jnp.int32)
    i2 = meta[1].astype(jnp.int32)
    d1 = offs[i1] + meta[4].astype(jnp.int32)
    d2 = offs[i2] + meta[5].astype(jnp.int32)
    gates = jnp.stack([meta[2], meta[3]], axis=1)
    ntile = 2 * n // EXPERT_TILE + N_EXPERTS
    rows = ntile * EXPERT_TILE
    r = jnp.arange(EXPERT_TILE, dtype=jnp.int32)[None, :]
    in_group = counts[:, None] + r < padded[:, None]
    spare = ends[-1] + (jnp.cumsum(jnp.where(in_group, 0, 1).reshape(-1)) - 1).reshape(in_group.shape)
    fill = jnp.where(in_group, offs[:, None] + counts[:, None] + r, spare).reshape(-1).astype(jnp.int32)
    tok = jnp.arange(n, dtype=jnp.int32)
    src = jnp.concatenate([tok, tok, jnp.zeros_like(fill)])
    dst = jnp.concatenate([d1, d2, fill])
    xs = _dispatch(x3, src, dst, rows)
    n_active = jnp.sum(tiles).astype(jnp.int32)
    tile_id = jnp.minimum(jnp.arange(ntile, dtype=jnp.int32), n_active - 1)
    tile_exp = jnp.minimum(jnp.searchsorted(ends, tile_id * EXPERT_TILE, side="right"),
                           N_EXPERTS - 1).astype(jnp.int32)
    ye = _experts(xs, tile_exp, n_active.reshape(1), wg, wu, wd)
    return _combine(ye, d1, d2, h, gates, p, gp, wpg, wpp, gf, final, tm)


def _trunk(x, p, past, prm, tok_tile, mix_tile):
    batch, t, d = x.shape
    n = batch * t
    depth = prm["w_in"].shape[0]
    h = x.reshape(n, d)
    new_k, new_v, new_b, new_c = [], [], [], []
    row = lambda a: a.reshape(1, -1)
    for i in range(depth):
        u, kv = _inproj(h, row(prm["norm_mix"][i]), prm["w_in"][i], tok_tile)
        kv = kv.reshape(batch, t, 2, HEADS, HEAD_DIM)
        if past is None:
            ya = _prompt_attention(u, _prompt_bias_table(prm["rel_bias"][i]), batch, t)
            keep = min(BAND, t)
            new_k.append(kv[:, t - keep:, 0])
            new_v.append(kv[:, t - keep:, 1])
            hist_b = jnp.zeros((batch, HIST_B, B_WIDTH), F32)
            hist_c = jnp.zeros((batch, HIST_C, C_WIDTH), F32)
        else:
            ck, cv, sb, sc = past
            n_past = ck.shape[2]
            bias_c, bias_n = _sample_bias_tables(prm["rel_bias"][i], t, n_past)
            ya = _sample_attention(u, ck[i].reshape(batch, n_past, A_WIDTH),
                                   cv[i].reshape(batch, n_past, A_WIDTH), bias_c, bias_n, batch, t)
            new_k.append(kv[:, :, 0])
            new_v.append(kv[:, :, 1])
            hist_b = jnp.pad(sb[i], ((0, 0), (HIST_B - (B_KERNEL - 1), 0), (0, 0)))
            hist_c = jnp.pad(sc[i], ((0, 0), (HIST_C - (C_KERNEL - 1), 0), (0, 0)))
        wb = jnp.pad(prm["w_dw_b"][i], ((0, HIST_B - B_KERNEL), (0, 0)))
        wc = jnp.pad(prm["w_dw_c"][i], ((0, HIST_C - C_KERNEL), (0, 0)))
        h, nb, nc = _mix(h, ya, u, hist_b, hist_c, wb, row(prm["b_dw_b"][i]), row(prm["ln_g_b"][i]),
                         row(prm["ln_b_b"][i]), wc, row(prm["g_out"][i]), prm["w_out"][i],
                         batch, t, mix_tile)
        new_b.append(nb[:, HIST_B - (B_KERNEL - 1):])
        new_c.append(nc[:, HIST_C - (C_KERNEL - 1):])
        final = i == depth - 1
        tail = (p[i].reshape(n, -1), row(prm["norm_ple"][i]), prm["w_ple_gate"][i], prm["w_ple_proj"][i],
                row(prm["norm_final"]), final)
        j = i // 2
        if i % 2 == 0:
            h = _ffn(h, row(prm["norm_ffn"][i]), prm["w_ff_gate"][j], prm["w_ff_up"][j], prm["w_ff_down"][j],
                     *tail, tok_tile, 512)
        else:
            h = _moe(h, row(prm["norm_ffn"][i]), prm["w_router"][j].T, prm["b_router"][j].reshape(-1, 1),
                     prm["w_ex_gate"][j], prm["w_ex_up"][j], prm["w_ex_down"][j], *tail, tok_tile)
    return (h.reshape(batch, t, d), jnp.stack(new_k), jnp.stack(new_v), jnp.stack(new_b), jnp.stack(new_c))


def kernel(x_prompt, x_sample, p_prompt, p_sample, cache_attn_k, cache_attn_v, state_conv_b, state_conv_c,
           w_in, rel_bias, w_dw_b, b_dw_b, ln_g_b, ln_b_b, w_dw_c, g_out, w_out, norm_mix, norm_ffn,
           w_ff_gate, w_ff_up, w_ff_down, w_router, b_router, w_ex_gate, w_ex_up, w_ex_down,
           norm_ple, w_ple_gate, w_ple_proj, norm_final):
    prm = dict(
        w_in=w_in.astype(BF16), rel_bias=rel_bias, w_dw_b=w_dw_b, b_dw_b=b_dw_b, ln_g_b=ln_g_b,
        ln_b_b=ln_b_b, w_dw_c=w_dw_c, g_out=g_out, w_out=w_out.astype(BF16), norm_mix=norm_mix,
        norm_ffn=norm_ffn, w_ff_gate=w_ff_gate.astype(BF16), w_ff_up=w_ff_up.astype(BF16),
        w_ff_down=w_ff_down.astype(BF16), w_router=w_router, b_router=b_router,
        w_ex_gate=w_ex_gate.astype(BF16), w_ex_up=w_ex_up.astype(BF16), w_ex_down=w_ex_down.astype(BF16),
        norm_ple=norm_ple, w_ple_gate=w_ple_gate.astype(BF16), w_ple_proj=w_ple_proj.astype(BF16),
        norm_final=norm_final)
    dec_t = x_sample.shape[1]
    y_p, k_p, v_p, b_p, c_p = _trunk(x_prompt, p_prompt, None, prm, 512, 512)
    y_s, k_s, v_s, b_s, c_s = _trunk(x_sample, p_sample,
                                     (cache_attn_k, cache_attn_v, state_conv_b, state_conv_c),
                                     prm, x_sample.shape[0] * dec_t, dec_t)
    return (y_p, y_s, k_p, v_p, b_p, c_p, k_s, v_s, b_s, c_s)
```

```python
import functools

import jax
import jax.numpy as jnp
from jax import lax
from jax.experimental import pallas as pl
from jax.experimental.pallas import tpu as pltpu

F32 = jnp.float32
BF16 = jnp.bfloat16

CHUNK = 64
BAND = 512
HEADS = 8
HEAD_DIM = 64
A_WIDTH = HEADS * HEAD_DIM
B_WIDTH = 256
C_WIDTH = 256
B_KERNEL = 31
C_KERNEL = 3
REL_CLIP = 128
N_EXPERTS = 8
EPS = 1e-6
NEG = -1e30

LANES = 128
ROW_SPLIT = 8
QTILE = 512
PAIR_ROWS = 2 * CHUNK
WIN = BAND + PAIR_ROWS
HIST_B = 32
HIST_C = 8
EXPERT_TILE = 512
FF_CHUNK = 512
DMA_UNROLL = 8
VMEM_LIMIT = 56 * 1024 * 1024


def _params(*sem):
    return pltpu.CompilerParams(dimension_semantics=sem, vmem_limit_bytes=VMEM_LIMIT)


def _rms(x, g):
    return x * lax.rsqrt(jnp.mean(x * x, axis=-1, keepdims=True) + EPS) * g


def _sigmoid(x):
    return 1.0 / (1.0 + jnp.exp(-x))


def _inproj_kernel(x_ref, g_ref, w_ref, u_ref, kv_ref):
    hn = _rms(x_ref[...], g_ref[...]).astype(BF16)
    q = jnp.dot(hn, w_ref[:, :A_WIDTH], preferred_element_type=F32)
    u_ref[:, :A_WIDTH] = (q * (HEAD_DIM ** -0.5)).astype(BF16)
    kv = jnp.dot(hn, w_ref[:, A_WIDTH:3 * A_WIDTH], preferred_element_type=F32)
    kv_ref[...] = kv
    u_ref[:, A_WIDTH:3 * A_WIDTH] = kv.astype(BF16)
    rest = jnp.dot(hn, w_ref[:, 3 * A_WIDTH:], preferred_element_type=F32)
    u_ref[:, 3 * A_WIDTH:] = rest.astype(BF16)


def _inproj(x, g, w, tm, t, keep):
    n, d = x.shape
    wi = w.shape[1]
    if keep == t:
        kv_index = lambda i: (i, 0)
    else:
        nt, ktiles = t // tm, keep // tm
        kv_index = lambda i: ((i // nt) * ktiles + jnp.maximum(i % nt - (nt - ktiles), 0), 0)
    return pl.pallas_call(
        _inproj_kernel,
        grid=(n // tm,),
        in_specs=[pl.BlockSpec((tm, d), lambda i: (i, 0)),
                  pl.BlockSpec((1, d), lambda i: (0, 0)),
                  pl.BlockSpec((d, wi), lambda i: (0, 0))],
        out_specs=[pl.BlockSpec((tm, wi), lambda i: (i, 0)),
                   pl.BlockSpec((tm, 2 * A_WIDTH), kv_index)],
        out_shape=[jax.ShapeDtypeStruct((n, wi), BF16),
                   jax.ShapeDtypeStruct((n // t * keep, 2 * A_WIDTH), F32)],
        compiler_params=_params("arbitrary"),
    )(x, g, w)


def _attn_kernel(q_ref, kp_ref, kc_ref, vp_ref, vc_ref, bias_ref, o_ref, k_scr, vt_scr):
    i = pl.program_id(1)
    p = pl.program_id(2)
    k_scr[0:QTILE, :] = kp_ref[...]
    k_scr[QTILE:2 * QTILE, :] = kc_ref[...]
    vt_scr[:, 0:QTILE] = vp_ref[...].astype(F32).T.astype(BF16)
    vt_scr[:, QTILE:2 * QTILE] = vc_ref[...].astype(F32).T.astype(BF16)
    lane = lax.broadcasted_iota(jnp.int32, (PAIR_ROWS, LANES), 1)
    row = lax.broadcasted_iota(jnp.int32, (WIN, LANES), 0)

    def tile(first):
        for c2 in range(QTILE // PAIR_ROWS):
            r0 = c2 * PAIR_ROWS
            kwin = k_scr[r0:r0 + WIN, :]
            vwin = vt_scr[:, r0:r0 + WIN]
            q2 = q_ref[r0:r0 + PAIR_ROWS, :]
            halves = []
            for hh in range(2):
                qm = jnp.where((lane >= HEAD_DIM) == bool(hh), q2, jnp.zeros_like(q2))
                s = lax.dot_general(kwin, qm, (((1,), (1,)), ((), ())),
                                    preferred_element_type=F32)
                s = s + bias_ref[2 * p + hh]
                if first:
                    s = jnp.where(row < BAND - r0, NEG, s)
                m = jnp.max(s, axis=0, keepdims=True)
                e = jnp.exp(s - m)
                l = jnp.sum(e, axis=0, keepdims=True)
                ot = jnp.dot(vwin, e.astype(BF16), preferred_element_type=F32)
                ot = ot * (1.0 / l)
                halves.append(ot[hh * HEAD_DIM:(hh + 1) * HEAD_DIM, :])
            o = jnp.concatenate(halves, axis=0).T
            o_ref[r0:r0 + PAIR_ROWS, :] = o.astype(o_ref.dtype)

    @pl.when(i == 0)
    def _():
        tile(True)

    @pl.when(i > 0)
    def _():
        tile(False)


def _prompt_attention(u, bias_t, batch, seq):
    nt = seq // QTILE
    npair = HEADS // 2
    kblk = A_WIDTH // LANES
    vblk = 2 * A_WIDTH // LANES
    return pl.pallas_call(
        _attn_kernel,
        grid=(batch, nt, npair),
        in_specs=[
            pl.BlockSpec((QTILE, LANES), lambda b, i, p: (b * nt + i, p)),
            pl.BlockSpec((QTILE, LANES), lambda b, i, p: (b * nt + jnp.maximum(i - 1, 0), kblk + p)),
            pl.BlockSpec((QTILE, LANES), lambda b, i, p: (b * nt + i, kblk + p)),
            pl.BlockSpec((QTILE, LANES), lambda b, i, p: (b * nt + jnp.maximum(i - 1, 0), vblk + p)),
            pl.BlockSpec((QTILE, LANES), lambda b, i, p: (b * nt + i, vblk + p)),
            pl.BlockSpec((HEADS, WIN, LANES), lambda b, i, p: (0, 0, 0)),
        ],
        out_specs=pl.BlockSpec((QTILE, LANES), lambda b, i, p: (b * nt + i, p)),
        out_shape=jax.ShapeDtypeStruct((batch * seq, A_WIDTH), BF16),
        scratch_shapes=[pltpu.VMEM((2 * QTILE, LANES), BF16),
                        pltpu.VMEM((LANES, 2 * QTILE), BF16)],
        compiler_params=_params("arbitrary", "arbitrary", "arbitrary"),
    )(u, u, u, u, u, bias_t)


def _prompt_bias_table(rel_bias):
    by_dist = jnp.concatenate(
        [jnp.broadcast_to(rel_bias[:, :1], (HEADS, BAND)), rel_bias[:, :2 * REL_CLIP]], axis=1).astype(F32)
    table = jnp.stack([by_dist[:, PAIR_ROWS - q:PAIR_ROWS - q + WIN] for q in range(PAIR_ROWS)], axis=-1)
    j = jnp.arange(WIN, dtype=jnp.int32)[:, None]
    lo = (jnp.arange(PAIR_ROWS, dtype=jnp.int32)[None, :] // CHUNK) * CHUNK
    valid = (j >= lo) & (j < lo + BAND + CHUNK)
    return jnp.where(valid[None], table, NEG)


def _sample_attn_kernel(q_ref, kn_ref, vn_ref, ck_ref, cv_ref, bc_ref, bn_ref, o_ref):
    t = q_ref.shape[0]
    lane = lax.broadcasted_iota(jnp.int32, (t, LANES), 1)
    kc = ck_ref[0].astype(BF16)
    vc = cv_ref[0].astype(BF16)
    for p in range(HEADS // 2):
        cols = slice(p * LANES, (p + 1) * LANES)
        q2 = q_ref[:, cols]
        acc = None
        for hh in range(2):
            h = 2 * p + hh
            sel = (lane >= HEAD_DIM) == bool(hh)
            qm = jnp.where(sel, q2, jnp.zeros_like(q2))
            nt_dims = (((1,), (1,)), ((), ()))
            s_c = lax.dot_general(qm, kc[:, cols], nt_dims, preferred_element_type=F32) + bc_ref[h]
            s_n = lax.dot_general(qm, kn_ref[:, cols], nt_dims, preferred_element_type=F32) + bn_ref[h]
            m = jnp.maximum(jnp.max(s_c, axis=-1, keepdims=True), jnp.max(s_n, axis=-1, keepdims=True))
            e_c = jnp.exp(s_c - m)
            e_n = jnp.exp(s_n - m)
            l = jnp.sum(e_c, axis=-1, keepdims=True) + jnp.sum(e_n, axis=-1, keepdims=True)
            pv = (jnp.dot(e_c.astype(BF16), vc[:, cols], preferred_element_type=F32)
                  + jnp.dot(e_n.astype(BF16), vn_ref[:, cols], preferred_element_type=F32)) * (1.0 / l)
            acc = pv if acc is None else jnp.where(sel, pv, acc)
        o_ref[:, cols] = acc.astype(o_ref.dtype)


def _sample_attention(u, cache_k, cache_v, bias_c, bias_n, batch, t):
    n_past = cache_k.shape[1]
    return pl.pallas_call(
        _sample_attn_kernel,
        grid=(batch,),
        in_specs=[
            pl.BlockSpec((t, A_WIDTH), lambda b: (b, 0)),
            pl.BlockSpec((t, A_WIDTH), lambda b: (b, 1)),
            pl.BlockSpec((t, A_WIDTH), lambda b: (b, 2)),
            pl.BlockSpec((1, n_past, A_WIDTH), lambda b: (b, 0, 0)),
            pl.BlockSpec((1, n_past, A_WIDTH), lambda b: (b, 0, 0)),
            pl.BlockSpec((HEADS, t, n_past), lambda b: (0, 0, 0)),
            pl.BlockSpec((HEADS, t, t), lambda b: (0, 0, 0)),
        ],
        out_specs=pl.BlockSpec((t, A_WIDTH), lambda b: (b, 0)),
        out_shape=jax.ShapeDtypeStruct((batch * t, A_WIDTH), BF16),
        compiler_params=_params("arbitrary"),
    )(u, u, u, cache_k, cache_v, bias_c, bias_n)


def _sample_bias_tables(rel_bias, t, n_past):
    def by_dist(lo, hi):
        dist = jnp.arange(lo, hi + 1, dtype=jnp.int32)
        return rel_bias[:, jnp.clip(dist, -REL_CLIP, REL_CLIP) + REL_CLIP].astype(F32)

    cache = by_dist(-n_past - t + 1, -1)
    new = by_dist(-t + 1, t - 1)
    bias_c = jnp.stack([cache[:, t - 1 - i:t - 1 - i + n_past] for i in range(t)], axis=1)
    bias_n = jnp.stack([new[:, t - 1 - i:2 * t - 1 - i] for i in range(t)], axis=1)
    return bias_c, bias_n


def _mix_kernel(x_ref, ya_ref, ab_ref, gc_ref, gb_ref, hc_ref, hb0_ref, hc0_ref,
                wb_ref, bb_ref, lg_ref, lb_ref, wc_ref, go_ref, wo_ref,
                h_ref, nb_ref, nc_ref, cb, cc):
    tm = x_ref.shape[0]

    @pl.when(pl.program_id(1) == 0)
    def _():
        cb[0:HIST_B, :] = hb0_ref[0]
        cc[0:HIST_C, :] = hc0_ref[0]

    ab = ab_ref[...].astype(F32)
    cb[HIST_B:HIST_B + tm, :] = ab[:, :B_WIDTH] * _sigmoid(ab[:, B_WIDTH:])
    z = jnp.zeros((tm, B_WIDTH), F32) + bb_ref[...]
    for j in range(B_KERNEL):
        z = z + wb_ref[j:j + 1, :] * cb[pl.ds(j + HIST_B - (B_KERNEL - 1), tm), :]
    mu = jnp.mean(z, axis=-1, keepdims=True)
    zc = z - mu
    var = jnp.mean(zc * zc, axis=-1, keepdims=True)
    yb = zc * lax.rsqrt(var + EPS) * lg_ref[...] + lb_ref[...]
    yb = yb * _sigmoid(yb)

    cc[HIST_C:HIST_C + tm, :] = gc_ref[...].astype(F32) * hc_ref[...].astype(F32)
    conv = jnp.zeros((tm, C_WIDTH), F32)
    for j in range(C_KERNEL):
        conv = conv + wc_ref[j:j + 1, :] * cc[pl.ds(j + HIST_C - (C_KERNEL - 1), tm), :]
    yc = gb_ref[...].astype(F32) * conv

    go = go_ref[...]
    na = _rms(ya_ref[...].astype(F32), go[:, :A_WIDTH]).astype(BF16)
    nb = _rms(yb, go[:, A_WIDTH:A_WIDTH + B_WIDTH]).astype(BF16)
    nc = _rms(yc, go[:, A_WIDTH + B_WIDTH:]).astype(BF16)
    h_ref[...] = (x_ref[...]
                  + jnp.dot(na, wo_ref[0:A_WIDTH, :], preferred_element_type=F32)
                  + jnp.dot(nb, wo_ref[A_WIDTH:A_WIDTH + B_WIDTH, :], preferred_element_type=F32)
                  + jnp.dot(nc, wo_ref[A_WIDTH + B_WIDTH:, :], preferred_element_type=F32))

    tail_b = cb[tm:tm + HIST_B, :]
    tail_c = cc[tm:tm + HIST_C, :]
    nb_ref[0] = tail_b
    nc_ref[0] = tail_c
    cb[0:HIST_B, :] = tail_b
    cc[0:HIST_C, :] = tail_c


def _mix(x, ya, u, hist_b, hist_c, wb, bb, lg, lb, wc, go, wo, batch, t, tm):
    n, d = x.shape
    nt = t // tm
    row = lambda b, i: b * nt + i
    ab_blk = 3 * A_WIDTH // (2 * B_WIDTH)
    c_blk = (3 * A_WIDTH + 2 * B_WIDTH) // C_WIDTH
    const = lambda b, i: (0, 0)
    return pl.pallas_call(
        _mix_kernel,
        grid=(batch, nt),
        in_specs=[
            pl.BlockSpec((tm, d), lambda b, i: (row(b, i), 0)),
            pl.BlockSpec((tm, A_WIDTH), lambda b, i: (row(b, i), 0)),
            pl.BlockSpec((tm, 2 * B_WIDTH), lambda b, i: (row(b, i), ab_blk)),
            pl.BlockSpec((tm, C_WIDTH), lambda b, i: (row(b, i), c_blk)),
            pl.BlockSpec((tm, C_WIDTH), lambda b, i: (row(b, i), c_blk + 1)),
            pl.BlockSpec((tm, C_WIDTH), lambda b, i: (row(b, i), c_blk + 2)),
            pl.BlockSpec((1, HIST_B, B_WIDTH), lambda b, i: (b, 0, 0)),
            pl.BlockSpec((1, HIST_C, C_WIDTH), lambda b, i: (b, 0, 0)),
            pl.BlockSpec(wb.shape, const), pl.BlockSpec(bb.shape, const),
            pl.BlockSpec(lg.shape, const), pl.BlockSpec(lb.shape, const),
            pl.BlockSpec(wc.shape, const), pl.BlockSpec(go.shape, const),
            pl.BlockSpec(wo.shape, const),
        ],
        out_specs=[pl.BlockSpec((tm, d), lambda b, i: (row(b, i), 0)),
                   pl.BlockSpec((1, HIST_B, B_WIDTH), lambda b, i: (b, 0, 0)),
                   pl.BlockSpec((1, HIST_C, C_WIDTH), lambda b, i: (b, 0, 0))],
        out_shape=[jax.ShapeDtypeStruct((n, d), F32),
                   jax.ShapeDtypeStruct((batch, HIST_B, B_WIDTH), F32),
                   jax.ShapeDtypeStruct((batch, HIST_C, C_WIDTH), F32)],
        scratch_shapes=[pltpu.VMEM((HIST_B + tm, B_WIDTH), F32),
                        pltpu.VMEM((HIST_C + tm, C_WIDTH), F32)],
        compiler_params=_params("arbitrary", "arbitrary"),
    )(x, ya, u, u, u, u, hist_b, hist_c, wb, bb, lg, lb, wc, go, wo)


def _ple_epilogue(h, p, gp, wpg, wpp, gf):
    gate = _sigmoid(jnp.dot(_rms(h, gp).astype(BF16), wpg, preferred_element_type=F32))
    h = h + gate * jnp.dot(p.astype(BF16), wpp, preferred_element_type=F32)
    return h if gf is None else _rms(h, gf)


def _swiglu_chunks(x, wg_ref, wu_ref, wd_ref, acc):
    f = wg_ref.shape[-1]
    for c in range(f // FF_CHUNK):
        cols = slice(c * FF_CHUNK, (c + 1) * FF_CHUNK)
        a = jnp.dot(x, wg_ref[0, :, cols], preferred_element_type=F32)
        b = jnp.dot(x, wu_ref[0, :, cols], preferred_element_type=F32)
        mid = (a * _sigmoid(a) * b).astype(BF16)
        part = jnp.dot(mid, wd_ref[0, cols, :], preferred_element_type=F32)
        if c == 0:
            acc[...] = part
        else:
            acc[...] += part


def _ffn_kernel(final, h_ref, g_ref, wg_ref, wu_ref, wd_ref, p_ref, gp_ref, wpg_ref, wpp_ref, gf_ref,
                o_ref, acc):
    xn = _rms(h_ref[...], g_ref[...]).astype(BF16)
    _swiglu_chunks(xn, wg_ref, wu_ref, wd_ref, acc)
    h = h_ref[...] + acc[...]
    o_ref[...] = _ple_epilogue(h, p_ref[0], gp_ref[...], wpg_ref[...], wpp_ref[...],
                               gf_ref[...] if final else None)


def _resident(shape):
    return pl.BlockSpec(shape, lambda *_: (0,) * len(shape), pipeline_mode=pl.Buffered(1))


def _ffn(h, g, wg, wu, wd, j, p, layer, gp, wpg, wpp, gf, final, tm):
    n, d = h.shape
    f = wg.shape[-1]
    const = lambda i: (0, 0)
    return pl.pallas_call(
        functools.partial(_ffn_kernel, final),
        grid=(n // tm,),
        in_specs=[
            pl.BlockSpec((tm, d), lambda i: (i, 0)),
            pl.BlockSpec((1, d), const),
            pl.BlockSpec((1, d, f), lambda i: (j, 0, 0), pipeline_mode=pl.Buffered(1)),
            pl.BlockSpec((1, d, f), lambda i: (j, 0, 0), pipeline_mode=pl.Buffered(1)),
            pl.BlockSpec((1, f, d), lambda i: (j, 0, 0), pipeline_mode=pl.Buffered(1)),
            pl.BlockSpec((1, tm, p.shape[-1]), lambda i: (layer, i, 0)),
            pl.BlockSpec((1, d), const),
            _resident(wpg.shape),
            _resident(wpp.shape),
            pl.BlockSpec((1, d), const),
        ],
        out_specs=pl.BlockSpec((tm, d), lambda i: (i, 0)),
        out_shape=jax.ShapeDtypeStruct((n, d), F32),
        scratch_shapes=[pltpu.VMEM((tm, d), F32)],
        compiler_params=_params("arbitrary"),
    )(h, g, wg, wu, wd, p, gp, wpg, wpp, gf)


def _router_kernel(h_ref, g_ref, wr_ref, br_ref, x3_ref, meta_ref, cnt_ref, carry):
    tm = h_ref.shape[0]

    @pl.when(pl.program_id(0) == 0)
    def _():
        carry[...] = jnp.zeros_like(carry)

    xn = _rms(h_ref[...], g_ref[...])
    for s in range(ROW_SPLIT):
        x3_ref[:, s, :] = xn[:, s * LANES:(s + 1) * LANES]
    logits = lax.dot_general(wr_ref[...], xn, (((1,), (1,)), ((), ())),
                             precision=lax.Precision.HIGHEST,
                             preferred_element_type=F32) + br_ref[...]
    eidx = lax.broadcasted_iota(jnp.int32, (N_EXPERTS, tm), 0)
    m1 = jnp.max(logits, axis=0, keepdims=True)
    i1 = jnp.min(jnp.where(logits == m1, eidx, N_EXPERTS), axis=0, keepdims=True)
    rest = jnp.where(eidx == i1, -jnp.inf, logits)
    m2 = jnp.max(rest, axis=0, keepdims=True)
    i2 = jnp.min(jnp.where(rest == m2, eidx, N_EXPERTS), axis=0, keepdims=True)
    e2 = jnp.exp(m2 - m1)
    g1 = 1.0 / (1.0 + e2)
    g2 = e2 / (1.0 + e2)
    hit1 = eidx == i1
    hit2 = eidx == i2
    chosen = jnp.where(hit1 | hit2, 1.0, 0.0)
    src = lax.broadcasted_iota(jnp.int32, (tm, tm), 0)
    dst = lax.broadcasted_iota(jnp.int32, (tm, tm), 1)
    before = jnp.where(src < dst, 1.0, 0.0).astype(BF16)
    rank = jnp.dot(chosen.astype(BF16), before, preferred_element_type=F32) + carry[:, 0:1]
    r1 = jnp.sum(jnp.where(hit1, rank, 0.0), axis=0, keepdims=True)
    r2 = jnp.sum(jnp.where(hit2, rank, 0.0), axis=0, keepdims=True)
    zero = jnp.zeros_like(g1)
    meta_ref[...] = jnp.concatenate(
        [i1.astype(F32), i2.astype(F32), g1, g2, r1, r2, zero, zero], axis=0)
    carry[...] = carry[...] + jnp.sum(chosen, axis=1, keepdims=True)
    cnt_ref[...] = carry[...]


def _router(h, g, wr_t, br, tm):
    n, d = h.shape
    return pl.pallas_call(
        _router_kernel,
        grid=(n // tm,),
        in_specs=[pl.BlockSpec((tm, d), lambda i: (i, 0)),
                  pl.BlockSpec((1, d), lambda i: (0, 0)),
                  pl.BlockSpec((N_EXPERTS, d), lambda i: (0, 0)),
                  pl.BlockSpec((N_EXPERTS, 1), lambda i: (0, 0))],
        out_specs=[pl.BlockSpec((tm, ROW_SPLIT, LANES), lambda i: (i, 0, 0)),
                   pl.BlockSpec((8, tm), lambda i: (0, i)),
                   pl.BlockSpec((N_EXPERTS, LANES), lambda i: (0, 0))],
        out_shape=[jax.ShapeDtypeStruct((n, ROW_SPLIT, LANES), F32),
                   jax.ShapeDtypeStruct((8, n), F32),
                   jax.ShapeDtypeStruct((N_EXPERTS, LANES), F32)],
        scratch_shapes=[pltpu.VMEM((N_EXPERTS, LANES), F32)],
        compiler_params=_params("arbitrary"),
    )(h, g, wr_t, br)


def _dispatch_kernel(d1_ref, d2_ref, x_ref, o_hbm, sem):
    tm = x_ref.shape[0]

    def issue(g, carry):
        for k in range(DMA_UNROLL):
            t = g * DMA_UNROLL + k
            pltpu.make_async_copy(x_ref.at[t], o_hbm.at[d1_ref[0, 0, t]], sem).start()
            pltpu.make_async_copy(x_ref.at[t], o_hbm.at[d2_ref[0, 0, t]], sem).start()
        return carry

    lax.fori_loop(0, tm // DMA_UNROLL, issue, 0)
    for _ in range(2):
        pltpu.make_async_copy(x_ref, o_hbm.at[pl.ds(0, tm)], sem).wait()


def _dispatch(x3, d1, d2, rows_out, tm):
    n = x3.shape[0]
    steps = d1.shape[0] // tm
    last = n // tm - 1
    idx_spec = pl.BlockSpec((1, 1, tm), lambda i: (i, 0, 0), memory_space=pltpu.SMEM)
    return pl.pallas_call(
        _dispatch_kernel,
        grid=(steps,),
        in_specs=[idx_spec, idx_spec,
                  pl.BlockSpec((tm, ROW_SPLIT, LANES), lambda i: (jnp.minimum(i, last), 0, 0))],
        out_specs=pl.BlockSpec(memory_space=pl.ANY),
        out_shape=jax.ShapeDtypeStruct((rows_out, ROW_SPLIT, LANES), x3.dtype),
        scratch_shapes=[pltpu.SemaphoreType.DMA(())],
        compiler_params=_params("arbitrary"),
    )(d1.reshape(steps, 1, tm), d2.reshape(steps, 1, tm), x3)


def _expert_kernel(texp_ref, nact_ref, x_ref, wg_ref, wu_ref, wd_ref, y_ref, x2, acc):
    del texp_ref

    @pl.when(pl.program_id(0) < nact_ref[0])
    def _():
        for s in range(ROW_SPLIT):
            x2[:, s * LANES:(s + 1) * LANES] = x_ref[:, s, :].astype(BF16)
        _swiglu_chunks(x2[...], wg_ref, wu_ref, wd_ref, acc)
        for s in range(ROW_SPLIT):
            y_ref[:, s, :] = acc[:, s * LANES:(s + 1) * LANES]

    @pl.when(pl.program_id(0) >= nact_ref[0])
    def _():
        y_ref[...] = jnp.zeros_like(y_ref)


def _experts(xs, tile_exp, n_active, wg, wu, wd):
    rows = xs.shape[0]
    ntile = tile_exp.shape[0]
    d, f = wg.shape[1], wg.shape[2]
    grid_spec = pltpu.PrefetchScalarGridSpec(
        num_scalar_prefetch=2,
        grid=(ntile,),
        in_specs=[
            pl.BlockSpec((EXPERT_TILE, ROW_SPLIT, LANES), lambda t, te, na: (t, 0, 0)),
            pl.BlockSpec((1, d, f), lambda t, te, na: (te[t], 0, 0)),
            pl.BlockSpec((1, d, f), lambda t, te, na: (te[t], 0, 0)),
            pl.BlockSpec((1, f, d), lambda t, te, na: (te[t], 0, 0)),
        ],
        out_specs=pl.BlockSpec((EXPERT_TILE, ROW_SPLIT, LANES), lambda t, te, na: (t, 0, 0)),
        scratch_shapes=[pltpu.VMEM((EXPERT_TILE, d), BF16), pltpu.VMEM((EXPERT_TILE, d), F32)],
    )
    return pl.pallas_call(
        _expert_kernel,
        grid_spec=grid_spec,
        out_shape=jax.ShapeDtypeStruct((rows, ROW_SPLIT, LANES), F32),
        compiler_params=_params("arbitrary"),
    )(tile_exp, n_active, xs, wg, wu, wd)


def _combine_kernel(final, d1_ref, d2_ref, ye_hbm, h_ref, gate_ref, p_ref, gp_ref, wpg_ref, wpp_ref, gf_ref,
                    o_ref, y1, y2, moe, sem):
    tm = h_ref.shape[0]

    def issue(g, carry):
        for k in range(DMA_UNROLL):
            t = g * DMA_UNROLL + k
            pltpu.make_async_copy(ye_hbm.at[d1_ref[0, 0, t]], y1.at[t], sem).start()
            pltpu.make_async_copy(ye_hbm.at[d2_ref[0, 0, t]], y2.at[t], sem).start()
        return carry

    lax.fori_loop(0, tm // DMA_UNROLL, issue, 0)
    pltpu.make_async_copy(ye_hbm.at[pl.ds(0, tm)], y1, sem).wait()
    pltpu.make_async_copy(ye_hbm.at[pl.ds(0, tm)], y2, sem).wait()
    g1 = gate_ref[:, 0:1]
    g2 = gate_ref[:, 1:2]
    for s in range(ROW_SPLIT):
        moe[:, s * LANES:(s + 1) * LANES] = g1 * y1[:, s, :] + g2 * y2[:, s, :]
    h = h_ref[...] + moe[...]
    o_ref[...] = _ple_epilogue(h, p_ref[0], gp_ref[...], wpg_ref[...], wpp_ref[...],
                               gf_ref[...] if final else None)


def _combine(ye, d1, d2, h, gates, p, layer, gp, wpg, wpp, gf, final, tm):
    n, d = h.shape
    steps = n // tm
    idx_spec = pl.BlockSpec((1, 1, tm), lambda i: (i, 0, 0), memory_space=pltpu.SMEM)
    const = lambda i: (0, 0)
    return pl.pallas_call(
        functools.partial(_combine_kernel, final),
        grid=(steps,),
        in_specs=[idx_spec, idx_spec,
                  pl.BlockSpec(memory_space=pl.ANY),
                  pl.BlockSpec((tm, d), lambda i: (i, 0)),
                  pl.BlockSpec((tm, 2), lambda i: (i, 0)),
                  pl.BlockSpec((1, tm, p.shape[-1]), lambda i: (layer, i, 0)),
                  pl.BlockSpec((1, d), const),
                  _resident(wpg.shape),
                  _resident(wpp.shape),
                  pl.BlockSpec((1, d), const)],
        out_specs=pl.BlockSpec((tm, d), lambda i: (i, 0)),
        out_shape=jax.ShapeDtypeStruct((n, d), F32),
        scratch_shapes=[pltpu.VMEM((tm, ROW_SPLIT, LANES), F32),
                        pltpu.VMEM((tm, ROW_SPLIT, LANES), F32),
                        pltpu.VMEM((tm, d), F32),
                        pltpu.SemaphoreType.DMA(())],
        compiler_params=_params("arbitrary"),
    )(d1.reshape(steps, 1, tm), d2.reshape(steps, 1, tm), ye, h, gates, p, gp, wpg, wpp, gf)


def _moe(h, g, wr_t, br, wg, wu, wd, p, layer, gp, wpg, wpp, gf, final, tm):
    n = h.shape[0]
    x3, meta, cnt = _router(h, g, wr_t, br, tm)
    counts = cnt[:, 0].astype(jnp.int32)
    tiles = (counts + EXPERT_TILE - 1) // EXPERT_TILE
    padded = tiles * EXPERT_TILE
    ends = jnp.cumsum(padded)
    offs = ends - padded
    expert = jnp.arange(N_EXPERTS, dtype=jnp.int32)[:, None]
    start_of = lambda idx: jnp.sum(jnp.where(idx[None, :] == expert, offs[:, None], 0), axis=0)
    d1 = start_of(meta[0].astype(jnp.int32)) + meta[4].astype(jnp.int32)
    d2 = start_of(meta[1].astype(jnp.int32)) + meta[5].astype(jnp.int32)
    gates = jnp.stack([meta[2], meta[3]], axis=1)
    ntile = 2 * n // EXPERT_TILE + N_EXPERTS
    r = jnp.arange(EXPERT_TILE, dtype=jnp.int32)[None, :]
    in_group = counts[:, None] + r < padded[:, None]
    spare = ends[-1] + (jnp.cumsum(jnp.where(in_group, 0, 1).reshape(-1)) - 1).reshape(in_group.shape)
    fill = jnp.where(in_group, offs[:, None] + counts[:, None] + r, spare).reshape(-1).astype(jnp.int32)
    half = fill.shape[0] // 2
    xs = _dispatch(x3, jnp.concatenate([d1, fill[:half]]), jnp.concatenate([d2, fill[half:]]),
                   ntile * EXPERT_TILE, tm)
    n_active = jnp.sum(tiles).astype(jnp.int32)
    tile_start = jnp.minimum(jnp.arange(ntile, dtype=jnp.int32), n_active - 1) * EXPERT_TILE
    tile_exp = jnp.minimum(jnp.sum(jnp.where(tile_start[:, None] >= ends[None, :], 1, 0), axis=1),
                           N_EXPERTS - 1).astype(jnp.int32)
    ye = _experts(xs, tile_exp, n_active.reshape(1), wg, wu, wd)
    return _combine(ye, d1, d2, h, gates, p, layer, gp, wpg, wpp, gf, final, tm)


def _trunk(x, p, past, prm, tok_tile, mix_tile):
    batch, t, d = x.shape
    n = batch * t
    depth = prm["w_in"].shape[0]
    h = x.reshape(n, d)
    new_k, new_v, new_b, new_c = [], [], [], []
    row = lambda a: a.reshape(1, -1)
    p = p.reshape(depth, n, -1)
    keep = min(BAND, t) if past is None else t
    for i in range(depth):
        u, kv = _inproj(h, row(prm["norm_mix"][i]), prm["w_in"][i], tok_tile, t, keep)
        new_k.append(kv[:, :A_WIDTH].reshape(batch, keep, HEADS, HEAD_DIM))
        new_v.append(kv[:, A_WIDTH:].reshape(batch, keep, HEADS, HEAD_DIM))
        if past is None:
            ya = _prompt_attention(u, _prompt_bias_table(prm["rel_bias"][i]), batch, t)
            hist_b = jnp.zeros((batch, HIST_B, B_WIDTH), F32)
            hist_c = jnp.zeros((batch, HIST_C, C_WIDTH), F32)
        else:
            ck, cv, sb, sc = past
            n_past = ck.shape[2]
            bias_c, bias_n = _sample_bias_tables(prm["rel_bias"][i], t, n_past)
            ya = _sample_attention(u, ck[i].reshape(batch, n_past, A_WIDTH),
                                   cv[i].reshape(batch, n_past, A_WIDTH), bias_c, bias_n, batch, t)
            hist_b = jnp.pad(sb[i], ((0, 0), (HIST_B - (B_KERNEL - 1), 0), (0, 0)))
            hist_c = jnp.pad(sc[i], ((0, 0), (HIST_C - (C_KERNEL - 1), 0), (0, 0)))
        wb = jnp.pad(prm["w_dw_b"][i], ((0, HIST_B - B_KERNEL), (0, 0)))
        wc = jnp.pad(prm["w_dw_c"][i], ((0, HIST_C - C_KERNEL), (0, 0)))
        h, nb, nc = _mix(h, ya, u, hist_b, hist_c, wb, row(prm["b_dw_b"][i]), row(prm["ln_g_b"][i]),
                         row(prm["ln_b_b"][i]), wc, row(prm["g_out"][i]), prm["w_out"][i],
                         batch, t, mix_tile)
        new_b.append(nb[:, HIST_B - (B_KERNEL - 1):])
        new_c.append(nc[:, HIST_C - (C_KERNEL - 1):])
        final = i == depth - 1
        tail = (p, i, row(prm["norm_ple"][i]), prm["w_ple_gate"][i], prm["w_ple_proj"][i],
                row(prm["norm_final"]), final)
        j = i // 2
        if i % 2 == 0:
            h = _ffn(h, row(prm["norm_ffn"][i]), prm["w_ff_gate"], prm["w_ff_up"], prm["w_ff_down"], j,
                     *tail, tok_tile)
        else:
            h = _moe(h, row(prm["norm_ffn"][i]), prm["w_router"][j].T, prm["b_router"][j].reshape(-1, 1),
                     prm["w_ex_gate"][j], prm["w_ex_up"][j], prm["w_ex_down"][j], *tail, tok_tile)
    return (h.reshape(batch, t, d), jnp.stack(new_k), jnp.stack(new_v), jnp.stack(new_b), jnp.stack(new_c))


def kernel(x_prompt, x_sample, p_prompt, p_sample, cache_attn_k, cache_attn_v, state_conv_b, state_conv_c,
           w_in, rel_bias, w_dw_b, b_dw_b, ln_g_b, ln_b_b, w_dw_c, g_out, w_out, norm_mix, norm_ffn,
           w_ff_gate, w_ff_up, w_ff_down, w_router, b_router, w_ex_gate, w_ex_up, w_ex_down,
           norm_ple, w_ple_gate, w_ple_proj, norm_final):
    prm = dict(
        w_in=w_in.astype(BF16), rel_bias=rel_bias, w_dw_b=w_dw_b, b_dw_b=b_dw_b, ln_g_b=ln_g_b,
        ln_b_b=ln_b_b, w_dw_c=w_dw_c, g_out=g_out, w_out=w_out.astype(BF16), norm_mix=norm_mix,
        norm_ffn=norm_ffn, w_ff_gate=w_ff_gate.astype(BF16), w_ff_up=w_ff_up.astype(BF16),
        w_ff_down=w_ff_down.astype(BF16), w_router=w_router, b_router=b_router,
        w_ex_gate=w_ex_gate.astype(BF16), w_ex_up=w_ex_up.astype(BF16), w_ex_down=w_ex_down.astype(BF16),
        norm_ple=norm_ple, w_ple_gate=w_ple_gate.astype(BF16), w_ple_proj=w_ple_proj.astype(BF16),
        norm_final=norm_final)
    dec_t = x_sample.shape[1]
    y_p, k_p, v_p, b_p, c_p = _trunk(x_prompt, p_prompt, None, prm, 512, 512)
    y_s, k_s, v_s, b_s, c_s = _trunk(x_sample, p_sample,
                                     (cache_attn_k, cache_attn_v, state_conv_b, state_conv_c),
                                     prm, x_sample.shape[0] * dec_t, dec_t)
    return (y_p, y_s, k_p, v_p, b_p, c_p, k_s, v_s, b_s, c_s)
```

```python
import functools

import jax
import jax.numpy as jnp
from jax import lax
from jax.experimental import pallas as pl
from jax.experimental.pallas import tpu as pltpu

F32 = jnp.float32
BF16 = jnp.bfloat16

CHUNK = 64
BAND = 512
HEADS = 8
HEAD_DIM = 64
A_WIDTH = HEADS * HEAD_DIM
B_WIDTH = 256
C_WIDTH = 256
B_KERNEL = 31
C_KERNEL = 3
REL_CLIP = 128
N_EXPERTS = 8
EPS = 1e-6
NEG = -1e30
LOG2E = 1.4426950408889634

LANES = 128
SUBLANES = 8
ROW_SPLIT = 8
QTILE = 512
PAIR_ROWS = 2 * CHUNK
WIN = BAND + PAIR_ROWS
HIST_B = 32
HIST_C = 8
EXPERT_TILE = 512
FF_CHUNK = 512
DMA_UNROLL = 8
VMEM_LIMIT = 56 * 1024 * 1024


def _params(*sem):
    return pltpu.CompilerParams(dimension_semantics=sem, vmem_limit_bytes=VMEM_LIMIT)


def _rms(x, g):
    return x * lax.rsqrt(jnp.mean(x * x, axis=-1, keepdims=True) + EPS) * g


def _sigmoid(x):
    return 1.0 / (1.0 + jnp.exp(-x))


def _chunk(s, n):
    return pl.ds(s, n, stride=ROW_SPLIT)


def _row_tile(r):
    return pl.ds(pl.multiple_of(r * ROW_SPLIT, ROW_SPLIT), ROW_SPLIT)


def _inproj_kernel(x_ref, g_ref, w_ref, u_ref, kv_ref):
    hn = _rms(x_ref[...], g_ref[...]).astype(BF16)
    q = jnp.dot(hn, w_ref[:, :A_WIDTH], preferred_element_type=F32)
    u_ref[:, :A_WIDTH] = (q * (HEAD_DIM ** -0.5 * LOG2E)).astype(BF16)
    kv = jnp.dot(hn, w_ref[:, A_WIDTH:3 * A_WIDTH], preferred_element_type=F32)
    kv_ref[...] = kv
    u_ref[:, A_WIDTH:3 * A_WIDTH] = kv.astype(BF16)
    rest = jnp.dot(hn, w_ref[:, 3 * A_WIDTH:], preferred_element_type=F32)
    u_ref[:, 3 * A_WIDTH:] = rest.astype(BF16)


def _inproj(x, g, w, tm, t, keep):
    n, d = x.shape
    wi = w.shape[1]
    if keep == t:
        kv_index = lambda i: (i, 0)
    else:
        nt, ktiles = t // tm, keep // tm
        kv_index = lambda i: ((i // nt) * ktiles + jnp.maximum(i % nt - (nt - ktiles), 0), 0)
    return pl.pallas_call(
        _inproj_kernel,
        grid=(n // tm,),
        in_specs=[pl.BlockSpec((tm, d), lambda i: (i, 0)),
                  pl.BlockSpec((1, d), lambda i: (0, 0)),
                  pl.BlockSpec((d, wi), lambda i: (0, 0))],
        out_specs=[pl.BlockSpec((tm, wi), lambda i: (i, 0)),
                   pl.BlockSpec((tm, 2 * A_WIDTH), kv_index)],
        out_shape=[jax.ShapeDtypeStruct((n, wi), BF16),
                   jax.ShapeDtypeStruct((n // t * keep, 2 * A_WIDTH), F32)],
        compiler_params=_params("arbitrary"),
    )(x, g, w)


def _attn_kernel(q_ref, kp_ref, kc_ref, vp_ref, vc_ref, bias_ref, o_ref, k_scr, vt_scr, s_scr):
    i = pl.program_id(1)
    k_scr[0:QTILE, :] = kp_ref[...]
    k_scr[QTILE:2 * QTILE, :] = kc_ref[...]
    vt_scr[:, 0:QTILE] = vp_ref[...].astype(F32).T.astype(BF16)
    vt_scr[:, QTILE:2 * QTILE] = vc_ref[...].astype(F32).T.astype(BF16)
    lane = lax.broadcasted_iota(jnp.int32, (PAIR_ROWS, LANES), 1)
    row = lax.broadcasted_iota(jnp.int32, (WIN, 2 * PAIR_ROWS), 0)
    nblk = QTILE // PAIR_ROWS
    npair = HEADS // 2

    def scores(p):
        cols = slice(p * LANES, (p + 1) * LANES)
        for c2 in range(nblk):
            r0 = c2 * PAIR_ROWS
            kwin = k_scr[r0:r0 + WIN, cols]
            q2 = q_ref[r0:r0 + PAIR_ROWS, cols]
            zero = jnp.zeros_like(q2)
            qq = jnp.concatenate([jnp.where(lane < HEAD_DIM, q2, zero),
                                  jnp.where(lane >= HEAD_DIM, q2, zero)], axis=0)
            s_scr[(p % 2) * nblk + c2] = lax.dot_general(
                kwin, qq, (((1,), (1,)), ((), ())), preferred_element_type=F32)

    def tile(first):
        scores(0)
        for p in range(npair):
            if p + 1 < npair:
                scores(p + 1)
            cols = slice(p * LANES, (p + 1) * LANES)
            for c2 in range(nblk):
                r0 = c2 * PAIR_ROWS
                vwin = vt_scr[cols, r0:r0 + WIN]
                s = s_scr[(p % 2) * nblk + c2] + bias_ref[p]
                if first:
                    s = jnp.where(row < BAND - r0, NEG, s)
                m = jnp.max(s, axis=0, keepdims=True)
                e = jnp.exp2(s - m)
                l = jnp.sum(e, axis=0, keepdims=True)
                ot = jnp.dot(vwin, e.astype(BF16), preferred_element_type=F32)
                ot = ot * (1.0 / l)
                o = jnp.concatenate([ot[:HEAD_DIM, :PAIR_ROWS], ot[HEAD_DIM:, PAIR_ROWS:]], axis=0).T
                o_ref[r0:r0 + PAIR_ROWS, cols] = o.astype(o_ref.dtype)

    @pl.when(i == 0)
    def _():
        tile(True)

    @pl.when(i > 0)
    def _():
        tile(False)


def _prompt_attention(u, bias_t, batch, seq):
    nt = seq // QTILE
    prev = lambda b, i: b * nt + jnp.maximum(i - 1, 0)
    return pl.pallas_call(
        _attn_kernel,
        grid=(batch, nt),
        in_specs=[
            pl.BlockSpec((QTILE, A_WIDTH), lambda b, i: (b * nt + i, 0)),
            pl.BlockSpec((QTILE, A_WIDTH), lambda b, i: (prev(b, i), 1)),
            pl.BlockSpec((QTILE, A_WIDTH), lambda b, i: (b * nt + i, 1)),
            pl.BlockSpec((QTILE, A_WIDTH), lambda b, i: (prev(b, i), 2)),
            pl.BlockSpec((QTILE, A_WIDTH), lambda b, i: (b * nt + i, 2)),
            pl.BlockSpec((HEADS // 2, WIN, 2 * PAIR_ROWS), lambda b, i: (0, 0, 0)),
        ],
        out_specs=pl.BlockSpec((QTILE, A_WIDTH), lambda b, i: (b * nt + i, 0)),
        out_shape=jax.ShapeDtypeStruct((batch * seq, A_WIDTH), BF16),
        scratch_shapes=[pltpu.VMEM((2 * QTILE, A_WIDTH), BF16),
                        pltpu.VMEM((A_WIDTH, 2 * QTILE), BF16),
                        pltpu.VMEM((2 * QTILE // PAIR_ROWS, WIN, 2 * PAIR_ROWS), F32)],
        compiler_params=_params("arbitrary", "arbitrary"),
    )(u, u, u, u, u, bias_t)


def _hankel(v, rows, cols):
    heads, length = v.shape
    flat = jnp.tile(v, (1, rows + 1))[:, :rows * (length + 1)]
    return flat.reshape(heads, rows, length + 1)[:, :, :cols]


def _prompt_bias_table(rel_bias):
    by_dist = jnp.concatenate(
        [jnp.broadcast_to(rel_bias[:, :1], (HEADS, BAND)), rel_bias[:, :2 * REL_CLIP]], axis=1).astype(F32)
    by_sum = _hankel(by_dist, PAIR_ROWS, WIN + 1)
    table = jnp.transpose(by_sum[:, ::-1, 1:], (0, 2, 1))
    j = jnp.arange(WIN, dtype=jnp.int32)[:, None]
    lo = (jnp.arange(PAIR_ROWS, dtype=jnp.int32)[None, :] // CHUNK) * CHUNK
    valid = (j >= lo) & (j < lo + BAND + CHUNK)
    table = jnp.where(valid[None], table * LOG2E, NEG)
    table = table.reshape(HEADS // 2, 2, WIN, PAIR_ROWS)
    return jnp.transpose(table, (0, 2, 1, 3)).reshape(HEADS // 2, WIN, 2 * PAIR_ROWS)


def _sample_attn_kernel(q_ref, kn_ref, vn_ref, ck_ref, cv_ref, bc_ref, bn_ref, o_ref):
    t = q_ref.shape[0]
    lane = lax.broadcasted_iota(jnp.int32, (t, LANES), 1)
    kc = ck_ref[0].astype(BF16)
    vc = cv_ref[0].astype(BF16)
    for p in range(HEADS // 2):
        cols = slice(p * LANES, (p + 1) * LANES)
        q2 = q_ref[:, cols]
        acc = None
        for hh in range(2):
            h = 2 * p + hh
            sel = (lane >= HEAD_DIM) == bool(hh)
            qm = jnp.where(sel, q2, jnp.zeros_like(q2))
            nt_dims = (((1,), (1,)), ((), ()))
            s_c = lax.dot_general(qm, kc[:, cols], nt_dims, preferred_element_type=F32) + bc_ref[h]
            s_n = lax.dot_general(qm, kn_ref[:, cols], nt_dims, preferred_element_type=F32) + bn_ref[h]
            m = jnp.maximum(jnp.max(s_c, axis=-1, keepdims=True), jnp.max(s_n, axis=-1, keepdims=True))
            e_c = jnp.exp2(s_c - m)
            e_n = jnp.exp2(s_n - m)
            l = jnp.sum(e_c, axis=-1, keepdims=True) + jnp.sum(e_n, axis=-1, keepdims=True)
            pv = (jnp.dot(e_c.astype(BF16), vc[:, cols], preferred_element_type=F32)
                  + jnp.dot(e_n.astype(BF16), vn_ref[:, cols], preferred_element_type=F32)) * (1.0 / l)
            acc = pv if acc is None else jnp.where(sel, pv, acc)
        o_ref[:, cols] = acc.astype(o_ref.dtype)


def _sample_attention(u, cache_k, cache_v, bias_c, bias_n, batch, t):
    n_past = cache_k.shape[1]
    return pl.pallas_call(
        _sample_attn_kernel,
        grid=(batch,),
        in_specs=[
            pl.BlockSpec((t, A_WIDTH), lambda b: (b, 0)),
            pl.BlockSpec((t, A_WIDTH), lambda b: (b, 1)),
            pl.BlockSpec((t, A_WIDTH), lambda b: (b, 2)),
            pl.BlockSpec((1, n_past, A_WIDTH), lambda b: (b, 0, 0)),
            pl.BlockSpec((1, n_past, A_WIDTH), lambda b: (b, 0, 0)),
            pl.BlockSpec((HEADS, t, n_past), lambda b: (0, 0, 0)),
            pl.BlockSpec((HEADS, t, t), lambda b: (0, 0, 0)),
        ],
        out_specs=pl.BlockSpec((t, A_WIDTH), lambda b: (b, 0)),
        out_shape=jax.ShapeDtypeStruct((batch * t, A_WIDTH), BF16),
        compiler_params=_params("arbitrary"),
    )(u, u, u, cache_k, cache_v, bias_c, bias_n)


def _sample_bias_tables(rel_bias, t, n_past):
    def by_dist(lo, hi):
        dist = jnp.arange(lo, hi + 1, dtype=jnp.int32)
        return rel_bias[:, jnp.clip(dist, -REL_CLIP, REL_CLIP) + REL_CLIP].astype(F32) * LOG2E

    cache = by_dist(-n_past - t + 1, -1)
    new = by_dist(-t + 1, t - 1)
    return _hankel(cache, t, n_past)[:, ::-1], _hankel(new, t, t)[:, ::-1]


def _mix_kernel(x_ref, ya_ref, ab_ref, gc_ref, gb_ref, hc_ref, hb0_ref, hc0_ref,
                wb_ref, bb_ref, lg_ref, lb_ref, wc_ref, go_ref, wo_ref,
                h_ref, nb_ref, nc_ref, cb, cc, sh):
    tm = x_ref.shape[0]

    @pl.when(pl.program_id(1) == 0)
    def _():
        cb[0:HIST_B, :] = hb0_ref[0]
        cc[0:HIST_C, :] = hc0_ref[0]

    ab = ab_ref[...].astype(F32)
    cb[HIST_B:HIST_B + tm, :] = ab[:, :B_WIDTH] * _sigmoid(ab[:, B_WIDTH:])
    span = tm + HIST_B - SUBLANES
    for r in range(1, SUBLANES):
        sh[r - 1, 0:span, :] = cb[pl.ds(r, span), :]
    z = jnp.zeros((tm, B_WIDTH), F32) + bb_ref[...]
    for j in range(B_KERNEL):
        m = j + HIST_B - (B_KERNEL - 1)
        a = m // SUBLANES * SUBLANES
        src = cb[a:a + tm, :] if m == a else sh[m - a - 1, a:a + tm, :]
        z = z + wb_ref[j:j + 1, :] * src
    mu = jnp.mean(z, axis=-1, keepdims=True)
    zc = z - mu
    var = jnp.mean(zc * zc, axis=-1, keepdims=True)
    yb = zc * lax.rsqrt(var + EPS) * lg_ref[...] + lb_ref[...]
    yb = yb * _sigmoid(yb)

    cc[HIST_C:HIST_C + tm, :] = gc_ref[...].astype(F32) * hc_ref[...].astype(F32)
    conv = jnp.zeros((tm, C_WIDTH), F32)
    for j in range(C_KERNEL):
        conv = conv + wc_ref[j:j + 1, :] * cc[pl.ds(j + HIST_C - (C_KERNEL - 1), tm), :]
    yc = gb_ref[...].astype(F32) * conv

    go = go_ref[...]
    na = _rms(ya_ref[...].astype(F32), go[:, :A_WIDTH]).astype(BF16)
    nb = _rms(yb, go[:, A_WIDTH:A_WIDTH + B_WIDTH]).astype(BF16)
    nc = _rms(yc, go[:, A_WIDTH + B_WIDTH:]).astype(BF16)
    h_ref[...] = (x_ref[...]
                  + jnp.dot(na, wo_ref[0:A_WIDTH, :], preferred_element_type=F32)
                  + jnp.dot(nb, wo_ref[A_WIDTH:A_WIDTH + B_WIDTH, :], preferred_element_type=F32)
                  + jnp.dot(nc, wo_ref[A_WIDTH + B_WIDTH:, :], preferred_element_type=F32))

    tail_b = cb[tm:tm + HIST_B, :]
    tail_c = cc[tm:tm + HIST_C, :]
    nb_ref[0] = tail_b
    nc_ref[0] = tail_c
    cb[0:HIST_B, :] = tail_b
    cc[0:HIST_C, :] = tail_c


def _mix(x, ya, u, hist_b, hist_c, wb, bb, lg, lb, wc, go, wo, batch, t, tm):
    n, d = x.shape
    nt = t // tm
    row = lambda b, i: b * nt + i
    ab_blk = 3 * A_WIDTH // (2 * B_WIDTH)
    c_blk = (3 * A_WIDTH + 2 * B_WIDTH) // C_WIDTH
    const = lambda b, i: (0, 0)
    return pl.pallas_call(
        _mix_kernel,
        grid=(batch, nt),
        in_specs=[
            pl.BlockSpec((tm, d), lambda b, i: (row(b, i), 0)),
            pl.BlockSpec((tm, A_WIDTH), lambda b, i: (row(b, i), 0)),
            pl.BlockSpec((tm, 2 * B_WIDTH), lambda b, i: (row(b, i), ab_blk)),
            pl.BlockSpec((tm, C_WIDTH), lambda b, i: (row(b, i), c_blk)),
            pl.BlockSpec((tm, C_WIDTH), lambda b, i: (row(b, i), c_blk + 1)),
            pl.BlockSpec((tm, C_WIDTH), lambda b, i: (row(b, i), c_blk + 2)),
            pl.BlockSpec((1, HIST_B, B_WIDTH), lambda b, i: (b, 0, 0)),
            pl.BlockSpec((1, HIST_C, C_WIDTH), lambda b, i: (b, 0, 0)),
            pl.BlockSpec(wb.shape, const), pl.BlockSpec(bb.shape, const),
            pl.BlockSpec(lg.shape, const), pl.BlockSpec(lb.shape, const),
            pl.BlockSpec(wc.shape, const), pl.BlockSpec(go.shape, const),
            pl.BlockSpec(wo.shape, const),
        ],
        out_specs=[pl.BlockSpec((tm, d), lambda b, i: (row(b, i), 0)),
                   pl.BlockSpec((1, HIST_B, B_WIDTH), lambda b, i: (b, 0, 0)),
                   pl.BlockSpec((1, HIST_C, C_WIDTH), lambda b, i: (b, 0, 0))],
        out_shape=[jax.ShapeDtypeStruct((n, d), F32),
                   jax.ShapeDtypeStruct((batch, HIST_B, B_WIDTH), F32),
                   jax.ShapeDtypeStruct((batch, HIST_C, C_WIDTH), F32)],
        scratch_shapes=[pltpu.VMEM((HIST_B + tm, B_WIDTH), F32),
                        pltpu.VMEM((HIST_C + tm, C_WIDTH), F32),
                        pltpu.VMEM((SUBLANES - 1, HIST_B + tm - SUBLANES, B_WIDTH), F32)],
        compiler_params=_params("arbitrary", "arbitrary"),
    )(x, ya, u, u, u, u, hist_b, hist_c, wb, bb, lg, lb, wc, go, wo)


def _ple_epilogue(h, p, gp, wpg, wpp, gf):
    gate = _sigmoid(jnp.dot(_rms(h, gp).astype(BF16), wpg, preferred_element_type=F32))
    h = h + gate * jnp.dot(p.astype(BF16), wpp, preferred_element_type=F32)
    return h if gf is None else _rms(h, gf)


def _swiglu_chunks(x, wg_ref, wu_ref, wd_ref, acc):
    f = wg_ref.shape[-1]
    for c in range(f // FF_CHUNK):
        cols = slice(c * FF_CHUNK, (c + 1) * FF_CHUNK)
        a = jnp.dot(x, wg_ref[0, :, cols], preferred_element_type=F32)
        b = jnp.dot(x, wu_ref[0, :, cols], preferred_element_type=F32)
        mid = (a * _sigmoid(a) * b).astype(BF16)
        part = jnp.dot(mid, wd_ref[0, cols, :], preferred_element_type=F32)
        if c == 0:
            acc[...] = part
        else:
            acc[...] += part


def _ffn_kernel(final, h_ref, g_ref, wg_ref, wu_ref, wd_ref, p_ref, gp_ref, wpg_ref, wpp_ref, gf_ref,
                o_ref, acc):
    xn = _rms(h_ref[...], g_ref[...]).astype(BF16)
    _swiglu_chunks(xn, wg_ref, wu_ref, wd_ref, acc)
    h = h_ref[...] + acc[...]
    o_ref[...] = _ple_epilogue(h, p_ref[0], gp_ref[...], wpg_ref[...], wpp_ref[...],
                               gf_ref[...] if final else None)


def _resident(shape):
    return pl.BlockSpec(shape, lambda *_: (0,) * len(shape), pipeline_mode=pl.Buffered(1))


def _ffn(h, g, wg, wu, wd, j, p, layer, gp, wpg, wpp, gf, final, tm):
    n, d = h.shape
    f = wg.shape[-1]
    const = lambda i: (0, 0)
    return pl.pallas_call(
        functools.partial(_ffn_kernel, final),
        grid=(n // tm,),
        in_specs=[
            pl.BlockSpec((tm, d), lambda i: (i, 0)),
            pl.BlockSpec((1, d), const),
            pl.BlockSpec((1, d, f), lambda i: (j, 0, 0), pipeline_mode=pl.Buffered(1)),
            pl.BlockSpec((1, d, f), lambda i: (j, 0, 0), pipeline_mode=pl.Buffered(1)),
            pl.BlockSpec((1, f, d), lambda i: (j, 0, 0), pipeline_mode=pl.Buffered(1)),
            pl.BlockSpec((1, tm, p.shape[-1]), lambda i: (layer, i, 0)),
            pl.BlockSpec((1, d), const),
            _resident(wpg.shape),
            _resident(wpp.shape),
            pl.BlockSpec((1, d), const),
        ],
        out_specs=pl.BlockSpec((tm, d), lambda i: (i, 0)),
        out_shape=jax.ShapeDtypeStruct((n, d), F32),
        scratch_shapes=[pltpu.VMEM((tm, d), F32)],
        compiler_params=_params("arbitrary"),
    )(h, g, wg, wu, wd, p, gp, wpg, wpp, gf)


def _router_kernel(h_ref, g_ref, wr_ref, br_ref, x3_ref, meta_ref, cnt_ref, carry):
    tm = h_ref.shape[0]

    @pl.when(pl.program_id(0) == 0)
    def _():
        carry[...] = jnp.zeros_like(carry)

    xn = _rms(h_ref[...], g_ref[...])
    for s in range(ROW_SPLIT):
        x3_ref[_chunk(s, tm), :] = xn[:, s * LANES:(s + 1) * LANES]
    logits = lax.dot_general(wr_ref[...], xn, (((1,), (1,)), ((), ())),
                             precision=lax.Precision.HIGHEST,
                             preferred_element_type=F32) + br_ref[...]
    eidx = lax.broadcasted_iota(jnp.int32, (N_EXPERTS, tm), 0)
    m1 = jnp.max(logits, axis=0, keepdims=True)
    i1 = jnp.min(jnp.where(logits == m1, eidx, N_EXPERTS), axis=0, keepdims=True)
    rest = jnp.where(eidx == i1, -jnp.inf, logits)
    m2 = jnp.max(rest, axis=0, keepdims=True)
    i2 = jnp.min(jnp.where(rest == m2, eidx, N_EXPERTS), axis=0, keepdims=True)
    e2 = jnp.exp(m2 - m1)
    g1 = 1.0 / (1.0 + e2)
    g2 = e2 / (1.0 + e2)
    hit1 = eidx == i1
    hit2 = eidx == i2
    chosen = jnp.where(hit1 | hit2, 1.0, 0.0)
    src = lax.broadcasted_iota(jnp.int32, (tm, tm), 0)
    dst = lax.broadcasted_iota(jnp.int32, (tm, tm), 1)
    before = jnp.where(src < dst, 1.0, 0.0).astype(BF16)
    rank = jnp.dot(chosen.astype(BF16), before, preferred_element_type=F32) + carry[:, 0:1]
    r1 = jnp.sum(jnp.where(hit1, rank, 0.0), axis=0, keepdims=True)
    r2 = jnp.sum(jnp.where(hit2, rank, 0.0), axis=0, keepdims=True)
    zero = jnp.zeros_like(g1)
    meta_ref[...] = jnp.concatenate(
        [i1.astype(F32), i2.astype(F32), g1, g2, r1, r2, zero, zero], axis=0)
    carry[...] = carry[...] + jnp.sum(chosen, axis=1, keepdims=True)
    cnt_ref[...] = carry[...]


def _router(h, g, wr_t, br, tm):
    n, d = h.shape
    return pl.pallas_call(
        _router_kernel,
        grid=(n // tm,),
        in_specs=[pl.BlockSpec((tm, d), lambda i: (i, 0)),
                  pl.BlockSpec((1, d), lambda i: (0, 0)),
                  pl.BlockSpec((N_EXPERTS, d), lambda i: (0, 0)),
                  pl.BlockSpec((N_EXPERTS, 1), lambda i: (0, 0))],
        out_specs=[pl.BlockSpec((tm * ROW_SPLIT, LANES), lambda i: (i, 0)),
                   pl.BlockSpec((8, tm), lambda i: (0, i)),
                   pl.BlockSpec((N_EXPERTS, LANES), lambda i: (0, 0))],
        out_shape=[jax.ShapeDtypeStruct((n * ROW_SPLIT, LANES), F32),
                   jax.ShapeDtypeStruct((8, n), F32),
                   jax.ShapeDtypeStruct((N_EXPERTS, LANES), F32)],
        scratch_shapes=[pltpu.VMEM((N_EXPERTS, LANES), F32)],
        compiler_params=_params("arbitrary"),
    )(h, g, wr_t, br)


def _dispatch_kernel(d1_ref, d2_ref, x_ref, o_hbm, sem):
    tm = x_ref.shape[0] // ROW_SPLIT

    def issue(g, carry):
        for k in range(DMA_UNROLL):
            t = g * DMA_UNROLL + k
            pltpu.make_async_copy(x_ref.at[_row_tile(t)], o_hbm.at[_row_tile(d1_ref[0, 0, t])], sem).start()
            pltpu.make_async_copy(x_ref.at[_row_tile(t)], o_hbm.at[_row_tile(d2_ref[0, 0, t])], sem).start()
        return carry

    lax.fori_loop(0, tm // DMA_UNROLL, issue, 0)
    for _ in range(2):
        pltpu.make_async_copy(x_ref, o_hbm.at[pl.ds(0, tm * ROW_SPLIT)], sem).wait()


def _dispatch(x3, d1, d2, rows_out, tm):
    n = x3.shape[0] // ROW_SPLIT
    steps = d1.shape[0] // tm
    last = n // tm - 1
    idx_spec = pl.BlockSpec((1, 1, tm), lambda i: (i, 0, 0), memory_space=pltpu.SMEM)
    return pl.pallas_call(
        _dispatch_kernel,
        grid=(steps,),
        in_specs=[idx_spec, idx_spec,
                  pl.BlockSpec((tm * ROW_SPLIT, LANES), lambda i: (jnp.minimum(i, last), 0))],
        out_specs=pl.BlockSpec(memory_space=pl.ANY),
        out_shape=jax.ShapeDtypeStruct((rows_out * ROW_SPLIT, LANES), x3.dtype),
        scratch_shapes=[pltpu.SemaphoreType.DMA(())],
        compiler_params=_params("arbitrary"),
    )(d1.reshape(steps, 1, tm), d2.reshape(steps, 1, tm), x3)


def _expert_kernel(texp_ref, nact_ref, x_ref, wg_ref, wu_ref, wd_ref, y_ref, x2, acc):
    del texp_ref

    @pl.when(pl.program_id(0) < nact_ref[0])
    def _():
        for s in range(ROW_SPLIT):
            x2[:, s * LANES:(s + 1) * LANES] = x_ref[_chunk(s, EXPERT_TILE), :].astype(BF16)
        _swiglu_chunks(x2[...], wg_ref, wu_ref, wd_ref, acc)
        for s in range(ROW_SPLIT):
            y_ref[_chunk(s, EXPERT_TILE), :] = acc[:, s * LANES:(s + 1) * LANES]

    @pl.when(pl.program_id(0) >= nact_ref[0])
    def _():
        y_ref[...] = jnp.zeros_like(y_ref)


def _experts(xs, tile_exp, n_active, wg, wu, wd):
    ntile = tile_exp.shape[0]
    d, f = wg.shape[1], wg.shape[2]
    grid_spec = pltpu.PrefetchScalarGridSpec(
        num_scalar_prefetch=2,
        grid=(ntile,),
        in_specs=[
            pl.BlockSpec((EXPERT_TILE * ROW_SPLIT, LANES), lambda t, te, na: (t, 0)),
            pl.BlockSpec((1, d, f), lambda t, te, na: (te[t], 0, 0)),
            pl.BlockSpec((1, d, f), lambda t, te, na: (te[t], 0, 0)),
            pl.BlockSpec((1, f, d), lambda t, te, na: (te[t], 0, 0)),
        ],
        out_specs=pl.BlockSpec((EXPERT_TILE * ROW_SPLIT, LANES), lambda t, te, na: (t, 0)),
        scratch_shapes=[pltpu.VMEM((EXPERT_TILE, d), BF16), pltpu.VMEM((EXPERT_TILE, d), F32)],
    )
    return pl.pallas_call(
        _expert_kernel,
        grid_spec=grid_spec,
        out_shape=jax.ShapeDtypeStruct(xs.shape, F32),
        compiler_params=_params("arbitrary"),
    )(tile_exp, n_active, xs, wg, wu, wd)


def _combine_kernel(final, d1_ref, d2_ref, ye_hbm, h_ref, gate_ref, p_ref, gp_ref, wpg_ref, wpp_ref, gf_ref,
                    o_ref, y1, y2, moe, sem):
    tm = h_ref.shape[0]

    def issue(g, carry):
        for k in range(DMA_UNROLL):
            t = g * DMA_UNROLL + k
            pltpu.make_async_copy(ye_hbm.at[_row_tile(d1_ref[0, 0, t])], y1.at[_row_tile(t)], sem).start()
            pltpu.make_async_copy(ye_hbm.at[_row_tile(d2_ref[0, 0, t])], y2.at[_row_tile(t)], sem).start()
        return carry

    lax.fori_loop(0, tm // DMA_UNROLL, issue, 0)
    pltpu.make_async_copy(ye_hbm.at[pl.ds(0, tm * ROW_SPLIT)], y1, sem).wait()
    pltpu.make_async_copy(ye_hbm.at[pl.ds(0, tm * ROW_SPLIT)], y2, sem).wait()
    g1 = gate_ref[:, 0:1]
    g2 = gate_ref[:, 1:2]
    for s in range(ROW_SPLIT):
        moe[:, s * LANES:(s + 1) * LANES] = g1 * y1[_chunk(s, tm), :] + g2 * y2[_chunk(s, tm), :]
    h = h_ref[...] + moe[...]
    o_ref[...] = _ple_epilogue(h, p_ref[0], gp_ref[...], wpg_ref[...], wpp_ref[...],
                               gf_ref[...] if final else None)


def _combine(ye, d1, d2, h, gates, p, layer, gp, wpg, wpp, gf, final, tm):
    n, d = h.shape
    steps = n // tm
    idx_spec = pl.BlockSpec((1, 1, tm), lambda i: (i, 0, 0), memory_space=pltpu.SMEM)
    const = lambda i: (0, 0)
    return pl.pallas_call(
        functools.partial(_combine_kernel, final),
        grid=(steps,),
        in_specs=[idx_spec, idx_spec,
                  pl.BlockSpec(memory_space=pl.ANY),
                  pl.BlockSpec((tm, d), lambda i: (i, 0)),
                  pl.BlockSpec((tm, 2), lambda i: (i, 0)),
                  pl.BlockSpec((1, tm, p.shape[-1]), lambda i: (layer, i, 0)),
                  pl.BlockSpec((1, d), const),
                  _resident(wpg.shape),
                  _resident(wpp.shape),
                  pl.BlockSpec((1, d), const)],
        out_specs=pl.BlockSpec((tm, d), lambda i: (i, 0)),
        out_shape=jax.ShapeDtypeStruct((n, d), F32),
        scratch_shapes=[pltpu.VMEM((tm * ROW_SPLIT, LANES), F32),
                        pltpu.VMEM((tm * ROW_SPLIT, LANES), F32),
                        pltpu.VMEM((tm, d), F32),
                        pltpu.SemaphoreType.DMA(())],
        compiler_params=_params("arbitrary"),
    )(d1.reshape(steps, 1, tm), d2.reshape(steps, 1, tm), ye, h, gates, p, gp, wpg, wpp, gf)


def _moe(h, g, wr_t, br, wg, wu, wd, p, layer, gp, wpg, wpp, gf, final, tm):
    n = h.shape[0]
    x3, meta, cnt = _router(h, g, wr_t, br, tm)
    counts = cnt[:, 0].astype(jnp.int32)
    tiles = (counts + EXPERT_TILE - 1) // EXPERT_TILE
    padded = tiles * EXPERT_TILE
    ends = jnp.cumsum(padded)
    offs = ends - padded
    expert = jnp.arange(N_EXPERTS, dtype=jnp.int32)[:, None]
    start_of = lambda idx: jnp.sum(jnp.where(idx[None, :] == expert, offs[:, None], 0), axis=0)
    d1 = start_of(meta[0].astype(jnp.int32)) + meta[4].astype(jnp.int32)
    d2 = start_of(meta[1].astype(jnp.int32)) + meta[5].astype(jnp.int32)
    gates = jnp.stack([meta[2], meta[3]], axis=1)
    ntile = 2 * n // EXPERT_TILE + N_EXPERTS
    r = jnp.arange(EXPERT_TILE, dtype=jnp.int32)[None, :]
    in_group = counts[:, None] + r < padded[:, None]
    spare = ends[-1] + (jnp.cumsum(jnp.where(in_group, 0, 1).reshape(-1)) - 1).reshape(in_group.shape)
    fill = jnp.where(in_group, offs[:, None] + counts[:, None] + r, spare).reshape(-1).astype(jnp.int32)
    half = fill.shape[0] // 2
    xs = _dispatch(x3, jnp.concatenate([d1, fill[:half]]), jnp.concatenate([d2, fill[half:]]),
                   ntile * EXPERT_TILE, tm)
    n_active = jnp.sum(tiles).astype(jnp.int32)
    tile_start = jnp.minimum(jnp.arange(ntile, dtype=jnp.int32), n_active - 1) * EXPERT_TILE
    tile_exp = jnp.minimum(jnp.sum(jnp.where(tile_start[:, None] >= ends[None, :], 1, 0), axis=1),
                           N_EXPERTS - 1).astype(jnp.int32)
    ye = _experts(xs, tile_exp, n_active.reshape(1), wg, wu, wd)
    return _combine(ye, d1, d2, h, gates, p, layer, gp, wpg, wpp, gf, final, tm)


def _trunk(x, p, past, prm, tok_tile, mix_tile):
    batch, t, d = x.shape
    n = batch * t
    depth = prm["w_in"].shape[0]
    h = x.reshape(n, d)
    new_k, new_v, new_b, new_c = [], [], [], []
    row = lambda a: a.reshape(1, -1)
    p = p.reshape(depth, n, -1)
    keep = min(BAND, t) if past is None else t
    for i in range(depth):
        u, kv = _inproj(h, row(prm["norm_mix"][i]), prm["w_in"][i], tok_tile, t, keep)
        new_k.append(kv[:, :A_WIDTH].reshape(batch, keep, HEADS, HEAD_DIM))
        new_v.append(kv[:, A_WIDTH:].reshape(batch, keep, HEADS, HEAD_DIM))
        if past is None:
            ya = _prompt_attention(u, _prompt_bias_table(prm["rel_bias"][i]), batch, t)
            hist_b = jnp.zeros((batch, HIST_B, B_WIDTH), F32)
            hist_c = jnp.zeros((batch, HIST_C, C_WIDTH), F32)
        else:
            ck, cv, sb, sc = past
            n_past = ck.shape[2]
            bias_c, bias_n = _sample_bias_tables(prm["rel_bias"][i], t, n_past)
            ya = _sample_attention(u, ck[i].reshape(batch, n_past, A_WIDTH),
                                   cv[i].reshape(batch, n_past, A_WIDTH), bias_c, bias_n, batch, t)
            hist_b = jnp.pad(sb[i], ((0, 0), (HIST_B - (B_KERNEL - 1), 0), (0, 0)))
            hist_c = jnp.pad(sc[i], ((0, 0), (HIST_C - (C_KERNEL - 1), 0), (0, 0)))
        wb = jnp.pad(prm["w_dw_b"][i], ((0, HIST_B - B_KERNEL), (0, 0)))
        wc = jnp.pad(prm["w_dw_c"][i], ((0, HIST_C - C_KERNEL), (0, 0)))
        h, nb, nc = _mix(h, ya, u, hist_b, hist_c, wb, row(prm["b_dw_b"][i]), row(prm["ln_g_b"][i]),
                         row(prm["ln_b_b"][i]), wc, row(prm["g_out"][i]), prm["w_out"][i],
                         batch, t, mix_tile)
        new_b.append(nb[:, HIST_B - (B_KERNEL - 1):])
        new_c.append(nc[:, HIST_C - (C_KERNEL - 1):])
        final = i == depth - 1
        tail = (p, i, row(prm["norm_ple"][i]), prm["w_ple_gate"][i], prm["w_ple_proj"][i],
                row(prm["norm_final"]), final)
        j = i // 2
        if i % 2 == 0:
            h = _ffn(h, row(prm["norm_ffn"][i]), prm["w_ff_gate"], prm["w_ff_up"], prm["w_ff_down"], j,
                     *tail, tok_tile)
        else:
            h = _moe(h, row(prm["norm_ffn"][i]), prm["w_router"][j].T, prm["b_router"][j].reshape(-1, 1),
                     prm["w_ex_gate"][j], prm["w_ex_up"][j], prm["w_ex_down"][j], *tail, tok_tile)
    return (h.reshape(batch, t, d), jnp.stack(new_k), jnp.stack(new_v), jnp.stack(new_b), jnp.stack(new_c))


def kernel(x_prompt, x_sample, p_prompt, p_sample, cache_attn_k, cache_attn_v, state_conv_b, state_conv_c,
           w_in, rel_bias, w_dw_b, b_dw_b, ln_g_b, ln_b_b, w_dw_c, g_out, w_out, norm_mix, norm_ffn,
           w_ff_gate, w_ff_up, w_ff_down, w_router, b_router, w_ex_gate, w_ex_up, w_ex_down,
           norm_ple, w_ple_gate, w_ple_proj, norm_final):
    prm = dict(
        w_in=w_in.astype(BF16), rel_bias=rel_bias, w_dw_b=w_dw_b, b_dw_b=b_dw_b, ln_g_b=ln_g_b,
        ln_b_b=ln_b_b, w_dw_c=w_dw_c, g_out=g_out, w_out=w_out.astype(BF16), norm_mix=norm_mix,
        norm_ffn=norm_ffn, w_ff_gate=w_ff_gate.astype(BF16), w_ff_up=w_ff_up.astype(BF16),
        w_ff_down=w_ff_down.astype(BF16), w_router=w_router, b_router=b_router,
        w_ex_gate=w_ex_gate.astype(BF16), w_ex_up=w_ex_up.astype(BF16), w_ex_down=w_ex_down.astype(BF16),
        norm_ple=norm_ple, w_ple_gate=w_ple_gate.astype(BF16), w_ple_proj=w_ple_proj.astype(BF16),
        norm_final=norm_final)
    dec_t = x_sample.shape[1]
    y_p, k_p, v_p, b_p, c_p = _trunk(x_prompt, p_prompt, None, prm, 512, 512)
    y_s, k_s, v_s, b_s, c_s = _trunk(x_sample, p_sample,
                                     (cache_attn_k, cache_attn_v, state_conv_b, state_conv_c),
                                     prm, x_sample.shape[0] * dec_t, dec_t)
    return (y_p, y_s, k_p, v_p, b_p, c_p, k_s, v_s, b_s, c_s)
```

```python
import functools

import jax
import jax.numpy as jnp
from jax import lax
from jax.experimental import pallas as pl
from jax.experimental.pallas import tpu as pltpu

F32 = jnp.float32
BF16 = jnp.bfloat16

CHUNK = 64
BAND = 512
HEADS = 8
HEAD_DIM = 64
A_WIDTH = HEADS * HEAD_DIM
B_WIDTH = 256
C_WIDTH = 256
B_KERNEL = 31
C_KERNEL = 3
REL_CLIP = 128
N_EXPERTS = 8
EPS = 1e-6
NEG = -1e30
LOG2E = 1.4426950408889634

LANES = 128
SUBLANES = 8
ROW_SPLIT = 8
QTILE = 512
PAIR_ROWS = 2 * CHUNK
WIN = BAND + PAIR_ROWS
HIST_B = 32
HIST_C = 8
EXPERT_TILE = 512
FF_CHUNK = 512
DMA_UNROLL = 8
VMEM_LIMIT = 56 * 1024 * 1024


def _params(*sem):
    return pltpu.CompilerParams(dimension_semantics=sem, vmem_limit_bytes=VMEM_LIMIT)


def _rms(x, g):
    return x * lax.rsqrt(jnp.mean(x * x, axis=-1, keepdims=True) + EPS) * g


def _sigmoid(x):
    return 1.0 / (1.0 + jnp.exp(-x))


def _chunk(s, n):
    return pl.ds(s, n, stride=ROW_SPLIT)


def _row_tile(r):
    return pl.ds(pl.multiple_of(r * ROW_SPLIT, ROW_SPLIT), ROW_SPLIT)


def _inproj_kernel(x_ref, g_ref, w_ref, u_ref, kv_ref):
    hn = _rms(x_ref[...], g_ref[...]).astype(BF16)
    q = jnp.dot(hn, w_ref[:, :A_WIDTH], preferred_element_type=F32)
    u_ref[:, :A_WIDTH] = (q * (HEAD_DIM ** -0.5 * LOG2E)).astype(BF16)
    kv = jnp.dot(hn, w_ref[:, A_WIDTH:3 * A_WIDTH], preferred_element_type=F32)
    kv_ref[...] = kv
    u_ref[:, A_WIDTH:3 * A_WIDTH] = kv.astype(BF16)
    rest = jnp.dot(hn, w_ref[:, 3 * A_WIDTH:], preferred_element_type=F32)
    u_ref[:, 3 * A_WIDTH:] = rest.astype(BF16)


def _inproj(x, g, w, tm, t, keep):
    n, d = x.shape
    wi = w.shape[1]
    if keep == t:
        kv_index = lambda i: (i, 0)
    else:
        nt, ktiles = t // tm, keep // tm
        kv_index = lambda i: ((i // nt) * ktiles + jnp.maximum(i % nt - (nt - ktiles), 0), 0)
    return pl.pallas_call(
        _inproj_kernel,
        grid=(n // tm,),
        in_specs=[pl.BlockSpec((tm, d), lambda i: (i, 0)),
                  pl.BlockSpec((1, d), lambda i: (0, 0)),
                  pl.BlockSpec((d, wi), lambda i: (0, 0))],
        out_specs=[pl.BlockSpec((tm, wi), lambda i: (i, 0)),
                   pl.BlockSpec((tm, 2 * A_WIDTH), kv_index)],
        out_shape=[jax.ShapeDtypeStruct((n, wi), BF16),
                   jax.ShapeDtypeStruct((n // t * keep, 2 * A_WIDTH), F32)],
        compiler_params=_params("arbitrary"),
    )(x, g, w)


def _attn_kernel(q_ref, kp_ref, kc_ref, vp_ref, vc_ref, bias_ref, o_ref, k_scr, vt_scr, s_scr):
    i = pl.program_id(1)
    k_scr[0:QTILE, :] = kp_ref[...]
    k_scr[QTILE:2 * QTILE, :] = kc_ref[...]
    vt_scr[:, 0:QTILE] = vp_ref[...].astype(F32).T.astype(BF16)
    vt_scr[:, QTILE:2 * QTILE] = vc_ref[...].astype(F32).T.astype(BF16)
    lane = lax.broadcasted_iota(jnp.int32, (PAIR_ROWS, LANES), 1)
    row = lax.broadcasted_iota(jnp.int32, (WIN, 2 * PAIR_ROWS), 0)
    nblk = QTILE // PAIR_ROWS
    npair = HEADS // 2

    def scores(p):
        cols = slice(p * LANES, (p + 1) * LANES)
        for c2 in range(nblk):
            r0 = c2 * PAIR_ROWS
            kwin = k_scr[r0:r0 + WIN, cols]
            q2 = q_ref[r0:r0 + PAIR_ROWS, cols]
            zero = jnp.zeros_like(q2)
            qq = jnp.concatenate([jnp.where(lane < HEAD_DIM, q2, zero),
                                  jnp.where(lane >= HEAD_DIM, q2, zero)], axis=0)
            s_scr[(p % 2) * nblk + c2] = lax.dot_general(
                kwin, qq, (((1,), (1,)), ((), ())), preferred_element_type=F32)

    def tile(first):
        scores(0)
        for p in range(npair):
            if p + 1 < npair:
                scores(p + 1)
            cols = slice(p * LANES, (p + 1) * LANES)
            for c2 in range(nblk):
                r0 = c2 * PAIR_ROWS
                vwin = vt_scr[cols, r0:r0 + WIN]
                s = s_scr[(p % 2) * nblk + c2] + bias_ref[p]
                if first:
                    s = jnp.where(row < BAND - r0, NEG, s)
                m = jnp.max(s, axis=0, keepdims=True)
                e = jnp.exp2(s - m)
                l = jnp.sum(e, axis=0, keepdims=True)
                ot = jnp.dot(vwin, e.astype(BF16), preferred_element_type=F32)
                ot = ot * (1.0 / l)
                o = jnp.concatenate([ot[:HEAD_DIM, :PAIR_ROWS], ot[HEAD_DIM:, PAIR_ROWS:]], axis=0).T
                o_ref[r0:r0 + PAIR_ROWS, cols] = o.astype(o_ref.dtype)

    @pl.when(i == 0)
    def _():
        tile(True)

    @pl.when(i > 0)
    def _():
        tile(False)


def _prompt_attention(u, bias_t, batch, seq):
    nt = seq // QTILE
    prev = lambda b, i: b * nt + jnp.maximum(i - 1, 0)
    return pl.pallas_call(
        _attn_kernel,
        grid=(batch, nt),
        in_specs=[
            pl.BlockSpec((QTILE, A_WIDTH), lambda b, i: (b * nt + i, 0)),
            pl.BlockSpec((QTILE, A_WIDTH), lambda b, i: (prev(b, i), 1)),
            pl.BlockSpec((QTILE, A_WIDTH), lambda b, i: (b * nt + i, 1)),
            pl.BlockSpec((QTILE, A_WIDTH), lambda b, i: (prev(b, i), 2)),
            pl.BlockSpec((QTILE, A_WIDTH), lambda b, i: (b * nt + i, 2)),
            pl.BlockSpec((HEADS // 2, WIN, 2 * PAIR_ROWS), lambda b, i: (0, 0, 0)),
        ],
        out_specs=pl.BlockSpec((QTILE, A_WIDTH), lambda b, i: (b * nt + i, 0)),
        out_shape=jax.ShapeDtypeStruct((batch * seq, A_WIDTH), BF16),
        scratch_shapes=[pltpu.VMEM((2 * QTILE, A_WIDTH), BF16),
                        pltpu.VMEM((A_WIDTH, 2 * QTILE), BF16),
                        pltpu.VMEM((2 * QTILE // PAIR_ROWS, WIN, 2 * PAIR_ROWS), F32)],
        compiler_params=_params("arbitrary", "arbitrary"),
    )(u, u, u, u, u, bias_t)


def _hankel(v, rows, cols):
    heads, length = v.shape
    flat = jnp.tile(v, (1, rows + 1))[:, :rows * (length + 1)]
    return flat.reshape(heads, rows, length + 1)[:, :, :cols]


def _prompt_bias_table(rel_bias):
    by_dist = jnp.concatenate(
        [jnp.broadcast_to(rel_bias[:, :1], (HEADS, BAND)), rel_bias[:, :2 * REL_CLIP]], axis=1).astype(F32)
    by_sum = _hankel(by_dist, PAIR_ROWS, WIN + 1)
    table = jnp.transpose(by_sum[:, ::-1, 1:], (0, 2, 1))
    j = jnp.arange(WIN, dtype=jnp.int32)[:, None]
    lo = (jnp.arange(PAIR_ROWS, dtype=jnp.int32)[None, :] // CHUNK) * CHUNK
    valid = (j >= lo) & (j < lo + BAND + CHUNK)
    table = jnp.where(valid[None], table * LOG2E, NEG)
    table = table.reshape(HEADS // 2, 2, WIN, PAIR_ROWS)
    return jnp.transpose(table, (0, 2, 1, 3)).reshape(HEADS // 2, WIN, 2 * PAIR_ROWS)


def _sample_attn_kernel(q_ref, kn_ref, vn_ref, ck_ref, cv_ref, bc_ref, bn_ref, o_ref):
    t = q_ref.shape[0]
    lane = lax.broadcasted_iota(jnp.int32, (t, LANES), 1)
    kc = ck_ref[0].astype(BF16)
    vc = cv_ref[0].astype(BF16)
    for p in range(HEADS // 2):
        cols = slice(p * LANES, (p + 1) * LANES)
        q2 = q_ref[:, cols]
        acc = None
        for hh in range(2):
            h = 2 * p + hh
            sel = (lane >= HEAD_DIM) == bool(hh)
            qm = jnp.where(sel, q2, jnp.zeros_like(q2))
            nt_dims = (((1,), (1,)), ((), ()))
            s_c = lax.dot_general(qm, kc[:, cols], nt_dims, preferred_element_type=F32) + bc_ref[h]
            s_n = lax.dot_general(qm, kn_ref[:, cols], nt_dims, preferred_element_type=F32) + bn_ref[h]
            m = jnp.maximum(jnp.max(s_c, axis=-1, keepdims=True), jnp.max(s_n, axis=-1, keepdims=True))
            e_c = jnp.exp2(s_c - m)
            e_n = jnp.exp2(s_n - m)
            l = jnp.sum(e_c, axis=-1, keepdims=True) + jnp.sum(e_n, axis=-1, keepdims=True)
            pv = (jnp.dot(e_c.astype(BF16), vc[:, cols], preferred_element_type=F32)
                  + jnp.dot(e_n.astype(BF16), vn_ref[:, cols], preferred_element_type=F32)) * (1.0 / l)
            acc = pv if acc is None else jnp.where(sel, pv, acc)
        o_ref[:, cols] = acc.astype(o_ref.dtype)


def _sample_attention(u, cache_k, cache_v, bias_c, bias_n, batch, t):
    n_past = cache_k.shape[1]
    return pl.pallas_call(
        _sample_attn_kernel,
        grid=(batch,),
        in_specs=[
            pl.BlockSpec((t, A_WIDTH), lambda b: (b, 0)),
            pl.BlockSpec((t, A_WIDTH), lambda b: (b, 1)),
            pl.BlockSpec((t, A_WIDTH), lambda b: (b, 2)),
            pl.BlockSpec((1, n_past, A_WIDTH), lambda b: (b, 0, 0)),
            pl.BlockSpec((1, n_past, A_WIDTH), lambda b: (b, 0, 0)),
            pl.BlockSpec((HEADS, t, n_past), lambda b: (0, 0, 0)),
            pl.BlockSpec((HEADS, t, t), lambda b: (0, 0, 0)),
        ],
        out_specs=pl.BlockSpec((t, A_WIDTH), lambda b: (b, 0)),
        out_shape=jax.ShapeDtypeStruct((batch * t, A_WIDTH), BF16),
        compiler_params=_params("arbitrary"),
    )(u, u, u, cache_k, cache_v, bias_c, bias_n)


def _sample_bias_tables(rel_bias, t, n_past):
    def by_dist(lo, hi):
        dist = jnp.arange(lo, hi + 1, dtype=jnp.int32)
        return rel_bias[:, jnp.clip(dist, -REL_CLIP, REL_CLIP) + REL_CLIP].astype(F32) * LOG2E

    cache = by_dist(-n_past - t + 1, -1)
    new = by_dist(-t + 1, t - 1)
    return _hankel(cache, t, n_past)[:, ::-1], _hankel(new, t, t)[:, ::-1]


def _mix_kernel(x_ref, ya_ref, ab_ref, gc_ref, gb_ref, hc_ref, hb0_ref, hc0_ref,
                wb_ref, bb_ref, lg_ref, lb_ref, wc_ref, go_ref, wo_ref,
                h_ref, nb_ref, nc_ref, cb, cc, sh):
    tm = x_ref.shape[0]

    @pl.when(pl.program_id(1) == 0)
    def _():
        cb[0:HIST_B, :] = hb0_ref[0]
        cc[0:HIST_C, :] = hc0_ref[0]

    ab = ab_ref[...].astype(F32)
    cb[HIST_B:HIST_B + tm, :] = ab[:, :B_WIDTH] * _sigmoid(ab[:, B_WIDTH:])
    span = tm + HIST_B - SUBLANES
    for r in range(1, SUBLANES):
        sh[r - 1, 0:span, :] = cb[pl.ds(r, span), :]
    z = jnp.zeros((tm, B_WIDTH), F32) + bb_ref[...]
    for j in range(B_KERNEL):
        m = j + HIST_B - (B_KERNEL - 1)
        a = m // SUBLANES * SUBLANES
        src = cb[a:a + tm, :] if m == a else sh[m - a - 1, a:a + tm, :]
        z = z + wb_ref[j:j + 1, :] * src
    mu = jnp.mean(z, axis=-1, keepdims=True)
    zc = z - mu
    var = jnp.mean(zc * zc, axis=-1, keepdims=True)
    yb = zc * lax.rsqrt(var + EPS) * lg_ref[...] + lb_ref[...]
    yb = yb * _sigmoid(yb)

    cc[HIST_C:HIST_C + tm, :] = gc_ref[...].astype(F32) * hc_ref[...].astype(F32)
    conv = jnp.zeros((tm, C_WIDTH), F32)
    for j in range(C_KERNEL):
        conv = conv + wc_ref[j:j + 1, :] * cc[pl.ds(j + HIST_C - (C_KERNEL - 1), tm), :]
    yc = gb_ref[...].astype(F32) * conv

    go = go_ref[...]
    na = _rms(ya_ref[...].astype(F32), go[:, :A_WIDTH]).astype(BF16)
    nb = _rms(yb, go[:, A_WIDTH:A_WIDTH + B_WIDTH]).astype(BF16)
    nc = _rms(yc, go[:, A_WIDTH + B_WIDTH:]).astype(BF16)
    h_ref[...] = (x_ref[...]
                  + jnp.dot(na, wo_ref[0:A_WIDTH, :], preferred_element_type=F32)
                  + jnp.dot(nb, wo_ref[A_WIDTH:A_WIDTH + B_WIDTH, :], preferred_element_type=F32)
                  + jnp.dot(nc, wo_ref[A_WIDTH + B_WIDTH:, :], preferred_element_type=F32))

    tail_b = cb[tm:tm + HIST_B, :]
    tail_c = cc[tm:tm + HIST_C, :]
    nb_ref[0] = tail_b
    nc_ref[0] = tail_c
    cb[0:HIST_B, :] = tail_b
    cc[0:HIST_C, :] = tail_c


def _mix(x, ya, u, hist_b, hist_c, wb, bb, lg, lb, wc, go, wo, batch, t, tm):
    n, d = x.shape
    nt = t // tm
    row = lambda b, i: b * nt + i
    ab_blk = 3 * A_WIDTH // (2 * B_WIDTH)
    c_blk = (3 * A_WIDTH + 2 * B_WIDTH) // C_WIDTH
    const = lambda b, i: (0, 0)
    return pl.pallas_call(
        _mix_kernel,
        grid=(batch, nt),
        in_specs=[
            pl.BlockSpec((tm, d), lambda b, i: (row(b, i), 0)),
            pl.BlockSpec((tm, A_WIDTH), lambda b, i: (row(b, i), 0)),
            pl.BlockSpec((tm, 2 * B_WIDTH), lambda b, i: (row(b, i), ab_blk)),
            pl.BlockSpec((tm, C_WIDTH), lambda b, i: (row(b, i), c_blk)),
            pl.BlockSpec((tm, C_WIDTH), lambda b, i: (row(b, i), c_blk + 1)),
            pl.BlockSpec((tm, C_WIDTH), lambda b, i: (row(b, i), c_blk + 2)),
            pl.BlockSpec((1, HIST_B, B_WIDTH), lambda b, i: (b, 0, 0)),
            pl.BlockSpec((1, HIST_C, C_WIDTH), lambda b, i: (b, 0, 0)),
            pl.BlockSpec(wb.shape, const), pl.BlockSpec(bb.shape, const),
            pl.BlockSpec(lg.shape, const), pl.BlockSpec(lb.shape, const),
            pl.BlockSpec(wc.shape, const), pl.BlockSpec(go.shape, const),
            pl.BlockSpec(wo.shape, const),
        ],
        out_specs=[pl.BlockSpec((tm, d), lambda b, i: (row(b, i), 0)),
                   pl.BlockSpec((1, HIST_B, B_WIDTH), lambda b, i: (b, 0, 0)),
                   pl.BlockSpec((1, HIST_C, C_WIDTH), lambda b, i: (b, 0, 0))],
        out_shape=[jax.ShapeDtypeStruct((n, d), F32),
                   jax.ShapeDtypeStruct((batch, HIST_B, B_WIDTH), F32),
                   jax.ShapeDtypeStruct((batch, HIST_C, C_WIDTH), F32)],
        scratch_shapes=[pltpu.VMEM((HIST_B + tm, B_WIDTH), F32),
                        pltpu.VMEM((HIST_C + tm, C_WIDTH), F32),
                        pltpu.VMEM((SUBLANES - 1, HIST_B + tm - SUBLANES, B_WIDTH), F32)],
        compiler_params=_params("arbitrary", "arbitrary"),
    )(x, ya, u, u, u, u, hist_b, hist_c, wb, bb, lg, lb, wc, go, wo)


def _ple_epilogue(h, p, gp, wpg, wpp, gf):
    gate = _sigmoid(jnp.dot(_rms(h, gp).astype(BF16), wpg, preferred_element_type=F32))
    h = h + gate * jnp.dot(p.astype(BF16), wpp, preferred_element_type=F32)
    return h if gf is None else _rms(h, gf)


def _swiglu_chunks(x, wg_ref, wu_ref, wd_ref, acc):
    f = wg_ref.shape[-1]
    for c in range(f // FF_CHUNK):
        cols = slice(c * FF_CHUNK, (c + 1) * FF_CHUNK)
        a = jnp.dot(x, wg_ref[0, :, cols], preferred_element_type=F32)
        b = jnp.dot(x, wu_ref[0, :, cols], preferred_element_type=F32)
        mid = (a * _sigmoid(a) * b).astype(BF16)
        part = jnp.dot(mid, wd_ref[0, cols, :], preferred_element_type=F32)
        if c == 0:
            acc[...] = part
        else:
            acc[...] += part


def _ffn_kernel(final, h_ref, g_ref, wg_ref, wu_ref, wd_ref, p_ref, gp_ref, wpg_ref, wpp_ref, gf_ref,
                o_ref, acc):
    xn = _rms(h_ref[...], g_ref[...]).astype(BF16)
    _swiglu_chunks(xn, wg_ref, wu_ref, wd_ref, acc)
    h = h_ref[...] + acc[...]
    o_ref[...] = _ple_epilogue(h, p_ref[0], gp_ref[...], wpg_ref[...], wpp_ref[...],
                               gf_ref[...] if final else None)


def _resident(shape):
    return pl.BlockSpec(shape, lambda *_: (0,) * len(shape), pipeline_mode=pl.Buffered(1))


def _ffn(h, g, wg, wu, wd, j, p, layer, gp, wpg, wpp, gf, final, tm):
    n, d = h.shape
    f = wg.shape[-1]
    const = lambda i: (0, 0)
    return pl.pallas_call(
        functools.partial(_ffn_kernel, final),
        grid=(n // tm,),
        in_specs=[
            pl.BlockSpec((tm, d), lambda i: (i, 0)),
            pl.BlockSpec((1, d), const),
            pl.BlockSpec((1, d, f), lambda i: (j, 0, 0), pipeline_mode=pl.Buffered(1)),
            pl.BlockSpec((1, d, f), lambda i: (j, 0, 0), pipeline_mode=pl.Buffered(1)),
            pl.BlockSpec((1, f, d), lambda i: (j, 0, 0), pipeline_mode=pl.Buffered(1)),
            pl.BlockSpec((1, tm, p.shape[-1]), lambda i: (layer, i, 0)),
            pl.BlockSpec((1, d), const),
            _resident(wpg.shape),
            _resident(wpp.shape),
            pl.BlockSpec((1, d), const),
        ],
        out_specs=pl.BlockSpec((tm, d), lambda i: (i, 0)),
        out_shape=jax.ShapeDtypeStruct((n, d), F32),
        scratch_shapes=[pltpu.VMEM((tm, d), F32)],
        compiler_params=_params("arbitrary"),
    )(h, g, wg, wu, wd, p, gp, wpg, wpp, gf)


def _router_kernel(h_ref, g_ref, wr_ref, br_ref, x3_ref, meta_ref, cnt_ref, carry):
    tm = h_ref.shape[0]

    @pl.when(pl.program_id(0) == 0)
    def _():
        carry[...] = jnp.zeros_like(carry)

    xn = _rms(h_ref[...], g_ref[...])
    for s in range(ROW_SPLIT):
        x3_ref[_chunk(s, tm), :] = xn[:, s * LANES:(s + 1) * LANES]
    logits = lax.dot_general(wr_ref[...], xn, (((1,), (1,)), ((), ())),
                             precision=lax.Precision.HIGHEST,
                             preferred_element_type=F32) + br_ref[...]
    eidx = lax.broadcasted_iota(jnp.int32, (N_EXPERTS, tm), 0)
    m1 = jnp.max(logits, axis=0, keepdims=True)
    i1 = jnp.min(jnp.where(logits == m1, eidx, N_EXPERTS), axis=0, keepdims=True)
    rest = jnp.where(eidx == i1, -jnp.inf, logits)
    m2 = jnp.max(rest, axis=0, keepdims=True)
    i2 = jnp.min(jnp.where(rest == m2, eidx, N_EXPERTS), axis=0, keepdims=True)
    e2 = jnp.exp(m2 - m1)
    g1 = 1.0 / (1.0 + e2)
    g2 = e2 / (1.0 + e2)
    hit1 = eidx == i1
    hit2 = eidx == i2
    chosen = jnp.where(hit1 | hit2, 1.0, 0.0)
    src = lax.broadcasted_iota(jnp.int32, (tm, tm), 0)
    dst = lax.broadcasted_iota(jnp.int32, (tm, tm), 1)
    before = jnp.where(src < dst, 1.0, 0.0).astype(BF16)
    rank = jnp.dot(chosen.astype(BF16), before, preferred_element_type=F32) + carry[:, 0:1]
    r1 = jnp.sum(jnp.where(hit1, rank, 0.0), axis=0, keepdims=True)
    r2 = jnp.sum(jnp.where(hit2, rank, 0.0), axis=0, keepdims=True)
    zero = jnp.zeros_like(g1)
    meta_ref[...] = jnp.concatenate(
        [i1.astype(F32), i2.astype(F32), g1, g2, r1, r2, zero, zero], axis=0)
    carry[...] = carry[...] + jnp.sum(chosen, axis=1, keepdims=True)
    cnt_ref[...] = carry[...]


def _router(h, g, wr_t, br, tm):
    n, d = h.shape
    return pl.pallas_call(
        _router_kernel,
        grid=(n // tm,),
        in_specs=[pl.BlockSpec((tm, d), lambda i: (i, 0)),
                  pl.BlockSpec((1, d), lambda i: (0, 0)),
                  pl.BlockSpec((N_EXPERTS, d), lambda i: (0, 0)),
                  pl.BlockSpec((N_EXPERTS, 1), lambda i: (0, 0))],
        out_specs=[pl.BlockSpec((tm * ROW_SPLIT, LANES), lambda i: (i, 0)),
                   pl.BlockSpec((8, tm), lambda i: (0, i)),
                   pl.BlockSpec((N_EXPERTS, LANES), lambda i: (0, 0))],
        out_shape=[jax.ShapeDtypeStruct((n * ROW_SPLIT, LANES), F32),
                   jax.ShapeDtypeStruct((8, n), F32),
                   jax.ShapeDtypeStruct((N_EXPERTS, LANES), F32)],
        scratch_shapes=[pltpu.VMEM((N_EXPERTS, LANES), F32)],
        compiler_params=_params("arbitrary"),
    )(h, g, wr_t, br)


def _dispatch_kernel(d1_ref, d2_ref, x_ref, o_hbm, sem):
    tm = x_ref.shape[0] // ROW_SPLIT

    def issue(g, carry):
        for k in range(DMA_UNROLL):
            t = g * DMA_UNROLL + k
            pltpu.make_async_copy(x_ref.at[_row_tile(t)], o_hbm.at[_row_tile(d1_ref[0, 0, t])], sem).start()
            pltpu.make_async_copy(x_ref.at[_row_tile(t)], o_hbm.at[_row_tile(d2_ref[0, 0, t])], sem).start()
        return carry

    lax.fori_loop(0, tm // DMA_UNROLL, issue, 0)
    for _ in range(2):
        pltpu.make_async_copy(x_ref, o_hbm.at[pl.ds(0, tm * ROW_SPLIT)], sem).wait()


def _dispatch(x3, d1, d2, rows_out, tm):
    n = x3.shape[0] // ROW_SPLIT
    steps = d1.shape[0] // tm
    last = n // tm - 1
    idx_spec = pl.BlockSpec((1, 1, tm), lambda i: (i, 0, 0), memory_space=pltpu.SMEM)
    return pl.pallas_call(
        _dispatch_kernel,
        grid=(steps,),
        in_specs=[idx_spec, idx_spec,
                  pl.BlockSpec((tm * ROW_SPLIT, LANES), lambda i: (jnp.minimum(i, last), 0))],
        out_specs=pl.BlockSpec(memory_space=pl.ANY),
        out_shape=jax.ShapeDtypeStruct((rows_out * ROW_SPLIT, LANES), x3.dtype),
        scratch_shapes=[pltpu.SemaphoreType.DMA(())],
        compiler_params=_params("arbitrary"),
    )(d1.reshape(steps, 1, tm), d2.reshape(steps, 1, tm), x3)


def _expert_kernel(texp_ref, nact_ref, x_ref, wg_ref, wu_ref, wd_ref, y_ref, x2, acc):
    del texp_ref

    @pl.when(pl.program_id(0) < nact_ref[0])
    def _():
        for s in range(ROW_SPLIT):
            x2[:, s * LANES:(s + 1) * LANES] = x_ref[_chunk(s, EXPERT_TILE), :].astype(BF16)
        _swiglu_chunks(x2[...], wg_ref, wu_ref, wd_ref, acc)
        for s in range(ROW_SPLIT):
            y_ref[_chunk(s, EXPERT_TILE), :] = acc[:, s * LANES:(s + 1) * LANES]

    @pl.when(pl.program_id(0) >= nact_ref[0])
    def _():
        y_ref[...] = jnp.zeros_like(y_ref)


def _experts(xs, tile_exp, n_active, wg, wu, wd):
    ntile = tile_exp.shape[0]
    d, f = wg.shape[1], wg.shape[2]
    grid_spec = pltpu.PrefetchScalarGridSpec(
        num_scalar_prefetch=2,
        grid=(ntile,),
        in_specs=[
            pl.BlockSpec((EXPERT_TILE * ROW_SPLIT, LANES), lambda t, te, na: (t, 0)),
            pl.BlockSpec((1, d, f), lambda t, te, na: (te[t], 0, 0)),
            pl.BlockSpec((1, d, f), lambda t, te, na: (te[t], 0, 0)),
            pl.BlockSpec((1, f, d), lambda t, te, na: (te[t], 0, 0)),
        ],
        out_specs=pl.BlockSpec((EXPERT_TILE * ROW_SPLIT, LANES), lambda t, te, na: (t, 0)),
        scratch_shapes=[pltpu.VMEM((EXPERT_TILE, d), BF16), pltpu.VMEM((EXPERT_TILE, d), F32)],
    )
    return pl.pallas_call(
        _expert_kernel,
        grid_spec=grid_spec,
        out_shape=jax.ShapeDtypeStruct(xs.shape, F32),
        compiler_params=_params("arbitrary"),
    )(tile_exp, n_active, xs, wg, wu, wd)


def _combine_kernel(final, d1a, d2a, d1b, d2b, d1c, d2c, ye_hbm, h_ref, gate_ref, p_ref, gp_ref, wpg_ref, wpp_ref,
                    gf_ref, o_ref, ya1, ya2, yb1, yb2, moe, sems):
    j = pl.program_id(0)
    tm = h_ref.shape[0] // 2

    def gather(d1_ref, d2_ref, y1, y2, sem):
        for t in range(tm):
            pltpu.make_async_copy(ye_hbm.at[_row_tile(d1_ref[0, 0, t])], y1.at[_row_tile(t)], sem).start()
            pltpu.make_async_copy(ye_hbm.at[_row_tile(d2_ref[0, 0, t])], y2.at[_row_tile(t)], sem).start()

    def wait(y1, y2, sem):
        pltpu.make_async_copy(ye_hbm.at[pl.ds(0, tm * ROW_SPLIT)], y1, sem).wait()
        pltpu.make_async_copy(ye_hbm.at[pl.ds(0, tm * ROW_SPLIT)], y2, sem).wait()

    def combine(y1, y2, rows):
        g1 = gate_ref[rows, 0:1]
        g2 = gate_ref[rows, 1:2]
        for s in range(ROW_SPLIT):
            moe[:, s * LANES:(s + 1) * LANES] = g1 * y1[_chunk(s, tm), :] + g2 * y2[_chunk(s, tm), :]
        h = h_ref[rows, :] + moe[...]
        o_ref[rows, :] = _ple_epilogue(h, p_ref[0, rows, :], gp_ref[...], wpg_ref[...], wpp_ref[...],
                                       gf_ref[...] if final else None)

    @pl.when(j == 0)
    def _():
        gather(d1a, d2a, ya1, ya2, sems.at[0])

    wait(ya1, ya2, sems.at[0])
    gather(d1b, d2b, yb1, yb2, sems.at[1])
    combine(ya1, ya2, slice(0, tm))
    wait(yb1, yb2, sems.at[1])
    gather(d1c, d2c, ya1, ya2, sems.at[0])
    combine(yb1, yb2, slice(tm, 2 * tm))

    @pl.when(j == pl.num_programs(0) - 1)
    def _():
        wait(ya1, ya2, sems.at[0])


def _combine(ye, d1, d2, h, gates, p, layer, gp, wpg, wpp, gf, final, tm):
    n, d = h.shape
    steps = n // (2 * tm)
    tiles = n // tm
    idx_spec = lambda f: pl.BlockSpec((1, 1, tm), lambda i: (f(i), 0, 0), memory_space=pltpu.SMEM)
    first = idx_spec(lambda i: 2 * i)
    second = idx_spec(lambda i: 2 * i + 1)
    ahead = idx_spec(lambda i: jnp.minimum(2 * i + 2, tiles - 1))
    const = lambda i: (0, 0)
    d1 = d1.reshape(tiles, 1, tm)
    d2 = d2.reshape(tiles, 1, tm)
    row_buf = pltpu.VMEM((tm * ROW_SPLIT, LANES), F32)
    return pl.pallas_call(
        functools.partial(_combine_kernel, final),
        grid=(steps,),
        in_specs=[first, first, second, second, ahead, ahead,
                  pl.BlockSpec(memory_space=pl.ANY),
                  pl.BlockSpec((2 * tm, d), lambda i: (i, 0)),
                  pl.BlockSpec((2 * tm, 2), lambda i: (i, 0)),
                  pl.BlockSpec((1, 2 * tm, p.shape[-1]), lambda i: (layer, i, 0)),
                  pl.BlockSpec((1, d), const),
                  _resident(wpg.shape),
                  _resident(wpp.shape),
                  pl.BlockSpec((1, d), const)],
        out_specs=pl.BlockSpec((2 * tm, d), lambda i: (i, 0)),
        out_shape=jax.ShapeDtypeStruct((n, d), F32),
        scratch_shapes=[row_buf, row_buf, row_buf, row_buf,
                        pltpu.VMEM((tm, d), F32),
                        pltpu.SemaphoreType.DMA((2,))],
        compiler_params=_params("arbitrary"),
    )(d1, d2, d1, d2, d1, d2, ye, h, gates, p, gp, wpg, wpp, gf)


def _moe(h, g, wr_t, br, wg, wu, wd, p, layer, gp, wpg, wpp, gf, final, tm):
    n = h.shape[0]
    x3, meta, cnt = _router(h, g, wr_t, br, tm)
    counts = cnt[:, 0].astype(jnp.int32)
    tiles = (counts + EXPERT_TILE - 1) // EXPERT_TILE
    padded = tiles * EXPERT_TILE
    ends = jnp.cumsum(padded)
    offs = ends - padded
    expert = jnp.arange(N_EXPERTS, dtype=jnp.int32)[:, None]
    start_of = lambda idx: jnp.sum(jnp.where(idx[None, :] == expert, offs[:, None], 0), axis=0)
    d1 = start_of(meta[0].astype(jnp.int32)) + meta[4].astype(jnp.int32)
    d2 = start_of(meta[1].astype(jnp.int32)) + meta[5].astype(jnp.int32)
    gates = jnp.stack([meta[2], meta[3]], axis=1)
    ntile = 2 * n // EXPERT_TILE + N_EXPERTS
    r = jnp.arange(EXPERT_TILE, dtype=jnp.int32)[None, :]
    in_group = counts[:, None] + r < padded[:, None]
    spare = ends[-1] + (jnp.cumsum(jnp.where(in_group, 0, 1).reshape(-1)) - 1).reshape(in_group.shape)
    fill = jnp.where(in_group, offs[:, None] + counts[:, None] + r, spare).reshape(-1).astype(jnp.int32)
    half = fill.shape[0] // 2
    xs = _dispatch(x3, jnp.concatenate([d1, fill[:half]]), jnp.concatenate([d2, fill[half:]]),
                   ntile * EXPERT_TILE, tm)
    n_active = jnp.sum(tiles).astype(jnp.int32)
    tile_start = jnp.minimum(jnp.arange(ntile, dtype=jnp.int32), n_active - 1) * EXPERT_TILE
    tile_exp = jnp.minimum(jnp.sum(jnp.where(tile_start[:, None] >= ends[None, :], 1, 0), axis=1),
                           N_EXPERTS - 1).astype(jnp.int32)
    ye = _experts(xs, tile_exp, n_active.reshape(1), wg, wu, wd)
    return _combine(ye, d1, d2, h, gates, p, layer, gp, wpg, wpp, gf, final, min(tm, n // 2))


def _trunk(x, p, past, prm, tok_tile, mix_tile):
    batch, t, d = x.shape
    n = batch * t
    depth = prm["w_in"].shape[0]
    h = x.reshape(n, d)
    new_k, new_v, new_b, new_c = [], [], [], []
    row = lambda a: a.reshape(1, -1)
    p = p.reshape(depth, n, -1)
    keep = min(BAND, t) if past is None else t
    for i in range(depth):
        u, kv = _inproj(h, row(prm["norm_mix"][i]), prm["w_in"][i], tok_tile, t, keep)
        new_k.append(kv[:, :A_WIDTH].reshape(batch, keep, HEADS, HEAD_DIM))
        new_v.append(kv[:, A_WIDTH:].reshape(batch, keep, HEADS, HEAD_DIM))
        if past is None:
            ya = _prompt_attention(u, _prompt_bias_table(prm["rel_bias"][i]), batch, t)
            hist_b = jnp.zeros((batch, HIST_B, B_WIDTH), F32)
            hist_c = jnp.zeros((batch, HIST_C, C_WIDTH), F32)
        else:
            ck, cv, sb, sc = past
            n_past = ck.shape[2]
            bias_c, bias_n = _sample_bias_tables(prm["rel_bias"][i], t, n_past)
            ya = _sample_attention(u, ck[i].reshape(batch, n_past, A_WIDTH),
                                   cv[i].reshape(batch, n_past, A_WIDTH), bias_c, bias_n, batch, t)
            hist_b = jnp.pad(sb[i], ((0, 0), (HIST_B - (B_KERNEL - 1), 0), (0, 0)))
            hist_c = jnp.pad(sc[i], ((0, 0), (HIST_C - (C_KERNEL - 1), 0), (0, 0)))
        wb = jnp.pad(prm["w_dw_b"][i], ((0, HIST_B - B_KERNEL), (0, 0)))
        wc = jnp.pad(prm["w_dw_c"][i], ((0, HIST_C - C_KERNEL), (0, 0)))
        h, nb, nc = _mix(h, ya, u, hist_b, hist_c, wb, row(prm["b_dw_b"][i]), row(prm["ln_g_b"][i]),
                         row(prm["ln_b_b"][i]), wc, row(prm["g_out"][i]), prm["w_out"][i],
                         batch, t, mix_tile)
        new_b.append(nb[:, HIST_B - (B_KERNEL - 1):])
        new_c.append(nc[:, HIST_C - (C_KERNEL - 1):])
        final = i == depth - 1
        tail = (p, i, row(prm["norm_ple"][i]), prm["w_ple_gate"][i], prm["w_ple_proj"][i],
                row(prm["norm_final"]), final)
        j = i // 2
        if i % 2 == 0:
            h = _ffn(h, row(prm["norm_ffn"][i]), prm["w_ff_gate"], prm["w_ff_up"], prm["w_ff_down"], j,
                     *tail, tok_tile)
        else:
            h = _moe(h, row(prm["norm_ffn"][i]), prm["w_router"][j].T, prm["b_router"][j].reshape(-1, 1),
                     prm["w_ex_gate"][j], prm["w_ex_up"][j], prm["w_ex_down"][j], *tail, tok_tile)
    return (h.reshape(batch, t, d), jnp.stack(new_k), jnp.stack(new_v), jnp.stack(new_b), jnp.stack(new_c))


def kernel(x_prompt, x_sample, p_prompt, p_sample, cache_attn_k, cache_attn_v, state_conv_b, state_conv_c,
           w_in, rel_bias, w_dw_b, b_dw_b, ln_g_b, ln_b_b, w_dw_c, g_out, w_out, norm_mix, norm_ffn,
           w_ff_gate, w_ff_up, w_ff_down, w_router, b_router, w_ex_gate, w_ex_up, w_ex_down,
           norm_ple, w_ple_gate, w_ple_proj, norm_final):
    prm = dict(
        w_in=w_in.astype(BF16), rel_bias=rel_bias, w_dw_b=w_dw_b, b_dw_b=b_dw_b, ln_g_b=ln_g_b,
        ln_b_b=ln_b_b, w_dw_c=w_dw_c, g_out=g_out, w_out=w_out.astype(BF16), norm_mix=norm_mix,
        norm_ffn=norm_ffn, w_ff_gate=w_ff_gate.astype(BF16), w_ff_up=w_ff_up.astype(BF16),
        w_ff_down=w_ff_down.astype(BF16), w_router=w_router, b_router=b_router,
        w_ex_gate=w_ex_gate.astype(BF16), w_ex_up=w_ex_up.astype(BF16), w_ex_down=w_ex_down.astype(BF16),
        norm_ple=norm_ple, w_ple_gate=w_ple_gate.astype(BF16), w_ple_proj=w_ple_proj.astype(BF16),
        norm_final=norm_final)
    dec_t = x_sample.shape[1]
    y_p, k_p, v_p, b_p, c_p = _trunk(x_prompt, p_prompt, None, prm, 512, 512)
    y_s, k_s, v_s, b_s, c_s = _trunk(x_sample, p_sample,
                                     (cache_attn_k, cache_attn_v, state_conv_b, state_conv_c),
                                     prm, x_sample.shape[0] * dec_t, dec_t)
    return (y_p, y_s, k_p, v_p, b_p, c_p, k_s, v_s, b_s, c_s)
```

```python
import functools

import jax
import jax.numpy as jnp
from jax import lax
from jax.experimental import pallas as pl
from jax.experimental.pallas import tpu as pltpu

F32 = jnp.float32
BF16 = jnp.bfloat16

CHUNK = 64
BAND = 512
HEADS = 8
HEAD_DIM = 64
A_WIDTH = HEADS * HEAD_DIM
B_WIDTH = 256
C_WIDTH = 256
B_KERNEL = 31
C_KERNEL = 3
REL_CLIP = 128
N_EXPERTS = 8
EPS = 1e-6
NEG = -1e30
LOG2E = 1.4426950408889634

LANES = 128
SUBLANES = 8
ROW_SPLIT = 8
QTILE = 512
PAIR_ROWS = 2 * CHUNK
WIN = BAND + PAIR_ROWS
HIST_B = 32
HIST_C = 8
EXPERT_TILE = 512
FF_CHUNK = 512
MIX_SHIFT_BREAKS = (3, 7)
MIX_TAP_BREAKS = (7, 15, 23, 30)
DMA_UNROLL = 8
VMEM_LIMIT = 56 * 1024 * 1024
FUSED_VMEM_LIMIT = 60 * 1024 * 1024


def _params(*sem, vmem=VMEM_LIMIT):
    return pltpu.CompilerParams(dimension_semantics=sem, vmem_limit_bytes=vmem)


def _rms(x, g):
    return x * lax.rsqrt(jnp.mean(x * x, axis=-1, keepdims=True) + EPS) * g


def _sigmoid(x):
    return 1.0 / (1.0 + jnp.exp(-x))


def _chunk(s, n):
    return pl.ds(s, n, stride=ROW_SPLIT)


def _row_tile(r):
    return pl.ds(pl.multiple_of(r * ROW_SPLIT, ROW_SPLIT), ROW_SPLIT)


def _inproj_kernel(x_ref, g_ref, w_ref, u_ref, kv_ref):
    hn = _rms(x_ref[...], g_ref[...]).astype(BF16)
    q = jnp.dot(hn, w_ref[:, :A_WIDTH], preferred_element_type=F32)
    u_ref[:, :A_WIDTH] = (q * (HEAD_DIM ** -0.5 * LOG2E)).astype(BF16)
    kv = jnp.dot(hn, w_ref[:, A_WIDTH:3 * A_WIDTH], preferred_element_type=F32)
    kv_ref[...] = kv
    u_ref[:, A_WIDTH:3 * A_WIDTH] = kv.astype(BF16)
    rest = jnp.dot(hn, w_ref[:, 3 * A_WIDTH:], preferred_element_type=F32)
    u_ref[:, 3 * A_WIDTH:] = rest.astype(BF16)


def _inproj(x, g, w, tm, t, keep):
    n, d = x.shape
    wi = w.shape[1]
    if keep == t:
        kv_index = lambda i: (i, 0)
    else:
        nt, ktiles = t // tm, keep // tm
        kv_index = lambda i: ((i // nt) * ktiles + jnp.maximum(i % nt - (nt - ktiles), 0), 0)
    return pl.pallas_call(
        _inproj_kernel,
        grid=(n // tm,),
        in_specs=[pl.BlockSpec((tm, d), lambda i: (i, 0)),
                  pl.BlockSpec((1, d), lambda i: (0, 0)),
                  pl.BlockSpec((d, wi), lambda i: (0, 0))],
        out_specs=[pl.BlockSpec((tm, wi), lambda i: (i, 0)),
                   pl.BlockSpec((tm, 2 * A_WIDTH), kv_index)],
        out_shape=[jax.ShapeDtypeStruct((n, wi), BF16),
                   jax.ShapeDtypeStruct((n // t * keep, 2 * A_WIDTH), F32)],
        compiler_params=_params("arbitrary"),
    )(x, g, w)


def _attn_kernel(q_ref, kp_ref, kc_ref, vp_ref, vc_ref, bias_ref, o_ref, k_scr, vt_scr, s_scr):
    i = pl.program_id(1)
    k_scr[0:QTILE, :] = kp_ref[...]
    k_scr[QTILE:2 * QTILE, :] = kc_ref[...]
    vt_scr[:, 0:QTILE] = vp_ref[...].astype(F32).T.astype(BF16)
    vt_scr[:, QTILE:2 * QTILE] = vc_ref[...].astype(F32).T.astype(BF16)
    lane = lax.broadcasted_iota(jnp.int32, (PAIR_ROWS, LANES), 1)
    row = lax.broadcasted_iota(jnp.int32, (WIN, 2 * PAIR_ROWS), 0)
    nblk = QTILE // PAIR_ROWS
    npair = HEADS // 2

    def scores(p):
        cols = slice(p * LANES, (p + 1) * LANES)
        for c2 in range(nblk):
            r0 = c2 * PAIR_ROWS
            kwin = k_scr[r0:r0 + WIN, cols]
            q2 = q_ref[r0:r0 + PAIR_ROWS, cols]
            zero = jnp.zeros_like(q2)
            qq = jnp.concatenate([jnp.where(lane < HEAD_DIM, q2, zero),
                                  jnp.where(lane >= HEAD_DIM, q2, zero)], axis=0)
            s_scr[(p % 2) * nblk + c2] = lax.dot_general(
                kwin, qq, (((1,), (1,)), ((), ())), preferred_element_type=F32)

    def tile(first):
        scores(0)
        for p in range(npair):
            if p + 1 < npair:
                scores(p + 1)
            cols = slice(p * LANES, (p + 1) * LANES)
            for c2 in range(nblk):
                r0 = c2 * PAIR_ROWS
                vwin = vt_scr[cols, r0:r0 + WIN]
                s = s_scr[(p % 2) * nblk + c2] + bias_ref[p]
                if first:
                    s = jnp.where(row < BAND - r0, NEG, s)
                m = jnp.max(s, axis=0, keepdims=True)
                e = jnp.exp2(s - m)
                l = jnp.sum(e, axis=0, keepdims=True)
                ot = jnp.dot(vwin, e.astype(BF16), preferred_element_type=F32)
                ot = ot * (1.0 / l)
                o = jnp.concatenate([ot[:HEAD_DIM, :PAIR_ROWS], ot[HEAD_DIM:, PAIR_ROWS:]], axis=0).T
                o_ref[r0:r0 + PAIR_ROWS, cols] = o.astype(o_ref.dtype)

    @pl.when(i == 0)
    def _():
        tile(True)

    @pl.when(i > 0)
    def _():
        tile(False)


def _prompt_attention(u, bias_t, batch, seq):
    nt = seq // QTILE
    prev = lambda b, i: b * nt + jnp.maximum(i - 1, 0)
    return pl.pallas_call(
        _attn_kernel,
        grid=(batch, nt),
        in_specs=[
            pl.BlockSpec((QTILE, A_WIDTH), lambda b, i: (b * nt + i, 0)),
            pl.BlockSpec((QTILE, A_WIDTH), lambda b, i: (prev(b, i), 1)),
            pl.BlockSpec((QTILE, A_WIDTH), lambda b, i: (b * nt + i, 1)),
            pl.BlockSpec((QTILE, A_WIDTH), lambda b, i: (prev(b, i), 2)),
            pl.BlockSpec((QTILE, A_WIDTH), lambda b, i: (b * nt + i, 2)),
            pl.BlockSpec((HEADS // 2, WIN, 2 * PAIR_ROWS), lambda b, i: (0, 0, 0)),
        ],
        out_specs=pl.BlockSpec((QTILE, A_WIDTH), lambda b, i: (b * nt + i, 0)),
        out_shape=jax.ShapeDtypeStruct((batch * seq, A_WIDTH), BF16),
        scratch_shapes=[pltpu.VMEM((2 * QTILE, A_WIDTH), BF16),
                        pltpu.VMEM((A_WIDTH, 2 * QTILE), BF16),
                        pltpu.VMEM((2 * QTILE // PAIR_ROWS, WIN, 2 * PAIR_ROWS), F32)],
        compiler_params=_params("arbitrary", "arbitrary"),
    )(u, u, u, u, u, bias_t)


def _hankel(v, rows, cols):
    heads, length = v.shape
    flat = jnp.tile(v, (1, rows + 1))[:, :rows * (length + 1)]
    return flat.reshape(heads, rows, length + 1)[:, :, :cols]


def _prompt_bias_table(rel_bias):
    by_dist = jnp.concatenate(
        [jnp.broadcast_to(rel_bias[:, :1], (HEADS, BAND)), rel_bias[:, :2 * REL_CLIP]], axis=1).astype(F32)
    by_sum = _hankel(by_dist, PAIR_ROWS, WIN + 1)
    table = jnp.transpose(by_sum[:, ::-1, 1:], (0, 2, 1))
    j = jnp.arange(WIN, dtype=jnp.int32)[:, None]
    lo = (jnp.arange(PAIR_ROWS, dtype=jnp.int32)[None, :] // CHUNK) * CHUNK
    valid = (j >= lo) & (j < lo + BAND + CHUNK)
    table = jnp.where(valid[None], table * LOG2E, NEG)
    table = table.reshape(HEADS // 2, 2, WIN, PAIR_ROWS)
    return jnp.transpose(table, (0, 2, 1, 3)).reshape(HEADS // 2, WIN, 2 * PAIR_ROWS)


def _sample_attn_kernel(q_ref, kn_ref, vn_ref, ck_ref, cv_ref, bc_ref, bn_ref, o_ref):
    t = q_ref.shape[0]
    lane = lax.broadcasted_iota(jnp.int32, (t, LANES), 1)
    kc = ck_ref[0].astype(BF16)
    vc = cv_ref[0].astype(BF16)
    for p in range(HEADS // 2):
        cols = slice(p * LANES, (p + 1) * LANES)
        q2 = q_ref[:, cols]
        acc = None
        for hh in range(2):
            h = 2 * p + hh
            sel = (lane >= HEAD_DIM) == bool(hh)
            qm = jnp.where(sel, q2, jnp.zeros_like(q2))
            nt_dims = (((1,), (1,)), ((), ()))
            s_c = lax.dot_general(qm, kc[:, cols], nt_dims, preferred_element_type=F32) + bc_ref[h]
            s_n = lax.dot_general(qm, kn_ref[:, cols], nt_dims, preferred_element_type=F32) + bn_ref[h]
            m = jnp.maximum(jnp.max(s_c, axis=-1, keepdims=True), jnp.max(s_n, axis=-1, keepdims=True))
            e_c = jnp.exp2(s_c - m)
            e_n = jnp.exp2(s_n - m)
            l = jnp.sum(e_c, axis=-1, keepdims=True) + jnp.sum(e_n, axis=-1, keepdims=True)
            pv = (jnp.dot(e_c.astype(BF16), vc[:, cols], preferred_element_type=F32)
                  + jnp.dot(e_n.astype(BF16), vn_ref[:, cols], preferred_element_type=F32)) * (1.0 / l)
            acc = pv if acc is None else jnp.where(sel, pv, acc)
        o_ref[:, cols] = acc.astype(o_ref.dtype)


def _sample_attention(u, cache_k, cache_v, bias_c, bias_n, batch, t):
    n_past = cache_k.shape[1]
    return pl.pallas_call(
        _sample_attn_kernel,
        grid=(batch,),
        in_specs=[
            pl.BlockSpec((t, A_WIDTH), lambda b: (b, 0)),
            pl.BlockSpec((t, A_WIDTH), lambda b: (b, 1)),
            pl.BlockSpec((t, A_WIDTH), lambda b: (b, 2)),
            pl.BlockSpec((1, n_past, A_WIDTH), lambda b: (b, 0, 0)),
            pl.BlockSpec((1, n_past, A_WIDTH), lambda b: (b, 0, 0)),
            pl.BlockSpec((HEADS, t, n_past), lambda b: (0, 0, 0)),
            pl.BlockSpec((HEADS, t, t), lambda b: (0, 0, 0)),
        ],
        out_specs=pl.BlockSpec((t, A_WIDTH), lambda b: (b, 0)),
        out_shape=jax.ShapeDtypeStruct((batch * t, A_WIDTH), BF16),
        compiler_params=_params("arbitrary"),
    )(u, u, u, cache_k, cache_v, bias_c, bias_n)


def _sample_bias_tables(rel_bias, t, n_past):
    def by_dist(lo, hi):
        dist = jnp.arange(lo, hi + 1, dtype=jnp.int32)
        return rel_bias[:, jnp.clip(dist, -REL_CLIP, REL_CLIP) + REL_CLIP].astype(F32) * LOG2E

    cache = by_dist(-n_past - t + 1, -1)
    new = by_dist(-t + 1, t - 1)
    return _hankel(cache, t, n_past)[:, ::-1], _hankel(new, t, t)[:, ::-1]


def _mix_stages(x_ref, ya_ref, ab_ref, gc_ref, gb_ref, hc_ref, wb_ref, bb_ref, lg_ref, lb_ref, wc_ref, go_ref,
                wo_ref, cb, cc, sh):
    tm = x_ref.shape[0]
    ab = ab_ref[...].astype(F32)
    glu = ab[:, :B_WIDTH] * _sigmoid(ab[:, B_WIDTH:])
    pre = gc_ref[...].astype(F32) * hc_ref[...].astype(F32)
    cb[HIST_B:HIST_B + tm, :] = glu
    cc[HIST_C:HIST_C + tm, :] = pre
    yield _token(glu) + _token(pre)
    span = tm + HIST_B - SUBLANES
    moved = 0.0
    for r in range(1, SUBLANES):
        shifted = cb[pl.ds(r, span), :]
        sh[r - 1, 0:span, :] = shifted
        moved = moved + _token(shifted)
        if r in MIX_SHIFT_BREAKS:
            yield moved
            moved = 0.0
    z = jnp.zeros((tm, B_WIDTH), F32) + bb_ref[...]
    for j in range(B_KERNEL):
        m = j + HIST_B - (B_KERNEL - 1)
        a = m // SUBLANES * SUBLANES
        src = cb[a:a + tm, :] if m == a else sh[m - a - 1, a:a + tm, :]
        z = z + wb_ref[j:j + 1, :] * src
        if j in MIX_TAP_BREAKS:
            yield _token(z)
    mu = jnp.mean(z, axis=-1, keepdims=True)
    zc = z - mu
    var = jnp.mean(zc * zc, axis=-1, keepdims=True)
    yb = zc * lax.rsqrt(var + EPS) * lg_ref[...] + lb_ref[...]
    yb = yb * _sigmoid(yb)

    conv = jnp.zeros((tm, C_WIDTH), F32)
    for j in range(C_KERNEL):
        conv = conv + wc_ref[j:j + 1, :] * cc[pl.ds(j + HIST_C - (C_KERNEL - 1), tm), :]
    yc = gb_ref[...].astype(F32) * conv
    yield _token(yb) + _token(yc)

    go = go_ref[...]
    na = _rms(ya_ref[...].astype(F32), go[:, :A_WIDTH]).astype(BF16)
    nb = _rms(yb, go[:, A_WIDTH:A_WIDTH + B_WIDTH]).astype(BF16)
    nc = _rms(yc, go[:, A_WIDTH + B_WIDTH:]).astype(BF16)
    return (x_ref[...]
            + jnp.dot(na, wo_ref[0:A_WIDTH, :], preferred_element_type=F32)
            + jnp.dot(nb, wo_ref[A_WIDTH:A_WIDTH + B_WIDTH, :], preferred_element_type=F32)
            + jnp.dot(nc, wo_ref[A_WIDTH + B_WIDTH:, :], preferred_element_type=F32))


def _token(v):
    t = jnp.sum(v, axis=0, keepdims=True)
    return sum(t[:, c:c + LANES] for c in range(0, v.shape[1], LANES))


def _finish(stages):
    while True:
        try:
            next(stages)
        except StopIteration as done:
            return done.value


def _mix_carry(tm, cb, cc, nb_ref, nc_ref):
    tail_b = cb[tm:tm + HIST_B, :]
    tail_c = cc[tm:tm + HIST_C, :]
    nb_ref[0] = tail_b
    nc_ref[0] = tail_c
    cb[0:HIST_B, :] = tail_b
    cc[0:HIST_C, :] = tail_c


def _mix_kernel(x_ref, ya_ref, ab_ref, gc_ref, gb_ref, hc_ref, hb0_ref, hc0_ref,
                wb_ref, bb_ref, lg_ref, lb_ref, wc_ref, go_ref, wo_ref,
                h_ref, nb_ref, nc_ref, cb, cc, sh):
    @pl.when(pl.program_id(1) == 0)
    def _():
        cb[0:HIST_B, :] = hb0_ref[0]
        cc[0:HIST_C, :] = hc0_ref[0]

    h_ref[...] = _finish(_mix_stages(x_ref, ya_ref, ab_ref, gc_ref, gb_ref, hc_ref, wb_ref, bb_ref, lg_ref,
                                     lb_ref, wc_ref, go_ref, wo_ref, cb, cc, sh))
    _mix_carry(x_ref.shape[0], cb, cc, nb_ref, nc_ref)


def _mix(x, ya, u, hist_b, hist_c, wb, bb, lg, lb, wc, go, wo, batch, t, tm):
    n, d = x.shape
    nt = t // tm
    row = lambda b, i: b * nt + i
    ab_blk = 3 * A_WIDTH // (2 * B_WIDTH)
    c_blk = (3 * A_WIDTH + 2 * B_WIDTH) // C_WIDTH
    const = lambda b, i: (0, 0)
    return pl.pallas_call(
        _mix_kernel,
        grid=(batch, nt),
        in_specs=[
            pl.BlockSpec((tm, d), lambda b, i: (row(b, i), 0)),
            pl.BlockSpec((tm, A_WIDTH), lambda b, i: (row(b, i), 0)),
            pl.BlockSpec((tm, 2 * B_WIDTH), lambda b, i: (row(b, i), ab_blk)),
            pl.BlockSpec((tm, C_WIDTH), lambda b, i: (row(b, i), c_blk)),
            pl.BlockSpec((tm, C_WIDTH), lambda b, i: (row(b, i), c_blk + 1)),
            pl.BlockSpec((tm, C_WIDTH), lambda b, i: (row(b, i), c_blk + 2)),
            pl.BlockSpec((1, HIST_B, B_WIDTH), lambda b, i: (b, 0, 0)),
            pl.BlockSpec((1, HIST_C, C_WIDTH), lambda b, i: (b, 0, 0)),
            pl.BlockSpec(wb.shape, const), pl.BlockSpec(bb.shape, const),
            pl.BlockSpec(lg.shape, const), pl.BlockSpec(lb.shape, const),
            pl.BlockSpec(wc.shape, const), pl.BlockSpec(go.shape, const),
            pl.BlockSpec(wo.shape, const),
        ],
        out_specs=[pl.BlockSpec((tm, d), lambda b, i: (row(b, i), 0)),
                   pl.BlockSpec((1, HIST_B, B_WIDTH), lambda b, i: (b, 0, 0)),
                   pl.BlockSpec((1, HIST_C, C_WIDTH), lambda b, i: (b, 0, 0))],
        out_shape=[jax.ShapeDtypeStruct((n, d), F32),
                   jax.ShapeDtypeStruct((batch, HIST_B, B_WIDTH), F32),
                   jax.ShapeDtypeStruct((batch, HIST_C, C_WIDTH), F32)],
        scratch_shapes=[pltpu.VMEM((HIST_B + tm, B_WIDTH), F32),
                        pltpu.VMEM((HIST_C + tm, C_WIDTH), F32),
                        pltpu.VMEM((SUBLANES - 1, HIST_B + tm - SUBLANES, B_WIDTH), F32)],
        compiler_params=_params("arbitrary", "arbitrary"),
    )(x, ya, u, u, u, u, hist_b, hist_c, wb, bb, lg, lb, wc, go, wo)


def _ple_epilogue(h, p, gp, wpg, wpp, gf):
    gate = _sigmoid(jnp.dot(_rms(h, gp).astype(BF16), wpg, preferred_element_type=F32))
    h = h + gate * jnp.dot(p.astype(BF16), wpp, preferred_element_type=F32)
    return h if gf is None else _rms(h, gf)


def _swiglu_chunks(x, wg_ref, wu_ref, wd_ref, acc, between=None):
    f = wg_ref.shape[-1]
    for c in range(f // FF_CHUNK):
        cols = slice(c * FF_CHUNK, (c + 1) * FF_CHUNK)
        a = jnp.dot(x, wg_ref[0, :, cols], preferred_element_type=F32)
        b = jnp.dot(x, wu_ref[0, :, cols], preferred_element_type=F32)
        mid = (a * _sigmoid(a) * b).astype(BF16)
        part = jnp.dot(mid, wd_ref[0, cols, :], preferred_element_type=F32)
        if c == 0:
            acc[...] = part
        else:
            acc[...] += part
        if between is not None:
            done = next(between)
            acc[0:SUBLANES, 0:LANES] += jnp.where(done != done, done, 0.0)


def _ffn_kernel(final, h_ref, g_ref, wg_ref, wu_ref, wd_ref, p_ref, gp_ref, wpg_ref, wpp_ref, gf_ref,
                o_ref, acc):
    xn = _rms(h_ref[...], g_ref[...]).astype(BF16)
    _swiglu_chunks(xn, wg_ref, wu_ref, wd_ref, acc)
    h = h_ref[...] + acc[...]
    o_ref[...] = _ple_epilogue(h, p_ref[0], gp_ref[...], wpg_ref[...], wpp_ref[...],
                               gf_ref[...] if final else None)


def _resident(shape):
    return pl.BlockSpec(shape, lambda *_: (0,) * len(shape), pipeline_mode=pl.Buffered(1))


def _ffn(h, g, wg, wu, wd, j, p, layer, gp, wpg, wpp, gf, final, tm):
    n, d = h.shape
    f = wg.shape[-1]
    const = lambda i: (0, 0)
    return pl.pallas_call(
        functools.partial(_ffn_kernel, final),
        grid=(n // tm,),
        in_specs=[
            pl.BlockSpec((tm, d), lambda i: (i, 0)),
            pl.BlockSpec((1, d), const),
            pl.BlockSpec((1, d, f), lambda i: (j, 0, 0), pipeline_mode=pl.Buffered(1)),
            pl.BlockSpec((1, d, f), lambda i: (j, 0, 0), pipeline_mode=pl.Buffered(1)),
            pl.BlockSpec((1, f, d), lambda i: (j, 0, 0), pipeline_mode=pl.Buffered(1)),
            pl.BlockSpec((1, tm, p.shape[-1]), lambda i: (layer, i, 0)),
            pl.BlockSpec((1, d), const),
            _resident(wpg.shape),
            _resident(wpp.shape),
            pl.BlockSpec((1, d), const),
        ],
        out_specs=pl.BlockSpec((tm, d), lambda i: (i, 0)),
        out_shape=jax.ShapeDtypeStruct((n, d), F32),
        scratch_shapes=[pltpu.VMEM((tm, d), F32)],
        compiler_params=_params("arbitrary"),
    )(h, g, wg, wu, wd, p, gp, wpg, wpp, gf)


def _mix_ffn_kernel(final, nt,
                    x_ref, ya_ref, ab_ref, gc_ref, gb_ref, hc_ref, hb0_ref, hc0_ref,
                    wb_ref, bb_ref, lg_ref, lb_ref, wc_ref, go_ref, wo_ref,
                    g_ref, wg_ref, wu_ref, wd_ref, p_ref, gp_ref, wpg_ref, wpp_ref, gf_ref,
                    o_ref, nb_ref, nc_ref, cb, cc, sh, h1a, h1b, acc):
    g = pl.program_id(0)
    tiles = pl.num_programs(0) - 1
    tm = x_ref.shape[0]

    @pl.when(g == 0)
    def _():
        h1b[...] = jnp.zeros_like(h1b)

    @pl.when(jnp.minimum(g, tiles - 1) % nt == 0)
    def _():
        cb[0:HIST_B, :] = hb0_ref[0]
        cc[0:HIST_C, :] = hc0_ref[0]

    def step(h_new, h_old):
        mixers = _mix_stages(x_ref, ya_ref, ab_ref, gc_ref, gb_ref, hc_ref, wb_ref, bb_ref, lg_ref, lb_ref,
                             wc_ref, go_ref, wo_ref, cb, cc, sh)
        _swiglu_chunks(_rms(h_old[...], g_ref[...]).astype(BF16), wg_ref, wu_ref, wd_ref, acc, mixers)
        h_new[...] = _finish(mixers)
        o_ref[...] = _ple_epilogue(h_old[...] + acc[...], p_ref[0], gp_ref[...], wpg_ref[...], wpp_ref[...],
                                   gf_ref[...] if final else None)

    @pl.when(g % 2 == 0)
    def _():
        step(h1a, h1b)

    @pl.when(g % 2 == 1)
    def _():
        step(h1b, h1a)

    @pl.when(g < tiles)
    def _():
        _mix_carry(tm, cb, cc, nb_ref, nc_ref)


def _mix_ffn(x, ya, u, hist_b, hist_c, wb, bb, lg, lb, wc, go, wo,
             g, wg, wu, wd, j, p, layer, gp, wpg, wpp, gf, final, batch, t, tm):
    n, d = x.shape
    f = wg.shape[-1]
    nt = t // tm
    tiles = n // tm
    cur = lambda i: jnp.minimum(i, tiles - 1)
    old = lambda i: jnp.maximum(i - 1, 0)
    ab_blk = 3 * A_WIDTH // (2 * B_WIDTH)
    c_blk = (3 * A_WIDTH + 2 * B_WIDTH) // C_WIDTH
    const = lambda i: (0, 0)
    state = lambda rows, width: pl.BlockSpec((1, rows, width), lambda i: (cur(i) // nt, 0, 0))
    weights = lambda: pl.BlockSpec((1,) + wg.shape[1:], lambda i: (j, 0, 0), pipeline_mode=pl.Buffered(1))
    return pl.pallas_call(
        functools.partial(_mix_ffn_kernel, final, nt),
        grid=(tiles + 1,),
        in_specs=[
            pl.BlockSpec((tm, d), lambda i: (cur(i), 0)),
            pl.BlockSpec((tm, A_WIDTH), lambda i: (cur(i), 0)),
            pl.BlockSpec((tm, 2 * B_WIDTH), lambda i: (cur(i), ab_blk)),
            pl.BlockSpec((tm, C_WIDTH), lambda i: (cur(i), c_blk)),
            pl.BlockSpec((tm, C_WIDTH), lambda i: (cur(i), c_blk + 1)),
            pl.BlockSpec((tm, C_WIDTH), lambda i: (cur(i), c_blk + 2)),
            state(HIST_B, B_WIDTH), state(HIST_C, C_WIDTH),
            pl.BlockSpec(wb.shape, const), pl.BlockSpec(bb.shape, const),
            pl.BlockSpec(lg.shape, const), pl.BlockSpec(lb.shape, const),
            pl.BlockSpec(wc.shape, const), pl.BlockSpec(go.shape, const),
            _resident(wo.shape),
            pl.BlockSpec((1, d), const),
            weights(), weights(),
            pl.BlockSpec((1, f, d), lambda i: (j, 0, 0), pipeline_mode=pl.Buffered(1)),
            pl.BlockSpec((1, tm, p.shape[-1]), lambda i: (layer, old(i), 0)),
            pl.BlockSpec((1, d), const),
            _resident(wpg.shape),
            _resident(wpp.shape),
            pl.BlockSpec((1, d), const),
        ],
        out_specs=[pl.BlockSpec((tm, d), lambda i: (old(i), 0)),
                   state(HIST_B, B_WIDTH), state(HIST_C, C_WIDTH)],
        out_shape=[jax.ShapeDtypeStruct((n, d), F32),
                   jax.ShapeDtypeStruct((batch, HIST_B, B_WIDTH), F32),
                   jax.ShapeDtypeStruct((batch, HIST_C, C_WIDTH), F32)],
        scratch_shapes=[pltpu.VMEM((HIST_B + tm, B_WIDTH), F32),
                        pltpu.VMEM((HIST_C + tm, C_WIDTH), F32),
                        pltpu.VMEM((SUBLANES - 1, HIST_B + tm - SUBLANES, B_WIDTH), F32),
                        pltpu.VMEM((tm, d), F32),
                        pltpu.VMEM((tm, d), F32),
                        pltpu.VMEM((tm, d), F32)],
        compiler_params=_params("arbitrary", vmem=FUSED_VMEM_LIMIT),
    )(x, ya, u, u, u, u, hist_b, hist_c, wb, bb, lg, lb, wc, go, wo,
      g, wg, wu, wd, p, gp, wpg, wpp, gf)


def _router_kernel(h_ref, g_ref, wr_ref, br_ref, x3_ref, meta_ref, cnt_ref, carry):
    tm = h_ref.shape[0]

    @pl.when(pl.program_id(0) == 0)
    def _():
        carry[...] = jnp.zeros_like(carry)

    xn = _rms(h_ref[...], g_ref[...])
    for s in range(ROW_SPLIT):
        x3_ref[_chunk(s, tm), :] = xn[:, s * LANES:(s + 1) * LANES]
    logits = lax.dot_general(wr_ref[...], xn, (((1,), (1,)), ((), ())),
                             precision=lax.Precision.HIGHEST,
                             preferred_element_type=F32) + br_ref[...]
    eidx = lax.broadcasted_iota(jnp.int32, (N_EXPERTS, tm), 0)
    m1 = jnp.max(logits, axis=0, keepdims=True)
    i1 = jnp.min(jnp.where(logits == m1, eidx, N_EXPERTS), axis=0, keepdims=True)
    rest = jnp.where(eidx == i1, -jnp.inf, logits)
    m2 = jnp.max(rest, axis=0, keepdims=True)
    i2 = jnp.min(jnp.where(rest == m2, eidx, N_EXPERTS), axis=0, keepdims=True)
    e2 = jnp.exp(m2 - m1)
    g1 = 1.0 / (1.0 + e2)
    g2 = e2 / (1.0 + e2)
    hit1 = eidx == i1
    hit2 = eidx == i2
    chosen = jnp.where(hit1 | hit2, 1.0, 0.0)
    src = lax.broadcasted_iota(jnp.int32, (tm, tm), 0)
    dst = lax.broadcasted_iota(jnp.int32, (tm, tm), 1)
    before = jnp.where(src < dst, 1.0, 0.0).astype(BF16)
    rank = jnp.dot(chosen.astype(BF16), before, preferred_element_type=F32) + carry[:, 0:1]
    r1 = jnp.sum(jnp.where(hit1, rank, 0.0), axis=0, keepdims=True)
    r2 = jnp.sum(jnp.where(hit2, rank, 0.0), axis=0, keepdims=True)
    zero = jnp.zeros_like(g1)
    meta_ref[...] = jnp.concatenate(
        [i1.astype(F32), i2.astype(F32), g1, g2, r1, r2, zero, zero], axis=0)
    carry[...] = carry[...] + jnp.sum(chosen, axis=1, keepdims=True)
    cnt_ref[...] = carry[...]


def _router(h, g, wr_t, br, tm):
    n, d = h.shape
    return pl.pallas_call(
        _router_kernel,
        grid=(n // tm,),
        in_specs=[pl.BlockSpec((tm, d), lambda i: (i, 0)),
                  pl.BlockSpec((1, d), lambda i: (0, 0)),
                  pl.BlockSpec((N_EXPERTS, d), lambda i: (0, 0)),
                  pl.BlockSpec((N_EXPERTS, 1), lambda i: (0, 0))],
        out_specs=[pl.BlockSpec((tm * ROW_SPLIT, LANES), lambda i: (i, 0)),
                   pl.BlockSpec((8, tm), lambda i: (0, i)),
                   pl.BlockSpec((N_EXPERTS, LANES), lambda i: (0, 0))],
        out_shape=[jax.ShapeDtypeStruct((n * ROW_SPLIT, LANES), F32),
                   jax.ShapeDtypeStruct((8, n), F32),
                   jax.ShapeDtypeStruct((N_EXPERTS, LANES), F32)],
        scratch_shapes=[pltpu.VMEM((N_EXPERTS, LANES), F32)],
        compiler_params=_params("arbitrary"),
    )(h, g, wr_t, br)


def _dispatch_kernel(d1_ref, d2_ref, x_ref, o_hbm, sem):
    tm = x_ref.shape[0] // ROW_SPLIT

    def issue(g, carry):
        for k in range(DMA_UNROLL):
            t = g * DMA_UNROLL + k
            pltpu.make_async_copy(x_ref.at[_row_tile(t)], o_hbm.at[_row_tile(d1_ref[0, 0, t])], sem).start()
            pltpu.make_async_copy(x_ref.at[_row_tile(t)], o_hbm.at[_row_tile(d2_ref[0, 0, t])], sem).start()
        return carry

    lax.fori_loop(0, tm // DMA_UNROLL, issue, 0)
    for _ in range(2):
        pltpu.make_async_copy(x_ref, o_hbm.at[pl.ds(0, tm * ROW_SPLIT)], sem).wait()


def _dispatch(x3, d1, d2, rows_out, tm):
    n = x3.shape[0] // ROW_SPLIT
    steps = d1.shape[0] // tm
    last = n // tm - 1
    idx_spec = pl.BlockSpec((1, 1, tm), lambda i: (i, 0, 0), memory_space=pltpu.SMEM)
    return pl.pallas_call(
        _dispatch_kernel,
        grid=(steps,),
        in_specs=[idx_spec, idx_spec,
                  pl.BlockSpec((tm * ROW_SPLIT, LANES), lambda i: (jnp.minimum(i, last), 0))],
        out_specs=pl.BlockSpec(memory_space=pl.ANY),
        out_shape=jax.ShapeDtypeStruct((rows_out * ROW_SPLIT, LANES), x3.dtype),
        scratch_shapes=[pltpu.SemaphoreType.DMA(())],
        compiler_params=_params("arbitrary"),
    )(d1.reshape(steps, 1, tm), d2.reshape(steps, 1, tm), x3)


def _expert_kernel(texp_ref, nact_ref, x_ref, wg_ref, wu_ref, wd_ref, y_ref, x2, acc):
    del texp_ref

    @pl.when(pl.program_id(0) < nact_ref[0])
    def _():
        for s in range(ROW_SPLIT):
            x2[:, s * LANES:(s + 1) * LANES] = x_ref[_chunk(s, EXPERT_TILE), :].astype(BF16)
        _swiglu_chunks(x2[...], wg_ref, wu_ref, wd_ref, acc)
        for s in range(ROW_SPLIT):
            y_ref[_chunk(s, EXPERT_TILE), :] = acc[:, s * LANES:(s + 1) * LANES]

    @pl.when(pl.program_id(0) >= nact_ref[0])
    def _():
        y_ref[...] = jnp.zeros_like(y_ref)


def _experts(xs, tile_exp, n_active, wg, wu, wd):
    ntile = tile_exp.shape[0]
    d, f = wg.shape[1], wg.shape[2]
    grid_spec = pltpu.PrefetchScalarGridSpec(
        num_scalar_prefetch=2,
        grid=(ntile,),
        in_specs=[
            pl.BlockSpec((EXPERT_TILE * ROW_SPLIT, LANES), lambda t, te, na: (t, 0)),
            pl.BlockSpec((1, d, f), lambda t, te, na: (te[t], 0, 0)),
            pl.BlockSpec((1, d, f), lambda t, te, na: (te[t], 0, 0)),
            pl.BlockSpec((1, f, d), lambda t, te, na: (te[t], 0, 0)),
        ],
        out_specs=pl.BlockSpec((EXPERT_TILE * ROW_SPLIT, LANES), lambda t, te, na: (t, 0)),
        scratch_shapes=[pltpu.VMEM((EXPERT_TILE, d), BF16), pltpu.VMEM((EXPERT_TILE, d), F32)],
    )
    return pl.pallas_call(
        _expert_kernel,
        grid_spec=grid_spec,
        out_shape=jax.ShapeDtypeStruct(xs.shape, F32),
        compiler_params=_params("arbitrary"),
    )(tile_exp, n_active, xs, wg, wu, wd)


def _combine_kernel(final, d1a, d2a, d1b, d2b, d1c, d2c, ye_hbm, h_ref, gate_ref, p_ref, gp_ref, wpg_ref, wpp_ref,
                    gf_ref, o_ref, ya1, ya2, yb1, yb2, moe, sems):
    j = pl.program_id(0)
    tm = h_ref.shape[0] // 2

    def gather(d1_ref, d2_ref, y1, y2, sem):
        for t in range(tm):
            pltpu.make_async_copy(ye_hbm.at[_row_tile(d1_ref[0, 0, t])], y1.at[_row_tile(t)], sem).start()
            pltpu.make_async_copy(ye_hbm.at[_row_tile(d2_ref[0, 0, t])], y2.at[_row_tile(t)], sem).start()

    def wait(y1, y2, sem):
        pltpu.make_async_copy(ye_hbm.at[pl.ds(0, tm * ROW_SPLIT)], y1, sem).wait()
        pltpu.make_async_copy(ye_hbm.at[pl.ds(0, tm * ROW_SPLIT)], y2, sem).wait()

    def combine(y1, y2, rows):
        g1 = gate_ref[rows, 0:1]
        g2 = gate_ref[rows, 1:2]
        for s in range(ROW_SPLIT):
            moe[:, s * LANES:(s + 1) * LANES] = g1 * y1[_chunk(s, tm), :] + g2 * y2[_chunk(s, tm), :]
        h = h_ref[rows, :] + moe[...]
        o_ref[rows, :] = _ple_epilogue(h, p_ref[0, rows, :], gp_ref[...], wpg_ref[...], wpp_ref[...],
                                       gf_ref[...] if final else None)

    @pl.when(j == 0)
    def _():
        gather(d1a, d2a, ya1, ya2, sems.at[0])

    wait(ya1, ya2, sems.at[0])
    gather(d1b, d2b, yb1, yb2, sems.at[1])
    combine(ya1, ya2, slice(0, tm))
    wait(yb1, yb2, sems.at[1])
    gather(d1c, d2c, ya1, ya2, sems.at[0])
    combine(yb1, yb2, slice(tm, 2 * tm))

    @pl.when(j == pl.num_programs(0) - 1)
    def _():
        wait(ya1, ya2, sems.at[0])


def _combine(ye, d1, d2, h, gates, p, layer, gp, wpg, wpp, gf, final, tm):
    n, d = h.shape
    steps = n // (2 * tm)
    tiles = n // tm
    idx_spec = lambda f: pl.BlockSpec((1, 1, tm), lambda i: (f(i), 0, 0), memory_space=pltpu.SMEM)
    first = idx_spec(lambda i: 2 * i)
    second = idx_spec(lambda i: 2 * i + 1)
    ahead = idx_spec(lambda i: jnp.minimum(2 * i + 2, tiles - 1))
    const = lambda i: (0, 0)
    d1 = d1.reshape(tiles, 1, tm)
    d2 = d2.reshape(tiles, 1, tm)
    row_buf = pltpu.VMEM((tm * ROW_SPLIT, LANES), F32)
    return pl.pallas_call(
        functools.partial(_combine_kernel, final),
        grid=(steps,),
        in_specs=[first, first, second, second, ahead, ahead,
                  pl.BlockSpec(memory_space=pl.ANY),
                  pl.BlockSpec((2 * tm, d), lambda i: (i, 0)),
                  pl.BlockSpec((2 * tm, 2), lambda i: (i, 0)),
                  pl.BlockSpec((1, 2 * tm, p.shape[-1]), lambda i: (layer, i, 0)),
                  pl.BlockSpec((1, d), const),
                  _resident(wpg.shape),
                  _resident(wpp.shape),
                  pl.BlockSpec((1, d), const)],
        out_specs=pl.BlockSpec((2 * tm, d), lambda i: (i, 0)),
        out_shape=jax.ShapeDtypeStruct((n, d), F32),
        scratch_shapes=[row_buf, row_buf, row_buf, row_buf,
                        pltpu.VMEM((tm, d), F32),
                        pltpu.SemaphoreType.DMA((2,))],
        compiler_params=_params("arbitrary"),
    )(d1, d2, d1, d2, d1, d2, ye, h, gates, p, gp, wpg, wpp, gf)


def _moe(h, g, wr_t, br, wg, wu, wd, p, layer, gp, wpg, wpp, gf, final, tm):
    n = h.shape[0]
    x3, meta, cnt = _router(h, g, wr_t, br, tm)
    counts = cnt[:, 0].astype(jnp.int32)
    tiles = (counts + EXPERT_TILE - 1) // EXPERT_TILE
    padded = tiles * EXPERT_TILE
    ends = jnp.cumsum(padded)
    offs = ends - padded
    expert = jnp.arange(N_EXPERTS, dtype=jnp.int32)[:, None]
    start_of = lambda idx: jnp.sum(jnp.where(idx[None, :] == expert, offs[:, None], 0), axis=0)
    d1 = start_of(meta[0].astype(jnp.int32)) + meta[4].astype(jnp.int32)
    d2 = start_of(meta[1].astype(jnp.int32)) + meta[5].astype(jnp.int32)
    gates = jnp.stack([meta[2], meta[3]], axis=1)
    ntile = 2 * n // EXPERT_TILE + N_EXPERTS
    r = jnp.arange(EXPERT_TILE, dtype=jnp.int32)[None, :]
    in_group = counts[:, None] + r < padded[:, None]
    spare = ends[-1] + (jnp.cumsum(jnp.where(in_group, 0, 1).reshape(-1)) - 1).reshape(in_group.shape)
    fill = jnp.where(in_group, offs[:, None] + counts[:, None] + r, spare).reshape(-1).astype(jnp.int32)
    half = fill.shape[0] // 2
    xs = _dispatch(x3, jnp.concatenate([d1, fill[:half]]), jnp.concatenate([d2, fill[half:]]),
                   ntile * EXPERT_TILE, tm)
    n_active = jnp.sum(tiles).astype(jnp.int32)
    tile_start = jnp.minimum(jnp.arange(ntile, dtype=jnp.int32), n_active - 1) * EXPERT_TILE
    tile_exp = jnp.minimum(jnp.sum(jnp.where(tile_start[:, None] >= ends[None, :], 1, 0), axis=1),
                           N_EXPERTS - 1).astype(jnp.int32)
    ye = _experts(xs, tile_exp, n_active.reshape(1), wg, wu, wd)
    return _combine(ye, d1, d2, h, gates, p, layer, gp, wpg, wpp, gf, final, min(tm, n // 2))


def _trunk(x, p, past, prm, tok_tile, mix_tile):
    batch, t, d = x.shape
    n = batch * t
    depth = prm["w_in"].shape[0]
    h = x.reshape(n, d)
    new_k, new_v, new_b, new_c = [], [], [], []
    row = lambda a: a.reshape(1, -1)
    p = p.reshape(depth, n, -1)
    keep = min(BAND, t) if past is None else t
    for i in range(depth):
        u, kv = _inproj(h, row(prm["norm_mix"][i]), prm["w_in"][i], tok_tile, t, keep)
        new_k.append(kv[:, :A_WIDTH].reshape(batch, keep, HEADS, HEAD_DIM))
        new_v.append(kv[:, A_WIDTH:].reshape(batch, keep, HEADS, HEAD_DIM))
        if past is None:
            ya = _prompt_attention(u, _prompt_bias_table(prm["rel_bias"][i]), batch, t)
            hist_b = jnp.zeros((batch, HIST_B, B_WIDTH), F32)
            hist_c = jnp.zeros((batch, HIST_C, C_WIDTH), F32)
        else:
            ck, cv, sb, sc = past
            n_past = ck.shape[2]
            bias_c, bias_n = _sample_bias_tables(prm["rel_bias"][i], t, n_past)
            ya = _sample_attention(u, ck[i].reshape(batch, n_past, A_WIDTH),
                                   cv[i].reshape(batch, n_past, A_WIDTH), bias_c, bias_n, batch, t)
            hist_b = jnp.pad(sb[i], ((0, 0), (HIST_B - (B_KERNEL - 1), 0), (0, 0)))
            hist_c = jnp.pad(sc[i], ((0, 0), (HIST_C - (C_KERNEL - 1), 0), (0, 0)))
        wb = jnp.pad(prm["w_dw_b"][i], ((0, HIST_B - B_KERNEL), (0, 0)))
        wc = jnp.pad(prm["w_dw_c"][i], ((0, HIST_C - C_KERNEL), (0, 0)))
        mix_args = (h, ya, u, hist_b, hist_c, wb, row(prm["b_dw_b"][i]), row(prm["ln_g_b"][i]),
                    row(prm["ln_b_b"][i]), wc, row(prm["g_out"][i]), prm["w_out"][i])
        final = i == depth - 1
        tail = (p, i, row(prm["norm_ple"][i]), prm["w_ple_gate"][i], prm["w_ple_proj"][i],
                row(prm["norm_final"]), final)
        j = i // 2
        ffn_args = (row(prm["norm_ffn"][i]), prm["w_ff_gate"], prm["w_ff_up"], prm["w_ff_down"], j)
        if i % 2 == 0 and tok_tile == mix_tile:
            h, nb, nc = _mix_ffn(*mix_args, *ffn_args, *tail, batch, t, mix_tile)
        else:
            h, nb, nc = _mix(*mix_args, batch, t, mix_tile)
            if i % 2 == 0:
                h = _ffn(h, *ffn_args, *tail, tok_tile)
            else:
                h = _moe(h, row(prm["norm_ffn"][i]), prm["w_router"][j].T, prm["b_router"][j].reshape(-1, 1),
                         prm["w_ex_gate"][j], prm["w_ex_up"][j], prm["w_ex_down"][j], *tail, tok_tile)
        new_b.append(nb[:, HIST_B - (B_KERNEL - 1):])
        new_c.append(nc[:, HIST_C - (C_KERNEL - 1):])
    return (h.reshape(batch, t, d), jnp.stack(new_k), jnp.stack(new_v), jnp.stack(new_b), jnp.stack(new_c))


def kernel(x_prompt, x_sample, p_prompt, p_sample, cache_attn_k, cache_attn_v, state_conv_b, state_conv_c,
           w_in, rel_bias, w_dw_b, b_dw_b, ln_g_b, ln_b_b, w_dw_c, g_out, w_out, norm_mix, norm_ffn,
           w_ff_gate, w_ff_up, w_ff_down, w_router, b_router, w_ex_gate, w_ex_up, w_ex_down,
           norm_ple, w_ple_gate, w_ple_proj, norm_final):
    prm = dict(
        w_in=w_in.astype(BF16), rel_bias=rel_bias, w_dw_b=w_dw_b, b_dw_b=b_dw_b, ln_g_b=ln_g_b,
        ln_b_b=ln_b_b, w_dw_c=w_dw_c, g_out=g_out, w_out=w_out.astype(BF16), norm_mix=norm_mix,
        norm_ffn=norm_ffn, w_ff_gate=w_ff_gate.astype(BF16), w_ff_up=w_ff_up.astype(BF16),
        w_ff_down=w_ff_down.astype(BF16), w_router=w_router, b_router=b_router,
        w_ex_gate=w_ex_gate.astype(BF16), w_ex_up=w_ex_up.astype(BF16), w_ex_down=w_ex_down.astype(BF16),
        norm_ple=norm_ple, w_ple_gate=w_ple_gate.astype(BF16), w_ple_proj=w_ple_proj.astype(BF16),
        norm_final=norm_final)
    dec_t = x_sample.shape[1]
    y_p, k_p, v_p, b_p, c_p = _trunk(x_prompt, p_prompt, None, prm, 512, 512)
    y_s, k_s, v_s, b_s, c_s = _trunk(x_sample, p_sample,
                                     (cache_attn_k, cache_attn_v, state_conv_b, state_conv_c),
                                     prm, x_sample.shape[0] * dec_t, dec_t)
    return (y_p, y_s, k_p, v_p, b_p, c_p, k_s, v_s, b_s, c_s)
```

```python
import functools

import jax
import jax.numpy as jnp
from jax import lax
from jax.experimental import pallas as pl
from jax.experimental.pallas import tpu as pltpu

F32 = jnp.float32
BF16 = jnp.bfloat16

CHUNK = 64
BAND = 512
HEADS = 8
HEAD_DIM = 64
A_WIDTH = HEADS * HEAD_DIM
B_WIDTH = 256
C_WIDTH = 256
B_KERNEL = 31
C_KERNEL = 3
REL_CLIP = 128
N_EXPERTS = 8
EPS = 1e-6
NEG = -1e30
LOG2E = 1.4426950408889634

LANES = 128
SUBLANES = 8
ROW_SPLIT = 8
QTILE = 512
PAIR_ROWS = 2 * CHUNK
WIN = BAND + PAIR_ROWS
HIST_B = 32
HIST_C = 8
EXPERT_TILE = 512
FF_CHUNK = 512
MIX_SHIFT_BREAKS = (3, 7)
MIX_TAP_BREAKS = (7, 15, 23, 30)
VMEM_LIMIT = 56 * 1024 * 1024
FUSED_VMEM_LIMIT = 60 * 1024 * 1024


def _params(*sem, vmem=VMEM_LIMIT):
    return pltpu.CompilerParams(dimension_semantics=sem, vmem_limit_bytes=vmem)


def _rms(x, g):
    return x * lax.rsqrt(jnp.mean(x * x, axis=-1, keepdims=True) + EPS) * g


def _sigmoid(x):
    return 1.0 / (1.0 + jnp.exp(-x))


def _chunk(s, n):
    return pl.ds(s, n, stride=ROW_SPLIT)


def _row_tile(r):
    return pl.ds(pl.multiple_of(r * ROW_SPLIT, ROW_SPLIT), ROW_SPLIT)


def _inproj_kernel(x_ref, g_ref, w_ref, u_ref, kv_ref):
    hn = _rms(x_ref[...], g_ref[...]).astype(BF16)
    q = jnp.dot(hn, w_ref[:, :A_WIDTH], preferred_element_type=F32)
    u_ref[:, :A_WIDTH] = (q * (HEAD_DIM ** -0.5 * LOG2E)).astype(BF16)
    kv = jnp.dot(hn, w_ref[:, A_WIDTH:3 * A_WIDTH], preferred_element_type=F32)
    kv_ref[...] = kv
    u_ref[:, A_WIDTH:3 * A_WIDTH] = kv.astype(BF16)
    rest = jnp.dot(hn, w_ref[:, 3 * A_WIDTH:], preferred_element_type=F32)
    u_ref[:, 3 * A_WIDTH:] = rest.astype(BF16)


def _inproj(x, g, w, tm, t, keep):
    n, d = x.shape
    wi = w.shape[1]
    if keep == t:
        kv_index = lambda i: (i, 0)
    else:
        nt, ktiles = t // tm, keep // tm
        kv_index = lambda i: ((i // nt) * ktiles + jnp.maximum(i % nt - (nt - ktiles), 0), 0)
    return pl.pallas_call(
        _inproj_kernel,
        grid=(n // tm,),
        in_specs=[pl.BlockSpec((tm, d), lambda i: (i, 0)),
                  pl.BlockSpec((1, d), lambda i: (0, 0)),
                  pl.BlockSpec((d, wi), lambda i: (0, 0))],
        out_specs=[pl.BlockSpec((tm, wi), lambda i: (i, 0)),
                   pl.BlockSpec((tm, 2 * A_WIDTH), kv_index)],
        out_shape=[jax.ShapeDtypeStruct((n, wi), BF16),
                   jax.ShapeDtypeStruct((n // t * keep, 2 * A_WIDTH), F32)],
        compiler_params=_params("arbitrary"),
    )(x, g, w)


def _attn_kernel(q_ref, kp_ref, kc_ref, vp_ref, vc_ref, bias_ref, o_ref, k_scr, vt_scr, s_scr):
    i = pl.program_id(1)
    k_scr[0:QTILE, :] = kp_ref[...]
    k_scr[QTILE:2 * QTILE, :] = kc_ref[...]
    vt_scr[:, 0:QTILE] = vp_ref[...].astype(F32).T.astype(BF16)
    vt_scr[:, QTILE:2 * QTILE] = vc_ref[...].astype(F32).T.astype(BF16)
    lane = lax.broadcasted_iota(jnp.int32, (PAIR_ROWS, LANES), 1)
    row = lax.broadcasted_iota(jnp.int32, (WIN, 2 * PAIR_ROWS), 0)
    nblk = QTILE // PAIR_ROWS
    npair = HEADS // 2

    def scores(p):
        cols = slice(p * LANES, (p + 1) * LANES)
        for c2 in range(nblk):
            r0 = c2 * PAIR_ROWS
            kwin = k_scr[r0:r0 + WIN, cols]
            q2 = q_ref[r0:r0 + PAIR_ROWS, cols]
            zero = jnp.zeros_like(q2)
            qq = jnp.concatenate([jnp.where(lane < HEAD_DIM, q2, zero),
                                  jnp.where(lane >= HEAD_DIM, q2, zero)], axis=0)
            s_scr[(p % 2) * nblk + c2] = lax.dot_general(
                kwin, qq, (((1,), (1,)), ((), ())), preferred_element_type=F32)

    def tile(first):
        scores(0)
        for p in range(npair):
            if p + 1 < npair:
                scores(p + 1)
            cols = slice(p * LANES, (p + 1) * LANES)
            for c2 in range(nblk):
                r0 = c2 * PAIR_ROWS
                vwin = vt_scr[cols, r0:r0 + WIN]
                s = s_scr[(p % 2) * nblk + c2] + bias_ref[p]
                if first:
                    s = jnp.where(row < BAND - r0, NEG, s)
                m = jnp.max(s, axis=0, keepdims=True)
                e = jnp.exp2(s - m)
                l = jnp.sum(e, axis=0, keepdims=True)
                ot = jnp.dot(vwin, e.astype(BF16), preferred_element_type=F32)
                ot = ot * (1.0 / l)
                o = jnp.concatenate([ot[:HEAD_DIM, :PAIR_ROWS], ot[HEAD_DIM:, PAIR_ROWS:]], axis=0).T
                o_ref[r0:r0 + PAIR_ROWS, cols] = o.astype(o_ref.dtype)

    @pl.when(i == 0)
    def _():
        tile(True)

    @pl.when(i > 0)
    def _():
        tile(False)


def _prompt_attention(u, bias_t, batch, seq):
    nt = seq // QTILE
    prev = lambda b, i: b * nt + jnp.maximum(i - 1, 0)
    return pl.pallas_call(
        _attn_kernel,
        grid=(batch, nt),
        in_specs=[
            pl.BlockSpec((QTILE, A_WIDTH), lambda b, i: (b * nt + i, 0)),
            pl.BlockSpec((QTILE, A_WIDTH), lambda b, i: (prev(b, i), 1)),
            pl.BlockSpec((QTILE, A_WIDTH), lambda b, i: (b * nt + i, 1)),
            pl.BlockSpec((QTILE, A_WIDTH), lambda b, i: (prev(b, i), 2)),
            pl.BlockSpec((QTILE, A_WIDTH), lambda b, i: (b * nt + i, 2)),
            pl.BlockSpec((HEADS // 2, WIN, 2 * PAIR_ROWS), lambda b, i: (0, 0, 0)),
        ],
        out_specs=pl.BlockSpec((QTILE, A_WIDTH), lambda b, i: (b * nt + i, 0)),
        out_shape=jax.ShapeDtypeStruct((batch * seq, A_WIDTH), BF16),
        scratch_shapes=[pltpu.VMEM((2 * QTILE, A_WIDTH), BF16),
                        pltpu.VMEM((A_WIDTH, 2 * QTILE), BF16),
                        pltpu.VMEM((2 * QTILE // PAIR_ROWS, WIN, 2 * PAIR_ROWS), F32)],
        compiler_params=_params("arbitrary", "arbitrary"),
    )(u, u, u, u, u, bias_t)


def _hankel(v, rows, cols):
    heads, length = v.shape
    flat = jnp.tile(v, (1, rows + 1))[:, :rows * (length + 1)]
    return flat.reshape(heads, rows, length + 1)[:, :, :cols]


def _prompt_bias_table(rel_bias):
    by_dist = jnp.concatenate(
        [jnp.broadcast_to(rel_bias[:, :1], (HEADS, BAND)), rel_bias[:, :2 * REL_CLIP]], axis=1).astype(F32)
    by_sum = _hankel(by_dist, PAIR_ROWS, WIN + 1)
    table = jnp.transpose(by_sum[:, ::-1, 1:], (0, 2, 1))
    j = jnp.arange(WIN, dtype=jnp.int32)[:, None]
    lo = (jnp.arange(PAIR_ROWS, dtype=jnp.int32)[None, :] // CHUNK) * CHUNK
    valid = (j >= lo) & (j < lo + BAND + CHUNK)
    table = jnp.where(valid[None], table * LOG2E, NEG)
    table = table.reshape(HEADS // 2, 2, WIN, PAIR_ROWS)
    return jnp.transpose(table, (0, 2, 1, 3)).reshape(HEADS // 2, WIN, 2 * PAIR_ROWS)


def _sample_attn_kernel(q_ref, kn_ref, vn_ref, ck_ref, cv_ref, bc_ref, bn_ref, o_ref):
    t = q_ref.shape[0]
    lane = lax.broadcasted_iota(jnp.int32, (t, LANES), 1)
    kc = ck_ref[0].astype(BF16)
    vc = cv_ref[0].astype(BF16)
    for p in range(HEADS // 2):
        cols = slice(p * LANES, (p + 1) * LANES)
        q2 = q_ref[:, cols]
        acc = None
        for hh in range(2):
            h = 2 * p + hh
            sel = (lane >= HEAD_DIM) == bool(hh)
            qm = jnp.where(sel, q2, jnp.zeros_like(q2))
            nt_dims = (((1,), (1,)), ((), ()))
            s_c = lax.dot_general(qm, kc[:, cols], nt_dims, preferred_element_type=F32) + bc_ref[h]
            s_n = lax.dot_general(qm, kn_ref[:, cols], nt_dims, preferred_element_type=F32) + bn_ref[h]
            m = jnp.maximum(jnp.max(s_c, axis=-1, keepdims=True), jnp.max(s_n, axis=-1, keepdims=True))
            e_c = jnp.exp2(s_c - m)
            e_n = jnp.exp2(s_n - m)
            l = jnp.sum(e_c, axis=-1, keepdims=True) + jnp.sum(e_n, axis=-1, keepdims=True)
            pv = (jnp.dot(e_c.astype(BF16), vc[:, cols], preferred_element_type=F32)
                  + jnp.dot(e_n.astype(BF16), vn_ref[:, cols], preferred_element_type=F32)) * (1.0 / l)
            acc = pv if acc is None else jnp.where(sel, pv, acc)
        o_ref[:, cols] = acc.astype(o_ref.dtype)


def _sample_attention(u, cache_k, cache_v, bias_c, bias_n, batch, t):
    n_past = cache_k.shape[1]
    return pl.pallas_call(
        _sample_attn_kernel,
        grid=(batch,),
        in_specs=[
            pl.BlockSpec((t, A_WIDTH), lambda b: (b, 0)),
            pl.BlockSpec((t, A_WIDTH), lambda b: (b, 1)),
            pl.BlockSpec((t, A_WIDTH), lambda b: (b, 2)),
            pl.BlockSpec((1, n_past, A_WIDTH), lambda b: (b, 0, 0)),
            pl.BlockSpec((1, n_past, A_WIDTH), lambda b: (b, 0, 0)),
            pl.BlockSpec((HEADS, t, n_past), lambda b: (0, 0, 0)),
            pl.BlockSpec((HEADS, t, t), lambda b: (0, 0, 0)),
        ],
        out_specs=pl.BlockSpec((t, A_WIDTH), lambda b: (b, 0)),
        out_shape=jax.ShapeDtypeStruct((batch * t, A_WIDTH), BF16),
        compiler_params=_params("arbitrary"),
    )(u, u, u, cache_k, cache_v, bias_c, bias_n)


def _sample_bias_tables(rel_bias, t, n_past):
    def by_dist(lo, hi):
        dist = jnp.arange(lo, hi + 1, dtype=jnp.int32)
        return rel_bias[:, jnp.clip(dist, -REL_CLIP, REL_CLIP) + REL_CLIP].astype(F32) * LOG2E

    cache = by_dist(-n_past - t + 1, -1)
    new = by_dist(-t + 1, t - 1)
    return _hankel(cache, t, n_past)[:, ::-1], _hankel(new, t, t)[:, ::-1]


def _mix_stages(x_ref, ya_ref, ab_ref, gc_ref, gb_ref, hc_ref, wb_ref, bb_ref, lg_ref, lb_ref, wc_ref, go_ref,
                wo_ref, cb, cc, sh):
    tm = x_ref.shape[0]
    ab = ab_ref[...].astype(F32)
    glu = ab[:, :B_WIDTH] * _sigmoid(ab[:, B_WIDTH:])
    pre = gc_ref[...].astype(F32) * hc_ref[...].astype(F32)
    cb[HIST_B:HIST_B + tm, :] = glu
    cc[HIST_C:HIST_C + tm, :] = pre
    yield _token(glu) + _token(pre)
    span = tm + HIST_B - SUBLANES
    moved = 0.0
    for r in range(1, SUBLANES):
        shifted = cb[pl.ds(r, span), :]
        sh[r - 1, 0:span, :] = shifted
        moved = moved + _token(shifted)
        if r in MIX_SHIFT_BREAKS:
            yield moved
            moved = 0.0
    z = jnp.zeros((tm, B_WIDTH), F32) + bb_ref[...]
    for j in range(B_KERNEL):
        m = j + HIST_B - (B_KERNEL - 1)
        a = m // SUBLANES * SUBLANES
        src = cb[a:a + tm, :] if m == a else sh[m - a - 1, a:a + tm, :]
        z = z + wb_ref[j:j + 1, :] * src
        if j in MIX_TAP_BREAKS:
            yield _token(z)
    mu = jnp.mean(z, axis=-1, keepdims=True)
    zc = z - mu
    var = jnp.mean(zc * zc, axis=-1, keepdims=True)
    yb = zc * lax.rsqrt(var + EPS) * lg_ref[...] + lb_ref[...]
    yb = yb * _sigmoid(yb)

    conv = jnp.zeros((tm, C_WIDTH), F32)
    for j in range(C_KERNEL):
        conv = conv + wc_ref[j:j + 1, :] * cc[pl.ds(j + HIST_C - (C_KERNEL - 1), tm), :]
    yc = gb_ref[...].astype(F32) * conv
    yield _token(yb) + _token(yc)

    go = go_ref[...]
    na = _rms(ya_ref[...].astype(F32), go[:, :A_WIDTH]).astype(BF16)
    nb = _rms(yb, go[:, A_WIDTH:A_WIDTH + B_WIDTH]).astype(BF16)
    nc = _rms(yc, go[:, A_WIDTH + B_WIDTH:]).astype(BF16)
    return (x_ref[...]
            + jnp.dot(na, wo_ref[0:A_WIDTH, :], preferred_element_type=F32)
            + jnp.dot(nb, wo_ref[A_WIDTH:A_WIDTH + B_WIDTH, :], preferred_element_type=F32)
            + jnp.dot(nc, wo_ref[A_WIDTH + B_WIDTH:, :], preferred_element_type=F32))


def _token(v):
    t = jnp.sum(v, axis=0, keepdims=True)
    return sum(t[:, c:c + LANES] for c in range(0, v.shape[1], LANES))


def _finish(stages):
    while True:
        try:
            next(stages)
        except StopIteration as done:
            return done.value


def _mix_carry(tm, cb, cc, nb_ref, nc_ref):
    tail_b = cb[tm:tm + HIST_B, :]
    tail_c = cc[tm:tm + HIST_C, :]
    nb_ref[0] = tail_b
    nc_ref[0] = tail_c
    cb[0:HIST_B, :] = tail_b
    cc[0:HIST_C, :] = tail_c


def _mix_kernel(x_ref, ya_ref, ab_ref, gc_ref, gb_ref, hc_ref, hb0_ref, hc0_ref,
                wb_ref, bb_ref, lg_ref, lb_ref, wc_ref, go_ref, wo_ref,
                h_ref, nb_ref, nc_ref, cb, cc, sh):
    @pl.when(pl.program_id(1) == 0)
    def _():
        cb[0:HIST_B, :] = hb0_ref[0]
        cc[0:HIST_C, :] = hc0_ref[0]

    h_ref[...] = _finish(_mix_stages(x_ref, ya_ref, ab_ref, gc_ref, gb_ref, hc_ref, wb_ref, bb_ref, lg_ref,
                                     lb_ref, wc_ref, go_ref, wo_ref, cb, cc, sh))
    _mix_carry(x_ref.shape[0], cb, cc, nb_ref, nc_ref)


def _mix(x, ya, u, hist_b, hist_c, wb, bb, lg, lb, wc, go, wo, batch, t, tm):
    n, d = x.shape
    nt = t // tm
    row = lambda b, i: b * nt + i
    ab_blk = 3 * A_WIDTH // (2 * B_WIDTH)
    c_blk = (3 * A_WIDTH + 2 * B_WIDTH) // C_WIDTH
    const = lambda b, i: (0, 0)
    return pl.pallas_call(
        _mix_kernel,
        grid=(batch, nt),
        in_specs=[
            pl.BlockSpec((tm, d), lambda b, i: (row(b, i), 0)),
            pl.BlockSpec((tm, A_WIDTH), lambda b, i: (row(b, i), 0)),
            pl.BlockSpec((tm, 2 * B_WIDTH), lambda b, i: (row(b, i), ab_blk)),
            pl.BlockSpec((tm, C_WIDTH), lambda b, i: (row(b, i), c_blk)),
            pl.BlockSpec((tm, C_WIDTH), lambda b, i: (row(b, i), c_blk + 1)),
            pl.BlockSpec((tm, C_WIDTH), lambda b, i: (row(b, i), c_blk + 2)),
            pl.BlockSpec((1, HIST_B, B_WIDTH), lambda b, i: (b, 0, 0)),
            pl.BlockSpec((1, HIST_C, C_WIDTH), lambda b, i: (b, 0, 0)),
            pl.BlockSpec(wb.shape, const), pl.BlockSpec(bb.shape, const),
            pl.BlockSpec(lg.shape, const), pl.BlockSpec(lb.shape, const),
            pl.BlockSpec(wc.shape, const), pl.BlockSpec(go.shape, const),
            pl.BlockSpec(wo.shape, const),
        ],
        out_specs=[pl.BlockSpec((tm, d), lambda b, i: (row(b, i), 0)),
                   pl.BlockSpec((1, HIST_B, B_WIDTH), lambda b, i: (b, 0, 0)),
                   pl.BlockSpec((1, HIST_C, C_WIDTH), lambda b, i: (b, 0, 0))],
        out_shape=[jax.ShapeDtypeStruct((n, d), F32),
                   jax.ShapeDtypeStruct((batch, HIST_B, B_WIDTH), F32),
                   jax.ShapeDtypeStruct((batch, HIST_C, C_WIDTH), F32)],
        scratch_shapes=[pltpu.VMEM((HIST_B + tm, B_WIDTH), F32),
                        pltpu.VMEM((HIST_C + tm, C_WIDTH), F32),
                        pltpu.VMEM((SUBLANES - 1, HIST_B + tm - SUBLANES, B_WIDTH), F32)],
        compiler_params=_params("arbitrary", "arbitrary"),
    )(x, ya, u, u, u, u, hist_b, hist_c, wb, bb, lg, lb, wc, go, wo)


def _ple_epilogue(h, p, gp, wpg, wpp, gf):
    gate = _sigmoid(jnp.dot(_rms(h, gp).astype(BF16), wpg, preferred_element_type=F32))
    h = h + gate * jnp.dot(p.astype(BF16), wpp, preferred_element_type=F32)
    return h if gf is None else _rms(h, gf)


def _swiglu_chunks(x, wg_ref, wu_ref, wd_ref, acc, between=None):
    f = wg_ref.shape[-1]
    for c in range(f // FF_CHUNK):
        cols = slice(c * FF_CHUNK, (c + 1) * FF_CHUNK)
        a = jnp.dot(x, wg_ref[0, :, cols], preferred_element_type=F32)
        b = jnp.dot(x, wu_ref[0, :, cols], preferred_element_type=F32)
        mid = (a * _sigmoid(a) * b).astype(BF16)
        part = jnp.dot(mid, wd_ref[0, cols, :], preferred_element_type=F32)
        if c == 0:
            acc[...] = part
        else:
            acc[...] += part
        if between is not None:
            done = next(between)
            acc[0:SUBLANES, 0:LANES] += jnp.where(done != done, done, 0.0)


def _ffn_kernel(final, h_ref, g_ref, wg_ref, wu_ref, wd_ref, p_ref, gp_ref, wpg_ref, wpp_ref, gf_ref,
                o_ref, acc):
    xn = _rms(h_ref[...], g_ref[...]).astype(BF16)
    _swiglu_chunks(xn, wg_ref, wu_ref, wd_ref, acc)
    h = h_ref[...] + acc[...]
    o_ref[...] = _ple_epilogue(h, p_ref[0], gp_ref[...], wpg_ref[...], wpp_ref[...],
                               gf_ref[...] if final else None)


def _resident(shape):
    return pl.BlockSpec(shape, lambda *_: (0,) * len(shape), pipeline_mode=pl.Buffered(1))


def _ffn(h, g, wg, wu, wd, j, p, layer, gp, wpg, wpp, gf, final, tm):
    n, d = h.shape
    f = wg.shape[-1]
    const = lambda i: (0, 0)
    return pl.pallas_call(
        functools.partial(_ffn_kernel, final),
        grid=(n // tm,),
        in_specs=[
            pl.BlockSpec((tm, d), lambda i: (i, 0)),
            pl.BlockSpec((1, d), const),
            pl.BlockSpec((1, d, f), lambda i: (j, 0, 0), pipeline_mode=pl.Buffered(1)),
            pl.BlockSpec((1, d, f), lambda i: (j, 0, 0), pipeline_mode=pl.Buffered(1)),
            pl.BlockSpec((1, f, d), lambda i: (j, 0, 0), pipeline_mode=pl.Buffered(1)),
            pl.BlockSpec((1, tm, p.shape[-1]), lambda i: (layer, i, 0)),
            pl.BlockSpec((1, d), const),
            _resident(wpg.shape),
            _resident(wpp.shape),
            pl.BlockSpec((1, d), const),
        ],
        out_specs=pl.BlockSpec((tm, d), lambda i: (i, 0)),
        out_shape=jax.ShapeDtypeStruct((n, d), F32),
        scratch_shapes=[pltpu.VMEM((tm, d), F32)],
        compiler_params=_params("arbitrary"),
    )(h, g, wg, wu, wd, p, gp, wpg, wpp, gf)


def _mix_ffn_kernel(final, nt,
                    x_ref, ya_ref, ab_ref, gc_ref, gb_ref, hc_ref, hb0_ref, hc0_ref,
                    wb_ref, bb_ref, lg_ref, lb_ref, wc_ref, go_ref, wo_ref,
                    g_ref, wg_ref, wu_ref, wd_ref, p_ref, gp_ref, wpg_ref, wpp_ref, gf_ref,
                    o_ref, nb_ref, nc_ref, cb, cc, sh, h1a, h1b, acc):
    g = pl.program_id(0)
    tiles = pl.num_programs(0) - 1
    tm = x_ref.shape[0]

    @pl.when(g == 0)
    def _():
        h1b[...] = jnp.zeros_like(h1b)

    @pl.when(jnp.minimum(g, tiles - 1) % nt == 0)
    def _():
        cb[0:HIST_B, :] = hb0_ref[0]
        cc[0:HIST_C, :] = hc0_ref[0]

    def step(h_new, h_old):
        mixers = _mix_stages(x_ref, ya_ref, ab_ref, gc_ref, gb_ref, hc_ref, wb_ref, bb_ref, lg_ref, lb_ref,
                             wc_ref, go_ref, wo_ref, cb, cc, sh)
        _swiglu_chunks(_rms(h_old[...], g_ref[...]).astype(BF16), wg_ref, wu_ref, wd_ref, acc, mixers)
        h_new[...] = _finish(mixers)
        o_ref[...] = _ple_epilogue(h_old[...] + acc[...], p_ref[0], gp_ref[...], wpg_ref[...], wpp_ref[...],
                                   gf_ref[...] if final else None)

    @pl.when(g % 2 == 0)
    def _():
        step(h1a, h1b)

    @pl.when(g % 2 == 1)
    def _():
        step(h1b, h1a)

    @pl.when(g < tiles)
    def _():
        _mix_carry(tm, cb, cc, nb_ref, nc_ref)


def _mix_ffn(x, ya, u, hist_b, hist_c, wb, bb, lg, lb, wc, go, wo,
             g, wg, wu, wd, j, p, layer, gp, wpg, wpp, gf, final, batch, t, tm):
    n, d = x.shape
    f = wg.shape[-1]
    nt = t // tm
    tiles = n // tm
    cur = lambda i: jnp.minimum(i, tiles - 1)
    old = lambda i: jnp.maximum(i - 1, 0)
    ab_blk = 3 * A_WIDTH // (2 * B_WIDTH)
    c_blk = (3 * A_WIDTH + 2 * B_WIDTH) // C_WIDTH
    const = lambda i: (0, 0)
    state = lambda rows, width: pl.BlockSpec((1, rows, width), lambda i: (cur(i) // nt, 0, 0))
    weights = lambda: pl.BlockSpec((1,) + wg.shape[1:], lambda i: (j, 0, 0), pipeline_mode=pl.Buffered(1))
    return pl.pallas_call(
        functools.partial(_mix_ffn_kernel, final, nt),
        grid=(tiles + 1,),
        in_specs=[
            pl.BlockSpec((tm, d), lambda i: (cur(i), 0)),
            pl.BlockSpec((tm, A_WIDTH), lambda i: (cur(i), 0)),
            pl.BlockSpec((tm, 2 * B_WIDTH), lambda i: (cur(i), ab_blk)),
            pl.BlockSpec((tm, C_WIDTH), lambda i: (cur(i), c_blk)),
            pl.BlockSpec((tm, C_WIDTH), lambda i: (cur(i), c_blk + 1)),
            pl.BlockSpec((tm, C_WIDTH), lambda i: (cur(i), c_blk + 2)),
            state(HIST_B, B_WIDTH), state(HIST_C, C_WIDTH),
            pl.BlockSpec(wb.shape, const), pl.BlockSpec(bb.shape, const),
            pl.BlockSpec(lg.shape, const), pl.BlockSpec(lb.shape, const),
            pl.BlockSpec(wc.shape, const), pl.BlockSpec(go.shape, const),
            _resident(wo.shape),
            pl.BlockSpec((1, d), const),
            weights(), weights(),
            pl.BlockSpec((1, f, d), lambda i: (j, 0, 0), pipeline_mode=pl.Buffered(1)),
            pl.BlockSpec((1, tm, p.shape[-1]), lambda i: (layer, old(i), 0)),
            pl.BlockSpec((1, d), const),
            _resident(wpg.shape),
            _resident(wpp.shape),
            pl.BlockSpec((1, d), const),
        ],
        out_specs=[pl.BlockSpec((tm, d), lambda i: (old(i), 0)),
                   state(HIST_B, B_WIDTH), state(HIST_C, C_WIDTH)],
        out_shape=[jax.ShapeDtypeStruct((n, d), F32),
                   jax.ShapeDtypeStruct((batch, HIST_B, B_WIDTH), F32),
                   jax.ShapeDtypeStruct((batch, HIST_C, C_WIDTH), F32)],
        scratch_shapes=[pltpu.VMEM((HIST_B + tm, B_WIDTH), F32),
                        pltpu.VMEM((HIST_C + tm, C_WIDTH), F32),
                        pltpu.VMEM((SUBLANES - 1, HIST_B + tm - SUBLANES, B_WIDTH), F32),
                        pltpu.VMEM((tm, d), F32),
                        pltpu.VMEM((tm, d), F32),
                        pltpu.VMEM((tm, d), F32)],
        compiler_params=_params("arbitrary", vmem=FUSED_VMEM_LIMIT),
    )(x, ya, u, u, u, u, hist_b, hist_c, wb, bb, lg, lb, wc, go, wo,
      g, wg, wu, wd, p, gp, wpg, wpp, gf)


def _router_kernel(h_ref, g_ref, wr_ref, br_ref, x3_ref, meta_ref, cnt_ref, carry):
    tm = h_ref.shape[0]

    @pl.when(pl.program_id(0) == 0)
    def _():
        carry[...] = jnp.zeros_like(carry)

    xn = _rms(h_ref[...], g_ref[...])
    for s in range(ROW_SPLIT):
        x3_ref[_chunk(s, tm), :] = xn[:, s * LANES:(s + 1) * LANES]
    logits = lax.dot_general(wr_ref[...], xn, (((1,), (1,)), ((), ())),
                             precision=lax.Precision.HIGHEST,
                             preferred_element_type=F32) + br_ref[...]
    eidx = lax.broadcasted_iota(jnp.int32, (N_EXPERTS, tm), 0)
    m1 = jnp.max(logits, axis=0, keepdims=True)
    i1 = jnp.min(jnp.where(logits == m1, eidx, N_EXPERTS), axis=0, keepdims=True)
    rest = jnp.where(eidx == i1, -jnp.inf, logits)
    m2 = jnp.max(rest, axis=0, keepdims=True)
    i2 = jnp.min(jnp.where(rest == m2, eidx, N_EXPERTS), axis=0, keepdims=True)
    e2 = jnp.exp(m2 - m1)
    g1 = 1.0 / (1.0 + e2)
    g2 = e2 / (1.0 + e2)
    hit1 = eidx == i1
    hit2 = eidx == i2
    chosen = jnp.where(hit1 | hit2, 1.0, 0.0)
    src = lax.broadcasted_iota(jnp.int32, (tm, tm), 0)
    dst = lax.broadcasted_iota(jnp.int32, (tm, tm), 1)
    before = jnp.where(src < dst, 1.0, 0.0).astype(BF16)
    rank = jnp.dot(chosen.astype(BF16), before, preferred_element_type=F32) + carry[:, 0:1]
    r1 = jnp.sum(jnp.where(hit1, rank, 0.0), axis=0, keepdims=True)
    r2 = jnp.sum(jnp.where(hit2, rank, 0.0), axis=0, keepdims=True)
    zero = jnp.zeros_like(g1)
    meta_ref[...] = jnp.concatenate(
        [i1.astype(F32), i2.astype(F32), g1, g2, r1, r2, zero, zero], axis=0)
    carry[...] = carry[...] + jnp.sum(chosen, axis=1, keepdims=True)
    cnt_ref[...] = carry[...]


def _router(h, g, wr_t, br, tm):
    n, d = h.shape
    return pl.pallas_call(
        _router_kernel,
        grid=(n // tm,),
        in_specs=[pl.BlockSpec((tm, d), lambda i: (i, 0)),
                  pl.BlockSpec((1, d), lambda i: (0, 0)),
                  pl.BlockSpec((N_EXPERTS, d), lambda i: (0, 0)),
                  pl.BlockSpec((N_EXPERTS, 1), lambda i: (0, 0))],
        out_specs=[pl.BlockSpec((tm * ROW_SPLIT, LANES), lambda i: (i, 0)),
                   pl.BlockSpec((8, tm), lambda i: (0, i)),
                   pl.BlockSpec((N_EXPERTS, LANES), lambda i: (0, 0))],
        out_shape=[jax.ShapeDtypeStruct((n * ROW_SPLIT, LANES), F32),
                   jax.ShapeDtypeStruct((8, n), F32),
                   jax.ShapeDtypeStruct((N_EXPERTS, LANES), F32)],
        scratch_shapes=[pltpu.VMEM((N_EXPERTS, LANES), F32)],
        compiler_params=_params("arbitrary"),
    )(h, g, wr_t, br)


def _dispatch_kernel(d1_ref, d2_ref, x_ref, o_hbm, sem):
    tm = x_ref.shape[0] // ROW_SPLIT

    for t in range(tm):
        pltpu.make_async_copy(x_ref.at[_row_tile(t)], o_hbm.at[_row_tile(d1_ref[0, 0, t])], sem).start(0)
        pltpu.make_async_copy(x_ref.at[_row_tile(t)], o_hbm.at[_row_tile(d2_ref[0, 0, t])], sem).start(1)
    for _ in range(2):
        pltpu.make_async_copy(x_ref, o_hbm.at[pl.ds(0, tm * ROW_SPLIT)], sem).wait()


def _dispatch(x3, d1, d2, rows_out, tm):
    n = x3.shape[0] // ROW_SPLIT
    steps = d1.shape[0] // tm
    last = n // tm - 1
    idx_spec = pl.BlockSpec((1, 1, tm), lambda i: (i, 0, 0), memory_space=pltpu.SMEM)
    return pl.pallas_call(
        _dispatch_kernel,
        grid=(steps,),
        in_specs=[idx_spec, idx_spec,
                  pl.BlockSpec((tm * ROW_SPLIT, LANES), lambda i: (jnp.minimum(i, last), 0))],
        out_specs=pl.BlockSpec(memory_space=pl.ANY),
        out_shape=jax.ShapeDtypeStruct((rows_out * ROW_SPLIT, LANES), x3.dtype),
        scratch_shapes=[pltpu.SemaphoreType.DMA(())],
        compiler_params=_params("arbitrary"),
    )(d1.reshape(steps, 1, tm), d2.reshape(steps, 1, tm), x3)


def _expert_kernel(texp_ref, nact_ref, x_ref, wg_ref, wu_ref, wd_ref, y_ref, x2, acc):
    del texp_ref

    @pl.when(pl.program_id(0) < nact_ref[0])
    def _():
        for s in range(ROW_SPLIT):
            x2[:, s * LANES:(s + 1) * LANES] = x_ref[_chunk(s, EXPERT_TILE), :].astype(BF16)
        _swiglu_chunks(x2[...], wg_ref, wu_ref, wd_ref, acc)
        for s in range(ROW_SPLIT):
            y_ref[_chunk(s, EXPERT_TILE), :] = acc[:, s * LANES:(s + 1) * LANES]

    @pl.when(pl.program_id(0) >= nact_ref[0])
    def _():
        y_ref[...] = jnp.zeros_like(y_ref)


def _experts(xs, tile_exp, n_active, wg, wu, wd):
    ntile = tile_exp.shape[0]
    d, f = wg.shape[1], wg.shape[2]
    grid_spec = pltpu.PrefetchScalarGridSpec(
        num_scalar_prefetch=2,
        grid=(ntile,),
        in_specs=[
            pl.BlockSpec((EXPERT_TILE * ROW_SPLIT, LANES), lambda t, te, na: (t, 0)),
            pl.BlockSpec((1, d, f), lambda t, te, na: (te[t], 0, 0)),
            pl.BlockSpec((1, d, f), lambda t, te, na: (te[t], 0, 0)),
            pl.BlockSpec((1, f, d), lambda t, te, na: (te[t], 0, 0)),
        ],
        out_specs=pl.BlockSpec((EXPERT_TILE * ROW_SPLIT, LANES), lambda t, te, na: (t, 0)),
        scratch_shapes=[pltpu.VMEM((EXPERT_TILE, d), BF16), pltpu.VMEM((EXPERT_TILE, d), F32)],
    )
    return pl.pallas_call(
        _expert_kernel,
        grid_spec=grid_spec,
        out_shape=jax.ShapeDtypeStruct(xs.shape, F32),
        compiler_params=_params("arbitrary"),
    )(tile_exp, n_active, xs, wg, wu, wd)


def _combine_kernel(final, d1a, d2a, d1b, d2b, d1c, d2c, ye_hbm, h_ref, gate_ref, p_ref, gp_ref, wpg_ref, wpp_ref,
                    gf_ref, o_ref, ya1, ya2, yb1, yb2, moe, sems):
    j = pl.program_id(0)
    tm = h_ref.shape[0] // 2

    def gather(d1_ref, d2_ref, y1, y2, sem):
        for t in range(tm):
            pltpu.make_async_copy(ye_hbm.at[_row_tile(d1_ref[0, 0, t])], y1.at[_row_tile(t)], sem).start(0)
            pltpu.make_async_copy(ye_hbm.at[_row_tile(d2_ref[0, 0, t])], y2.at[_row_tile(t)], sem).start(1)

    def wait(y1, y2, sem):
        pltpu.make_async_copy(ye_hbm.at[pl.ds(0, tm * ROW_SPLIT)], y1, sem).wait()
        pltpu.make_async_copy(ye_hbm.at[pl.ds(0, tm * ROW_SPLIT)], y2, sem).wait()

    def combine(y1, y2, rows):
        g1 = gate_ref[rows, 0:1]
        g2 = gate_ref[rows, 1:2]
        for s in range(ROW_SPLIT):
            moe[:, s * LANES:(s + 1) * LANES] = g1 * y1[_chunk(s, tm), :] + g2 * y2[_chunk(s, tm), :]
        h = h_ref[rows, :] + moe[...]
        o_ref[rows, :] = _ple_epilogue(h, p_ref[0, rows, :], gp_ref[...], wpg_ref[...], wpp_ref[...],
                                       gf_ref[...] if final else None)

    @pl.when(j == 0)
    def _():
        gather(d1a, d2a, ya1, ya2, sems.at[0])

    wait(ya1, ya2, sems.at[0])
    gather(d1b, d2b, yb1, yb2, sems.at[1])
    combine(ya1, ya2, slice(0, tm))
    wait(yb1, yb2, sems.at[1])
    gather(d1c, d2c, ya1, ya2, sems.at[0])
    combine(yb1, yb2, slice(tm, 2 * tm))

    @pl.when(j == pl.num_programs(0) - 1)
    def _():
        wait(ya1, ya2, sems.at[0])


def _combine(ye, d1, d2, h, gates, p, layer, gp, wpg, wpp, gf, final, tm):
    n, d = h.shape
    steps = n // (2 * tm)
    tiles = n // tm
    idx_spec = lambda f: pl.BlockSpec((1, 1, tm), lambda i: (f(i), 0, 0), memory_space=pltpu.SMEM)
    first = idx_spec(lambda i: 2 * i)
    second = idx_spec(lambda i: 2 * i + 1)
    ahead = idx_spec(lambda i: jnp.minimum(2 * i + 2, tiles - 1))
    const = lambda i: (0, 0)
    d1 = d1.reshape(tiles, 1, tm)
    d2 = d2.reshape(tiles, 1, tm)
    row_buf = pltpu.VMEM((tm * ROW_SPLIT, LANES), F32)
    return pl.pallas_call(
        functools.partial(_combine_kernel, final),
        grid=(steps,),
        in_specs=[first, first, second, second, ahead, ahead,
                  pl.BlockSpec(memory_space=pl.ANY),
                  pl.BlockSpec((2 * tm, d), lambda i: (i, 0)),
                  pl.BlockSpec((2 * tm, 2), lambda i: (i, 0)),
                  pl.BlockSpec((1, 2 * tm, p.shape[-1]), lambda i: (layer, i, 0)),
                  pl.BlockSpec((1, d), const),
                  _resident(wpg.shape),
                  _resident(wpp.shape),
                  pl.BlockSpec((1, d), const)],
        out_specs=pl.BlockSpec((2 * tm, d), lambda i: (i, 0)),
        out_shape=jax.ShapeDtypeStruct((n, d), F32),
        scratch_shapes=[row_buf, row_buf, row_buf, row_buf,
                        pltpu.VMEM((tm, d), F32),
                        pltpu.SemaphoreType.DMA((2,))],
        compiler_params=_params("arbitrary"),
    )(d1, d2, d1, d2, d1, d2, ye, h, gates, p, gp, wpg, wpp, gf)


def _moe(h, g, wr_t, br, wg, wu, wd, p, layer, gp, wpg, wpp, gf, final, tm):
    n = h.shape[0]
    x3, meta, cnt = _router(h, g, wr_t, br, tm)
    counts = cnt[:, 0].astype(jnp.int32)
    tiles = (counts + EXPERT_TILE - 1) // EXPERT_TILE
    padded = tiles * EXPERT_TILE
    ends = jnp.cumsum(padded)
    offs = ends - padded
    expert = jnp.arange(N_EXPERTS, dtype=jnp.int32)[:, None]
    start_of = lambda idx: jnp.sum(jnp.where(idx[None, :] == expert, offs[:, None], 0), axis=0)
    d1 = start_of(meta[0].astype(jnp.int32)) + meta[4].astype(jnp.int32)
    d2 = start_of(meta[1].astype(jnp.int32)) + meta[5].astype(jnp.int32)
    gates = jnp.stack([meta[2], meta[3]], axis=1)
    ntile = 2 * n // EXPERT_TILE + N_EXPERTS
    r = jnp.arange(EXPERT_TILE, dtype=jnp.int32)[None, :]
    in_group = counts[:, None] + r < padded[:, None]
    spare = ends[-1] + (jnp.cumsum(jnp.where(in_group, 0, 1).reshape(-1)) - 1).reshape(in_group.shape)
    fill = jnp.where(in_group, offs[:, None] + counts[:, None] + r, spare).reshape(-1).astype(jnp.int32)
    half = fill.shape[0] // 2
    xs = _dispatch(x3, jnp.concatenate([d1, fill[:half]]), jnp.concatenate([d2, fill[half:]]),
                   ntile * EXPERT_TILE, tm)
    n_active = jnp.sum(tiles).astype(jnp.int32)
    tile_start = jnp.minimum(jnp.arange(ntile, dtype=jnp.int32), n_active - 1) * EXPERT_TILE
    tile_exp = jnp.minimum(jnp.sum(jnp.where(tile_start[:, None] >= ends[None, :], 1, 0), axis=1),
                           N_EXPERTS - 1).astype(jnp.int32)
    ye = _experts(xs, tile_exp, n_active.reshape(1), wg, wu, wd)
    return _combine(ye, d1, d2, h, gates, p, layer, gp, wpg, wpp, gf, final, min(tm, n // 2))


def _trunk(x, p, past, prm, tok_tile, mix_tile):
    batch, t, d = x.shape
    n = batch * t
    depth = prm["w_in"].shape[0]
    h = x.reshape(n, d)
    new_k, new_v, new_b, new_c = [], [], [], []
    row = lambda a: a.reshape(1, -1)
    p = p.reshape(depth, n, -1)
    keep = min(BAND, t) if past is None else t
    for i in range(depth):
        u, kv = _inproj(h, row(prm["norm_mix"][i]), prm["w_in"][i], tok_tile, t, keep)
        new_k.append(kv[:, :A_WIDTH].reshape(batch, keep, HEADS, HEAD_DIM))
        new_v.append(kv[:, A_WIDTH:].reshape(batch, keep, HEADS, HEAD_DIM))
        if past is None:
            ya = _prompt_attention(u, _prompt_bias_table(prm["rel_bias"][i]), batch, t)
            hist_b = jnp.zeros((batch, HIST_B, B_WIDTH), F32)
            hist_c = jnp.zeros((batch, HIST_C, C_WIDTH), F32)
        else:
            ck, cv, sb, sc = past
            n_past = ck.shape[2]
            bias_c, bias_n = _sample_bias_tables(prm["rel_bias"][i], t, n_past)
            ya = _sample_attention(u, ck[i].reshape(batch, n_past, A_WIDTH),
                                   cv[i].reshape(batch, n_past, A_WIDTH), bias_c, bias_n, batch, t)
            hist_b = jnp.pad(sb[i], ((0, 0), (HIST_B - (B_KERNEL - 1), 0), (0, 0)))
            hist_c = jnp.pad(sc[i], ((0, 0), (HIST_C - (C_KERNEL - 1), 0), (0, 0)))
        wb = jnp.pad(prm["w_dw_b"][i], ((0, HIST_B - B_KERNEL), (0, 0)))
        wc = jnp.pad(prm["w_dw_c"][i], ((0, HIST_C - C_KERNEL), (0, 0)))
        mix_args = (h, ya, u, hist_b, hist_c, wb, row(prm["b_dw_b"][i]), row(prm["ln_g_b"][i]),
                    row(prm["ln_b_b"][i]), wc, row(prm["g_out"][i]), prm["w_out"][i])
        final = i == depth - 1
        tail = (p, i, row(prm["norm_ple"][i]), prm["w_ple_gate"][i], prm["w_ple_proj"][i],
                row(prm["norm_final"]), final)
        j = i // 2
        ffn_args = (row(prm["norm_ffn"][i]), prm["w_ff_gate"], prm["w_ff_up"], prm["w_ff_down"], j)
        if i % 2 == 0 and tok_tile == mix_tile:
            h, nb, nc = _mix_ffn(*mix_args, *ffn_args, *tail, batch, t, mix_tile)
        else:
            h, nb, nc = _mix(*mix_args, batch, t, mix_tile)
            if i % 2 == 0:
                h = _ffn(h, *ffn_args, *tail, tok_tile)
            else:
                h = _moe(h, row(prm["norm_ffn"][i]), prm["w_router"][j].T, prm["b_router"][j].reshape(-1, 1),
                         prm["w_ex_gate"][j], prm["w_ex_up"][j], prm["w_ex_down"][j], *tail, tok_tile)
        new_b.append(nb[:, HIST_B - (B_KERNEL - 1):])
        new_c.append(nc[:, HIST_C - (C_KERNEL - 1):])
    return (h.reshape(batch, t, d), jnp.stack(new_k), jnp.stack(new_v), jnp.stack(new_b), jnp.stack(new_c))


def kernel(x_prompt, x_sample, p_prompt, p_sample, cache_attn_k, cache_attn_v, state_conv_b, state_conv_c,
           w_in, rel_bias, w_dw_b, b_dw_b, ln_g_b, ln_b_b, w_dw_c, g_out, w_out, norm_mix, norm_ffn,
           w_ff_gate, w_ff_up, w_ff_down, w_router, b_router, w_ex_gate, w_ex_up, w_ex_down,
           norm_ple, w_ple_gate, w_ple_proj, norm_final):
    prm = dict(
        w_in=w_in.astype(BF16), rel_bias=rel_bias, w_dw_b=w_dw_b, b_dw_b=b_dw_b, ln_g_b=ln_g_b,
        ln_b_b=ln_b_b, w_dw_c=w_dw_c, g_out=g_out, w_out=w_out.astype(BF16), norm_mix=norm_mix,
        norm_ffn=norm_ffn, w_ff_gate=w_ff_gate.astype(BF16), w_ff_up=w_ff_up.astype(BF16),
        w_ff_down=w_ff_down.astype(BF16), w_router=w_router, b_router=b_router,
        w_ex_gate=w_ex_gate.astype(BF16), w_ex_up=w_ex_up.astype(BF16), w_ex_down=w_ex_down.astype(BF16),
        norm_ple=norm_ple, w_ple_gate=w_ple_gate.astype(BF16), w_ple_proj=w_ple_proj.astype(BF16),
        norm_final=norm_final)
    dec_t = x_sample.shape[1]
    y_p, k_p, v_p, b_p, c_p = _trunk(x_prompt, p_prompt, None, prm, 512, 512)
    y_s, k_s, v_s, b_s, c_s = _trunk(x_sample, p_sample,
                                     (cache_attn_k, cache_attn_v, state_conv_b, state_conv_c),
                                     prm, x_sample.shape[0] * dec_t, dec_t)
    return (y_p, y_s, k_p, v_p, b_p, c_p, k_s, v_s, b_s, c_s)
```

```python
import functools

import jax
import jax.numpy as jnp
from jax import lax
from jax.experimental import pallas as pl
from jax.experimental.pallas import tpu as pltpu

F32 = jnp.float32
BF16 = jnp.bfloat16

CHUNK = 64
BAND = 512
HEADS = 8
HEAD_DIM = 64
A_WIDTH = HEADS * HEAD_DIM
B_WIDTH = 256
C_WIDTH = 256
B_KERNEL = 31
C_KERNEL = 3
REL_CLIP = 128
N_EXPERTS = 8
EPS = 1e-6
NEG = -1e30
LOG2E = 1.4426950408889634

LANES = 128
SUBLANES = 8
ROW_SPLIT = 8
QTILE = 512
PAIR_ROWS = 2 * CHUNK
WIN = BAND + PAIR_ROWS
HIST_B = 32
HIST_C = 8
EXPERT_TILE = 512
FF_CHUNK = 512
MIX_SHIFT_BREAKS = (3, 7)
MIX_TAP_BREAKS = (7, 15, 23, 30)
VMEM_LIMIT = 56 * 1024 * 1024
FUSED_VMEM_LIMIT = 60 * 1024 * 1024


def _params(*sem, vmem=VMEM_LIMIT):
    return pltpu.CompilerParams(dimension_semantics=sem, vmem_limit_bytes=vmem)


def _rms(x, g):
    return x * lax.rsqrt(jnp.mean(x * x, axis=-1, keepdims=True) + EPS) * g


def _sigmoid(x):
    return 1.0 / (1.0 + jnp.exp(-x))


def _chunk(s, n):
    return pl.ds(s, n, stride=ROW_SPLIT)


def _row_tile(r):
    return pl.ds(pl.multiple_of(r * ROW_SPLIT, ROW_SPLIT), ROW_SPLIT)


def _inproj_kernel(x_ref, g_ref, w_ref, u_ref, kv_ref):
    hn = _rms(x_ref[...], g_ref[...]).astype(BF16)
    q = jnp.dot(hn, w_ref[:, :A_WIDTH], preferred_element_type=F32)
    u_ref[:, :A_WIDTH] = (q * (HEAD_DIM ** -0.5 * LOG2E)).astype(BF16)
    kv = jnp.dot(hn, w_ref[:, A_WIDTH:3 * A_WIDTH], preferred_element_type=F32)
    kv_ref[...] = kv
    u_ref[:, A_WIDTH:3 * A_WIDTH] = kv.astype(BF16)
    rest = jnp.dot(hn, w_ref[:, 3 * A_WIDTH:], preferred_element_type=F32)
    u_ref[:, 3 * A_WIDTH:] = rest.astype(BF16)


def _inproj(x, g, w, tm, t, keep):
    n, d = x.shape
    wi = w.shape[1]
    if keep == t:
        kv_index = lambda i: (i, 0)
    else:
        nt, ktiles = t // tm, keep // tm
        kv_index = lambda i: ((i // nt) * ktiles + jnp.maximum(i % nt - (nt - ktiles), 0), 0)
    return pl.pallas_call(
        _inproj_kernel,
        grid=(n // tm,),
        in_specs=[pl.BlockSpec((tm, d), lambda i: (i, 0)),
                  pl.BlockSpec((1, d), lambda i: (0, 0)),
                  pl.BlockSpec((d, wi), lambda i: (0, 0))],
        out_specs=[pl.BlockSpec((tm, wi), lambda i: (i, 0)),
                   pl.BlockSpec((tm, 2 * A_WIDTH), kv_index)],
        out_shape=[jax.ShapeDtypeStruct((n, wi), BF16),
                   jax.ShapeDtypeStruct((n // t * keep, 2 * A_WIDTH), F32)],
        compiler_params=_params("arbitrary"),
    )(x, g, w)


def _attn_kernel(q_ref, kp_ref, kc_ref, vp_ref, vc_ref, bias_ref, o_ref, k_scr, vt_scr, s_scr):
    i = pl.program_id(1)
    k_scr[0:QTILE, :] = kp_ref[...]
    k_scr[QTILE:2 * QTILE, :] = kc_ref[...]
    vt_scr[:, 0:QTILE] = vp_ref[...].astype(F32).T.astype(BF16)
    vt_scr[:, QTILE:2 * QTILE] = vc_ref[...].astype(F32).T.astype(BF16)
    lane = lax.broadcasted_iota(jnp.int32, (PAIR_ROWS, LANES), 1)
    row = lax.broadcasted_iota(jnp.int32, (WIN, 2 * PAIR_ROWS), 0)
    nblk = QTILE // PAIR_ROWS
    npair = HEADS // 2

    def scores(p):
        cols = slice(p * LANES, (p + 1) * LANES)
        for c2 in range(nblk):
            r0 = c2 * PAIR_ROWS
            kwin = k_scr[r0:r0 + WIN, cols]
            q2 = q_ref[r0:r0 + PAIR_ROWS, cols]
            zero = jnp.zeros_like(q2)
            qq = jnp.concatenate([jnp.where(lane < HEAD_DIM, q2, zero),
                                  jnp.where(lane >= HEAD_DIM, q2, zero)], axis=0)
            s_scr[(p % 2) * nblk + c2] = lax.dot_general(
                kwin, qq, (((1,), (1,)), ((), ())), preferred_element_type=F32)

    def tile(first):
        scores(0)
        for p in range(npair):
            if p + 1 < npair:
                scores(p + 1)
            cols = slice(p * LANES, (p + 1) * LANES)
            for c2 in range(nblk):
                r0 = c2 * PAIR_ROWS
                vwin = vt_scr[cols, r0:r0 + WIN]
                s = s_scr[(p % 2) * nblk + c2] + bias_ref[p]
                if first:
                    s = jnp.where(row < BAND - r0, NEG, s)
                m = jnp.max(s, axis=0, keepdims=True)
                e = jnp.exp2(s - m)
                l = jnp.sum(e, axis=0, keepdims=True)
                ot = jnp.dot(vwin, e.astype(BF16), preferred_element_type=F32)
                ot = ot * (1.0 / l)
                o = jnp.concatenate([ot[:HEAD_DIM, :PAIR_ROWS], ot[HEAD_DIM:, PAIR_ROWS:]], axis=0).T
                o_ref[r0:r0 + PAIR_ROWS, cols] = o.astype(o_ref.dtype)

    @pl.when(i == 0)
    def _():
        tile(True)

    @pl.when(i > 0)
    def _():
        tile(False)


def _prompt_attention(u, bias_t, batch, seq):
    nt = seq // QTILE
    prev = lambda b, i: b * nt + jnp.maximum(i - 1, 0)
    return pl.pallas_call(
        _attn_kernel,
        grid=(batch, nt),
        in_specs=[
            pl.BlockSpec((QTILE, A_WIDTH), lambda b, i: (b * nt + i, 0)),
            pl.BlockSpec((QTILE, A_WIDTH), lambda b, i: (prev(b, i), 1)),
            pl.BlockSpec((QTILE, A_WIDTH), lambda b, i: (b * nt + i, 1)),
            pl.BlockSpec((QTILE, A_WIDTH), lambda b, i: (prev(b, i), 2)),
            pl.BlockSpec((QTILE, A_WIDTH), lambda b, i: (b * nt + i, 2)),
            pl.BlockSpec((HEADS // 2, WIN, 2 * PAIR_ROWS), lambda b, i: (0, 0, 0)),
        ],
        out_specs=pl.BlockSpec((QTILE, A_WIDTH), lambda b, i: (b * nt + i, 0)),
        out_shape=jax.ShapeDtypeStruct((batch * seq, A_WIDTH), BF16),
        scratch_shapes=[pltpu.VMEM((2 * QTILE, A_WIDTH), BF16),
                        pltpu.VMEM((A_WIDTH, 2 * QTILE), BF16),
                        pltpu.VMEM((2 * QTILE // PAIR_ROWS, WIN, 2 * PAIR_ROWS), F32)],
        compiler_params=_params("arbitrary", "arbitrary"),
    )(u, u, u, u, u, bias_t)


def _hankel(v, rows, cols):
    heads, length = v.shape
    flat = jnp.tile(v, (1, rows + 1))[:, :rows * (length + 1)]
    return flat.reshape(heads, rows, length + 1)[:, :, :cols]


def _prompt_bias_table(rel_bias):
    by_dist = jnp.concatenate(
        [jnp.broadcast_to(rel_bias[:, :1], (HEADS, BAND)), rel_bias[:, :2 * REL_CLIP]], axis=1).astype(F32)
    by_sum = _hankel(by_dist, PAIR_ROWS, WIN + 1)
    table = jnp.transpose(by_sum[:, ::-1, 1:], (0, 2, 1))
    j = jnp.arange(WIN, dtype=jnp.int32)[:, None]
    lo = (jnp.arange(PAIR_ROWS, dtype=jnp.int32)[None, :] // CHUNK) * CHUNK
    valid = (j >= lo) & (j < lo + BAND + CHUNK)
    table = jnp.where(valid[None], table * LOG2E, NEG)
    table = table.reshape(HEADS // 2, 2, WIN, PAIR_ROWS)
    return jnp.transpose(table, (0, 2, 1, 3)).reshape(HEADS // 2, WIN, 2 * PAIR_ROWS)


def _sample_attn_kernel(q_ref, kn_ref, vn_ref, ck_ref, cv_ref, bc_ref, bn_ref, o_ref):
    t = q_ref.shape[0]
    lane = lax.broadcasted_iota(jnp.int32, (t, LANES), 1)
    kc = ck_ref[0].astype(BF16)
    vc = cv_ref[0].astype(BF16)
    for p in range(HEADS // 2):
        cols = slice(p * LANES, (p + 1) * LANES)
        q2 = q_ref[:, cols]
        acc = None
        for hh in range(2):
            h = 2 * p + hh
            sel = (lane >= HEAD_DIM) == bool(hh)
            qm = jnp.where(sel, q2, jnp.zeros_like(q2))
            nt_dims = (((1,), (1,)), ((), ()))
            s_c = lax.dot_general(qm, kc[:, cols], nt_dims, preferred_element_type=F32) + bc_ref[h]
            s_n = lax.dot_general(qm, kn_ref[:, cols], nt_dims, preferred_element_type=F32) + bn_ref[h]
            m = jnp.maximum(jnp.max(s_c, axis=-1, keepdims=True), jnp.max(s_n, axis=-1, keepdims=True))
            e_c = jnp.exp2(s_c - m)
            e_n = jnp.exp2(s_n - m)
            l = jnp.sum(e_c, axis=-1, keepdims=True) + jnp.sum(e_n, axis=-1, keepdims=True)
            pv = (jnp.dot(e_c.astype(BF16), vc[:, cols], preferred_element_type=F32)
                  + jnp.dot(e_n.astype(BF16), vn_ref[:, cols], preferred_element_type=F32)) * (1.0 / l)
            acc = pv if acc is None else jnp.where(sel, pv, acc)
        o_ref[:, cols] = acc.astype(o_ref.dtype)


def _sample_attention(u, cache_k, cache_v, bias_c, bias_n, batch, t):
    n_past = cache_k.shape[1]
    return pl.pallas_call(
        _sample_attn_kernel,
        grid=(batch,),
        in_specs=[
            pl.BlockSpec((t, A_WIDTH), lambda b: (b, 0)),
            pl.BlockSpec((t, A_WIDTH), lambda b: (b, 1)),
            pl.BlockSpec((t, A_WIDTH), lambda b: (b, 2)),
            pl.BlockSpec((1, n_past, A_WIDTH), lambda b: (b, 0, 0)),
            pl.BlockSpec((1, n_past, A_WIDTH), lambda b: (b, 0, 0)),
            pl.BlockSpec((HEADS, t, n_past), lambda b: (0, 0, 0)),
            pl.BlockSpec((HEADS, t, t), lambda b: (0, 0, 0)),
        ],
        out_specs=pl.BlockSpec((t, A_WIDTH), lambda b: (b, 0)),
        out_shape=jax.ShapeDtypeStruct((batch * t, A_WIDTH), BF16),
        compiler_params=_params("arbitrary"),
    )(u, u, u, cache_k, cache_v, bias_c, bias_n)


def _sample_bias_tables(rel_bias, t, n_past):
    def by_dist(lo, hi):
        dist = jnp.arange(lo, hi + 1, dtype=jnp.int32)
        return rel_bias[:, jnp.clip(dist, -REL_CLIP, REL_CLIP) + REL_CLIP].astype(F32) * LOG2E

    cache = by_dist(-n_past - t + 1, -1)
    new = by_dist(-t + 1, t - 1)
    return _hankel(cache, t, n_past)[:, ::-1], _hankel(new, t, t)[:, ::-1]


def _mix_stages(x_ref, ya_ref, ab_ref, gc_ref, gb_ref, hc_ref, wb_ref, bb_ref, lg_ref, lb_ref, wc_ref, go_ref,
                wo_ref, cb, cc, sh):
    tm = x_ref.shape[0]
    ab = ab_ref[...].astype(F32)
    glu = ab[:, :B_WIDTH] * _sigmoid(ab[:, B_WIDTH:])
    pre = gc_ref[...].astype(F32) * hc_ref[...].astype(F32)
    cb[HIST_B:HIST_B + tm, :] = glu
    cc[HIST_C:HIST_C + tm, :] = pre
    yield _token(glu) + _token(pre)
    span = tm + HIST_B - SUBLANES
    moved = 0.0
    for r in range(1, SUBLANES):
        shifted = cb[pl.ds(r, span), :]
        sh[r - 1, 0:span, :] = shifted
        moved = moved + _token(shifted)
        if r in MIX_SHIFT_BREAKS:
            yield moved
            moved = 0.0
    z = jnp.zeros((tm, B_WIDTH), F32) + bb_ref[...]
    for j in range(B_KERNEL):
        m = j + HIST_B - (B_KERNEL - 1)
        a = m // SUBLANES * SUBLANES
        src = cb[a:a + tm, :] if m == a else sh[m - a - 1, a:a + tm, :]
        z = z + wb_ref[j:j + 1, :] * src
        if j in MIX_TAP_BREAKS:
            yield _token(z)
    mu = jnp.mean(z, axis=-1, keepdims=True)
    zc = z - mu
    var = jnp.mean(zc * zc, axis=-1, keepdims=True)
    yb = zc * lax.rsqrt(var + EPS) * lg_ref[...] + lb_ref[...]
    yb = yb * _sigmoid(yb)

    conv = jnp.zeros((tm, C_WIDTH), F32)
    for j in range(C_KERNEL):
        conv = conv + wc_ref[j:j + 1, :] * cc[pl.ds(j + HIST_C - (C_KERNEL - 1), tm), :]
    yc = gb_ref[...].astype(F32) * conv
    yield _token(yb) + _token(yc)

    go = go_ref[...]
    na = _rms(ya_ref[...].astype(F32), go[:, :A_WIDTH]).astype(BF16)
    nb = _rms(yb, go[:, A_WIDTH:A_WIDTH + B_WIDTH]).astype(BF16)
    nc = _rms(yc, go[:, A_WIDTH + B_WIDTH:]).astype(BF16)
    return (x_ref[...]
            + jnp.dot(na, wo_ref[0:A_WIDTH, :], preferred_element_type=F32)
            + jnp.dot(nb, wo_ref[A_WIDTH:A_WIDTH + B_WIDTH, :], preferred_element_type=F32)
            + jnp.dot(nc, wo_ref[A_WIDTH + B_WIDTH:, :], preferred_element_type=F32))


def _token(v):
    t = jnp.sum(v, axis=0, keepdims=True)
    return sum(t[:, c:c + LANES] for c in range(0, v.shape[1], LANES))


def _finish(stages):
    while True:
        try:
            next(stages)
        except StopIteration as done:
            return done.value


def _mix_carry(tm, cb, cc, nb_ref, nc_ref):
    tail_b = cb[tm:tm + HIST_B, :]
    tail_c = cc[tm:tm + HIST_C, :]
    nb_ref[0] = tail_b
    nc_ref[0] = tail_c
    cb[0:HIST_B, :] = tail_b
    cc[0:HIST_C, :] = tail_c


def _mix_kernel(x_ref, ya_ref, ab_ref, gc_ref, gb_ref, hc_ref, hb0_ref, hc0_ref,
                wb_ref, bb_ref, lg_ref, lb_ref, wc_ref, go_ref, wo_ref,
                h_ref, nb_ref, nc_ref, cb, cc, sh):
    @pl.when(pl.program_id(1) == 0)
    def _():
        cb[0:HIST_B, :] = hb0_ref[0]
        cc[0:HIST_C, :] = hc0_ref[0]

    h_ref[...] = _finish(_mix_stages(x_ref, ya_ref, ab_ref, gc_ref, gb_ref, hc_ref, wb_ref, bb_ref, lg_ref,
                                     lb_ref, wc_ref, go_ref, wo_ref, cb, cc, sh))
    _mix_carry(x_ref.shape[0], cb, cc, nb_ref, nc_ref)


def _mix(x, ya, u, hist_b, hist_c, wb, bb, lg, lb, wc, go, wo, batch, t, tm):
    n, d = x.shape
    nt = t // tm
    row = lambda b, i: b * nt + i
    ab_blk = 3 * A_WIDTH // (2 * B_WIDTH)
    c_blk = (3 * A_WIDTH + 2 * B_WIDTH) // C_WIDTH
    const = lambda b, i: (0, 0)
    return pl.pallas_call(
        _mix_kernel,
        grid=(batch, nt),
        in_specs=[
            pl.BlockSpec((tm, d), lambda b, i: (row(b, i), 0)),
            pl.BlockSpec((tm, A_WIDTH), lambda b, i: (row(b, i), 0)),
            pl.BlockSpec((tm, 2 * B_WIDTH), lambda b, i: (row(b, i), ab_blk)),
            pl.BlockSpec((tm, C_WIDTH), lambda b, i: (row(b, i), c_blk)),
            pl.BlockSpec((tm, C_WIDTH), lambda b, i: (row(b, i), c_blk + 1)),
            pl.BlockSpec((tm, C_WIDTH), lambda b, i: (row(b, i), c_blk + 2)),
            pl.BlockSpec((1, HIST_B, B_WIDTH), lambda b, i: (b, 0, 0)),
            pl.BlockSpec((1, HIST_C, C_WIDTH), lambda b, i: (b, 0, 0)),
            pl.BlockSpec(wb.shape, const), pl.BlockSpec(bb.shape, const),
            pl.BlockSpec(lg.shape, const), pl.BlockSpec(lb.shape, const),
            pl.BlockSpec(wc.shape, const), pl.BlockSpec(go.shape, const),
            pl.BlockSpec(wo.shape, const),
        ],
        out_specs=[pl.BlockSpec((tm, d), lambda b, i: (row(b, i), 0)),
                   pl.BlockSpec((1, HIST_B, B_WIDTH), lambda b, i: (b, 0, 0)),
                   pl.BlockSpec((1, HIST_C, C_WIDTH), lambda b, i: (b, 0, 0))],
        out_shape=[jax.ShapeDtypeStruct((n, d), F32),
                   jax.ShapeDtypeStruct((batch, HIST_B, B_WIDTH), F32),
                   jax.ShapeDtypeStruct((batch, HIST_C, C_WIDTH), F32)],
        scratch_shapes=[pltpu.VMEM((HIST_B + tm, B_WIDTH), F32),
                        pltpu.VMEM((HIST_C + tm, C_WIDTH), F32),
                        pltpu.VMEM((SUBLANES - 1, HIST_B + tm - SUBLANES, B_WIDTH), F32)],
        compiler_params=_params("arbitrary", "arbitrary"),
    )(x, ya, u, u, u, u, hist_b, hist_c, wb, bb, lg, lb, wc, go, wo)


def _ple_epilogue(h, p, gp, wpg, wpp, gf):
    gate = _sigmoid(jnp.dot(_rms(h, gp).astype(BF16), wpg, preferred_element_type=F32))
    h = h + gate * jnp.dot(p.astype(BF16), wpp, preferred_element_type=F32)
    return h if gf is None else _rms(h, gf)


def _swiglu_chunks(x, wg_ref, wu_ref, wd_ref, acc, between=None):
    f = wg_ref.shape[-1]
    for c in range(f // FF_CHUNK):
        cols = slice(c * FF_CHUNK, (c + 1) * FF_CHUNK)
        a = jnp.dot(x, wg_ref[0, :, cols], preferred_element_type=F32)
        b = jnp.dot(x, wu_ref[0, :, cols], preferred_element_type=F32)
        mid = (a * _sigmoid(a) * b).astype(BF16)
        part = jnp.dot(mid, wd_ref[0, cols, :], preferred_element_type=F32)
        if c == 0:
            acc[...] = part
        else:
            acc[...] += part
        if between is not None:
            done = next(between)
            acc[0:SUBLANES, 0:LANES] += jnp.where(done != done, done, 0.0)


def _ffn_kernel(final, h_ref, g_ref, wg_ref, wu_ref, wd_ref, p_ref, gp_ref, wpg_ref, wpp_ref, gf_ref,
                o_ref, acc):
    xn = _rms(h_ref[...], g_ref[...]).astype(BF16)
    _swiglu_chunks(xn, wg_ref, wu_ref, wd_ref, acc)
    h = h_ref[...] + acc[...]
    o_ref[...] = _ple_epilogue(h, p_ref[0], gp_ref[...], wpg_ref[...], wpp_ref[...],
                               gf_ref[...] if final else None)


def _resident(shape):
    return pl.BlockSpec(shape, lambda *_: (0,) * len(shape), pipeline_mode=pl.Buffered(1))


def _ffn(h, g, wg, wu, wd, j, p, layer, gp, wpg, wpp, gf, final, tm):
    n, d = h.shape
    f = wg.shape[-1]
    const = lambda i: (0, 0)
    return pl.pallas_call(
        functools.partial(_ffn_kernel, final),
        grid=(n // tm,),
        in_specs=[
            pl.BlockSpec((tm, d), lambda i: (i, 0)),
            pl.BlockSpec((1, d), const),
            pl.BlockSpec((1, d, f), lambda i: (j, 0, 0), pipeline_mode=pl.Buffered(1)),
            pl.BlockSpec((1, d, f), lambda i: (j, 0, 0), pipeline_mode=pl.Buffered(1)),
            pl.BlockSpec((1, f, d), lambda i: (j, 0, 0), pipeline_mode=pl.Buffered(1)),
            pl.BlockSpec((1, tm, p.shape[-1]), lambda i: (layer, i, 0)),
            pl.BlockSpec((1, d), const),
            _resident(wpg.shape),
            _resident(wpp.shape),
            pl.BlockSpec((1, d), const),
        ],
        out_specs=pl.BlockSpec((tm, d), lambda i: (i, 0)),
        out_shape=jax.ShapeDtypeStruct((n, d), F32),
        scratch_shapes=[pltpu.VMEM((tm, d), F32)],
        compiler_params=_params("arbitrary"),
    )(h, g, wg, wu, wd, p, gp, wpg, wpp, gf)


def _mix_ffn_kernel(final, nt,
                    x_ref, ya_ref, ab_ref, gc_ref, gb_ref, hc_ref, hb0_ref, hc0_ref,
                    wb_ref, bb_ref, lg_ref, lb_ref, wc_ref, go_ref, wo_ref,
                    g_ref, wg_ref, wu_ref, wd_ref, p_ref, gp_ref, wpg_ref, wpp_ref, gf_ref,
                    o_ref, nb_ref, nc_ref, cb, cc, sh, h1a, h1b, acc):
    g = pl.program_id(0)
    tiles = pl.num_programs(0) - 1
    tm = x_ref.shape[0]

    @pl.when(g == 0)
    def _():
        h1b[...] = jnp.zeros_like(h1b)

    @pl.when(jnp.minimum(g, tiles - 1) % nt == 0)
    def _():
        cb[0:HIST_B, :] = hb0_ref[0]
        cc[0:HIST_C, :] = hc0_ref[0]

    def step(h_new, h_old):
        mixers = _mix_stages(x_ref, ya_ref, ab_ref, gc_ref, gb_ref, hc_ref, wb_ref, bb_ref, lg_ref, lb_ref,
                             wc_ref, go_ref, wo_ref, cb, cc, sh)
        _swiglu_chunks(_rms(h_old[...], g_ref[...]).astype(BF16), wg_ref, wu_ref, wd_ref, acc, mixers)
        h_new[...] = _finish(mixers)
        o_ref[...] = _ple_epilogue(h_old[...] + acc[...], p_ref[0], gp_ref[...], wpg_ref[...], wpp_ref[...],
                                   gf_ref[...] if final else None)

    @pl.when(g % 2 == 0)
    def _():
        step(h1a, h1b)

    @pl.when(g % 2 == 1)
    def _():
        step(h1b, h1a)

    @pl.when(g < tiles)
    def _():
        _mix_carry(tm, cb, cc, nb_ref, nc_ref)


def _mix_ffn(x, ya, u, hist_b, hist_c, wb, bb, lg, lb, wc, go, wo,
             g, wg, wu, wd, j, p, layer, gp, wpg, wpp, gf, final, batch, t, tm):
    n, d = x.shape
    f = wg.shape[-1]
    nt = t // tm
    tiles = n // tm
    cur = lambda i: jnp.minimum(i, tiles - 1)
    old = lambda i: jnp.maximum(i - 1, 0)
    ab_blk = 3 * A_WIDTH // (2 * B_WIDTH)
    c_blk = (3 * A_WIDTH + 2 * B_WIDTH) // C_WIDTH
    const = lambda i: (0, 0)
    state = lambda rows, width: pl.BlockSpec((1, rows, width), lambda i: (cur(i) // nt, 0, 0))
    weights = lambda: pl.BlockSpec((1,) + wg.shape[1:], lambda i: (j, 0, 0), pipeline_mode=pl.Buffered(1))
    return pl.pallas_call(
        functools.partial(_mix_ffn_kernel, final, nt),
        grid=(tiles + 1,),
        in_specs=[
            pl.BlockSpec((tm, d), lambda i: (cur(i), 0)),
            pl.BlockSpec((tm, A_WIDTH), lambda i: (cur(i), 0)),
            pl.BlockSpec((tm, 2 * B_WIDTH), lambda i: (cur(i), ab_blk)),
            pl.BlockSpec((tm, C_WIDTH), lambda i: (cur(i), c_blk)),
            pl.BlockSpec((tm, C_WIDTH), lambda i: (cur(i), c_blk + 1)),
            pl.BlockSpec((tm, C_WIDTH), lambda i: (cur(i), c_blk + 2)),
            state(HIST_B, B_WIDTH), state(HIST_C, C_WIDTH),
            pl.BlockSpec(wb.shape, const), pl.BlockSpec(bb.shape, const),
            pl.BlockSpec(lg.shape, const), pl.BlockSpec(lb.shape, const),
            pl.BlockSpec(wc.shape, const), pl.BlockSpec(go.shape, const),
            _resident(wo.shape),
            pl.BlockSpec((1, d), const),
            weights(), weights(),
            pl.BlockSpec((1, f, d), lambda i: (j, 0, 0), pipeline_mode=pl.Buffered(1)),
            pl.BlockSpec((1, tm, p.shape[-1]), lambda i: (layer, old(i), 0)),
            pl.BlockSpec((1, d), const),
            _resident(wpg.shape),
            _resident(wpp.shape),
            pl.BlockSpec((1, d), const),
        ],
        out_specs=[pl.BlockSpec((tm, d), lambda i: (old(i), 0)),
                   state(HIST_B, B_WIDTH), state(HIST_C, C_WIDTH)],
        out_shape=[jax.ShapeDtypeStruct((n, d), F32),
                   jax.ShapeDtypeStruct((batch, HIST_B, B_WIDTH), F32),
                   jax.ShapeDtypeStruct((batch, HIST_C, C_WIDTH), F32)],
        scratch_shapes=[pltpu.VMEM((HIST_B + tm, B_WIDTH), F32),
                        pltpu.VMEM((HIST_C + tm, C_WIDTH), F32),
                        pltpu.VMEM((SUBLANES - 1, HIST_B + tm - SUBLANES, B_WIDTH), F32),
                        pltpu.VMEM((tm, d), F32),
                        pltpu.VMEM((tm, d), F32),
                        pltpu.VMEM((tm, d), F32)],
        compiler_params=_params("arbitrary", vmem=FUSED_VMEM_LIMIT),
    )(x, ya, u, u, u, u, hist_b, hist_c, wb, bb, lg, lb, wc, go, wo,
      g, wg, wu, wd, p, gp, wpg, wpp, gf)


def _router_kernel(h_ref, g_ref, wr_ref, br_ref, x3_ref, meta_ref, cnt_ref, carry):
    tm = h_ref.shape[0]

    @pl.when(pl.program_id(0) == 0)
    def _():
        carry[...] = jnp.zeros_like(carry)

    xn = _rms(h_ref[...], g_ref[...])
    for s in range(ROW_SPLIT):
        x3_ref[_chunk(s, tm), :] = xn[:, s * LANES:(s + 1) * LANES]
    logits = lax.dot_general(wr_ref[...], xn, (((1,), (1,)), ((), ())),
                             precision=lax.Precision.HIGHEST,
                             preferred_element_type=F32) + br_ref[...]
    eidx = lax.broadcasted_iota(jnp.int32, (N_EXPERTS, tm), 0)
    m1 = jnp.max(logits, axis=0, keepdims=True)
    i1 = jnp.min(jnp.where(logits == m1, eidx, N_EXPERTS), axis=0, keepdims=True)
    rest = jnp.where(eidx == i1, -jnp.inf, logits)
    m2 = jnp.max(rest, axis=0, keepdims=True)
    i2 = jnp.min(jnp.where(rest == m2, eidx, N_EXPERTS), axis=0, keepdims=True)
    e2 = jnp.exp(m2 - m1)
    g1 = 1.0 / (1.0 + e2)
    g2 = e2 / (1.0 + e2)
    hit1 = eidx == i1
    hit2 = eidx == i2
    chosen = jnp.where(hit1 | hit2, 1.0, 0.0)
    src = lax.broadcasted_iota(jnp.int32, (tm, tm), 0)
    dst = lax.broadcasted_iota(jnp.int32, (tm, tm), 1)
    before = jnp.where(src < dst, 1.0, 0.0).astype(BF16)
    rank = jnp.dot(chosen.astype(BF16), before, preferred_element_type=F32) + carry[:, 0:1]
    r1 = jnp.sum(jnp.where(hit1, rank, 0.0), axis=0, keepdims=True)
    r2 = jnp.sum(jnp.where(hit2, rank, 0.0), axis=0, keepdims=True)
    zero = jnp.zeros_like(g1)
    meta_ref[...] = jnp.concatenate(
        [i1.astype(F32), i2.astype(F32), g1, g2, r1, r2, zero, zero], axis=0)
    carry[...] = carry[...] + jnp.sum(chosen, axis=1, keepdims=True)
    cnt_ref[...] = carry[...]


def _router(h, g, wr_t, br, tm):
    n, d = h.shape
    return pl.pallas_call(
        _router_kernel,
        grid=(n // tm,),
        in_specs=[pl.BlockSpec((tm, d), lambda i: (i, 0)),
                  pl.BlockSpec((1, d), lambda i: (0, 0)),
                  pl.BlockSpec((N_EXPERTS, d), lambda i: (0, 0)),
                  pl.BlockSpec((N_EXPERTS, 1), lambda i: (0, 0))],
        out_specs=[pl.BlockSpec((tm * ROW_SPLIT, LANES), lambda i: (i, 0)),
                   pl.BlockSpec((8, tm), lambda i: (0, i)),
                   pl.BlockSpec((N_EXPERTS, LANES), lambda i: (0, 0))],
        out_shape=[jax.ShapeDtypeStruct((n * ROW_SPLIT, LANES), F32),
                   jax.ShapeDtypeStruct((8, n), F32),
                   jax.ShapeDtypeStruct((N_EXPERTS, LANES), F32)],
        scratch_shapes=[pltpu.VMEM((N_EXPERTS, LANES), F32)],
        compiler_params=_params("arbitrary"),
    )(h, g, wr_t, br)


def _dispatch_kernel(d1_ref, d2_ref, x_ref, o_hbm, sem):
    tm = x_ref.shape[0] // ROW_SPLIT

    for t in range(tm):
        pltpu.make_async_copy(x_ref.at[_row_tile(t)], o_hbm.at[_row_tile(d1_ref[0, 0, t])], sem).start(0)
        pltpu.make_async_copy(x_ref.at[_row_tile(t)], o_hbm.at[_row_tile(d2_ref[0, 0, t])], sem).start(1)
    for _ in range(2):
        pltpu.make_async_copy(x_ref, o_hbm.at[pl.ds(0, tm * ROW_SPLIT)], sem).wait()


def _dispatch(x3, d1, d2, rows_out, tm):
    n = x3.shape[0] // ROW_SPLIT
    steps = d1.shape[0] // tm
    last = n // tm - 1
    idx_spec = pl.BlockSpec((1, 1, tm), lambda i: (i, 0, 0), memory_space=pltpu.SMEM)
    return pl.pallas_call(
        _dispatch_kernel,
        grid=(steps,),
        in_specs=[idx_spec, idx_spec,
                  pl.BlockSpec((tm * ROW_SPLIT, LANES), lambda i: (jnp.minimum(i, last), 0))],
        out_specs=pl.BlockSpec(memory_space=pl.ANY),
        out_shape=jax.ShapeDtypeStruct((rows_out * ROW_SPLIT, LANES), x3.dtype),
        scratch_shapes=[pltpu.SemaphoreType.DMA(())],
        compiler_params=_params("arbitrary"),
    )(d1.reshape(steps, 1, tm), d2.reshape(steps, 1, tm), x3)


def _expert_kernel(texp_ref, nact_ref, x_ref, wg_ref, wu_ref, wd_ref, y_ref, x2, acc):
    del texp_ref

    @pl.when(pl.program_id(0) < nact_ref[0])
    def _():
        for s in range(ROW_SPLIT):
            x2[:, s * LANES:(s + 1) * LANES] = x_ref[_chunk(s, EXPERT_TILE), :].astype(BF16)
        _swiglu_chunks(x2[...], wg_ref, wu_ref, wd_ref, acc)
        for s in range(ROW_SPLIT):
            y_ref[_chunk(s, EXPERT_TILE), :] = acc[:, s * LANES:(s + 1) * LANES]

    @pl.when(pl.program_id(0) >= nact_ref[0])
    def _():
        y_ref[...] = jnp.zeros_like(y_ref)


def _experts(xs, tile_exp, n_active, wg, wu, wd):
    ntile = tile_exp.shape[0]
    d, f = wg.shape[1], wg.shape[2]
    grid_spec = pltpu.PrefetchScalarGridSpec(
        num_scalar_prefetch=2,
        grid=(ntile,),
        in_specs=[
            pl.BlockSpec((EXPERT_TILE * ROW_SPLIT, LANES), lambda t, te, na: (t, 0)),
            pl.BlockSpec((1, d, f), lambda t, te, na: (te[t], 0, 0)),
            pl.BlockSpec((1, d, f), lambda t, te, na: (te[t], 0, 0)),
            pl.BlockSpec((1, f, d), lambda t, te, na: (te[t], 0, 0)),
        ],
        out_specs=pl.BlockSpec((EXPERT_TILE * ROW_SPLIT, LANES), lambda t, te, na: (t, 0)),
        scratch_shapes=[pltpu.VMEM((EXPERT_TILE, d), BF16), pltpu.VMEM((EXPERT_TILE, d), F32)],
    )
    return pl.pallas_call(
        _expert_kernel,
        grid_spec=grid_spec,
        out_shape=jax.ShapeDtypeStruct(xs.shape, F32),
        compiler_params=_params("arbitrary"),
    )(tile_exp, n_active, xs, wg, wu, wd)


def _combine_kernel(final, d1a, d2a, d1b, d2b, d1na, d2na, d1nb, d2nb, ye_hbm, h_ref, gate_ref, p_ref, gp_ref,
                    wpg_ref, wpp_ref, gf_ref, o_ref, a1, a2, b1, b2, na1, na2, nb1, nb2, moe, sems):
    j = pl.program_id(0)
    tm = h_ref.shape[0] // 2

    def gather(d1_ref, d2_ref, y1, y2, sem):
        for t in range(tm):
            pltpu.make_async_copy(ye_hbm.at[_row_tile(d1_ref[0, 0, t])], y1.at[_row_tile(t)], sem).start(0)
            pltpu.make_async_copy(ye_hbm.at[_row_tile(d2_ref[0, 0, t])], y2.at[_row_tile(t)], sem).start(1)

    def wait(y1, y2, sem):
        pltpu.make_async_copy(ye_hbm.at[pl.ds(0, tm * ROW_SPLIT)], y1, sem).wait()
        pltpu.make_async_copy(ye_hbm.at[pl.ds(0, tm * ROW_SPLIT)], y2, sem).wait()

    def combine(y1, y2, rows):
        g1 = gate_ref[rows, 0:1]
        g2 = gate_ref[rows, 1:2]
        for s in range(ROW_SPLIT):
            moe[:, s * LANES:(s + 1) * LANES] = g1 * y1[_chunk(s, tm), :] + g2 * y2[_chunk(s, tm), :]
        h = h_ref[rows, :] + moe[...]
        o_ref[rows, :] = _ple_epilogue(h, p_ref[0, rows, :], gp_ref[...], wpg_ref[...], wpp_ref[...],
                                       gf_ref[...] if final else None)

    def step(cur, nxt, base):
        ca1, ca2, cb1, cb2 = cur
        xa1, xa2, xb1, xb2 = nxt
        wait(ca1, ca2, sems.at[base])
        gather(d1na, d2na, xa1, xa2, sems.at[2 - base])
        combine(ca1, ca2, slice(0, tm))
        wait(cb1, cb2, sems.at[base + 1])
        gather(d1nb, d2nb, xb1, xb2, sems.at[3 - base])
        combine(cb1, cb2, slice(tm, 2 * tm))

        @pl.when(j == pl.num_programs(0) - 1)
        def _():
            wait(xa1, xa2, sems.at[2 - base])
            wait(xb1, xb2, sems.at[3 - base])

    even = (a1, a2, b1, b2)
    odd = (na1, na2, nb1, nb2)

    @pl.when(j == 0)
    def _():
        gather(d1a, d2a, a1, a2, sems.at[0])
        gather(d1b, d2b, b1, b2, sems.at[1])

    @pl.when(j % 2 == 0)
    def _():
        step(even, odd, 0)

    @pl.when(j % 2 == 1)
    def _():
        step(odd, even, 2)


def _combine(ye, d1, d2, h, gates, p, layer, gp, wpg, wpp, gf, final, tm):
    n, d = h.shape
    steps = n // (2 * tm)
    tiles = n // tm
    idx_spec = lambda f: pl.BlockSpec((1, 1, tm), lambda i: (jnp.minimum(f(i), tiles - 1), 0, 0),
                                      memory_space=pltpu.SMEM)
    tile_a, tile_b = idx_spec(lambda i: 2 * i), idx_spec(lambda i: 2 * i + 1)
    next_a, next_b = idx_spec(lambda i: 2 * i + 2), idx_spec(lambda i: 2 * i + 3)
    const = lambda i: (0, 0)
    d1 = d1.reshape(tiles, 1, tm)
    d2 = d2.reshape(tiles, 1, tm)
    row_buf = pltpu.VMEM((tm * ROW_SPLIT, LANES), F32)
    return pl.pallas_call(
        functools.partial(_combine_kernel, final),
        grid=(steps,),
        in_specs=[tile_a, tile_a, tile_b, tile_b, next_a, next_a, next_b, next_b,
                  pl.BlockSpec(memory_space=pl.ANY),
                  pl.BlockSpec((2 * tm, d), lambda i: (i, 0)),
                  pl.BlockSpec((2 * tm, 2), lambda i: (i, 0)),
                  pl.BlockSpec((1, 2 * tm, p.shape[-1]), lambda i: (layer, i, 0)),
                  pl.BlockSpec((1, d), const),
                  _resident(wpg.shape),
                  _resident(wpp.shape),
                  pl.BlockSpec((1, d), const)],
        out_specs=pl.BlockSpec((2 * tm, d), lambda i: (i, 0)),
        out_shape=jax.ShapeDtypeStruct((n, d), F32),
        scratch_shapes=[row_buf] * 8 + [pltpu.VMEM((tm, d), F32), pltpu.SemaphoreType.DMA((4,))],
        compiler_params=_params("arbitrary"),
    )(d1, d2, d1, d2, d1, d2, d1, d2, ye, h, gates, p, gp, wpg, wpp, gf)


def _moe(h, g, wr_t, br, wg, wu, wd, p, layer, gp, wpg, wpp, gf, final, tm):
    n = h.shape[0]
    x3, meta, cnt = _router(h, g, wr_t, br, tm)
    counts = cnt[:, 0].astype(jnp.int32)
    tiles = (counts + EXPERT_TILE - 1) // EXPERT_TILE
    padded = tiles * EXPERT_TILE
    ends = jnp.cumsum(padded)
    offs = ends - padded
    expert = jnp.arange(N_EXPERTS, dtype=jnp.int32)[:, None]
    start_of = lambda idx: jnp.sum(jnp.where(idx[None, :] == expert, offs[:, None], 0), axis=0)
    d1 = start_of(meta[0].astype(jnp.int32)) + meta[4].astype(jnp.int32)
    d2 = start_of(meta[1].astype(jnp.int32)) + meta[5].astype(jnp.int32)
    gates = jnp.stack([meta[2], meta[3]], axis=1)
    ntile = 2 * n // EXPERT_TILE + N_EXPERTS
    r = jnp.arange(EXPERT_TILE, dtype=jnp.int32)[None, :]
    in_group = counts[:, None] + r < padded[:, None]
    spare = ends[-1] + (jnp.cumsum(jnp.where(in_group, 0, 1).reshape(-1)) - 1).reshape(in_group.shape)
    fill = jnp.where(in_group, offs[:, None] + counts[:, None] + r, spare).reshape(-1).astype(jnp.int32)
    half = fill.shape[0] // 2
    xs = _dispatch(x3, jnp.concatenate([d1, fill[:half]]), jnp.concatenate([d2, fill[half:]]),
                   ntile * EXPERT_TILE, tm)
    n_active = jnp.sum(tiles).astype(jnp.int32)
    tile_start = jnp.minimum(jnp.arange(ntile, dtype=jnp.int32), n_active - 1) * EXPERT_TILE
    tile_exp = jnp.minimum(jnp.sum(jnp.where(tile_start[:, None] >= ends[None, :], 1, 0), axis=1),
                           N_EXPERTS - 1).astype(jnp.int32)
    ye = _experts(xs, tile_exp, n_active.reshape(1), wg, wu, wd)
    return _combine(ye, d1, d2, h, gates, p, layer, gp, wpg, wpp, gf, final, min(tm, n // 2))


def _trunk(x, p, past, prm, tok_tile, mix_tile):
    batch, t, d = x.shape
    n = batch * t
    depth = prm["w_in"].shape[0]
    h = x.reshape(n, d)
    new_k, new_v, new_b, new_c = [], [], [], []
    row = lambda a: a.reshape(1, -1)
    p = p.reshape(depth, n, -1)
    keep = min(BAND, t) if past is None else t
    for i in range(depth):
        u, kv = _inproj(h, row(prm["norm_mix"][i]), prm["w_in"][i], tok_tile, t, keep)
        new_k.append(kv[:, :A_WIDTH].reshape(batch, keep, HEADS, HEAD_DIM))
        new_v.append(kv[:, A_WIDTH:].reshape(batch, keep, HEADS, HEAD_DIM))
        if past is None:
            ya = _prompt_attention(u, _prompt_bias_table(prm["rel_bias"][i]), batch, t)
            hist_b = jnp.zeros((batch, HIST_B, B_WIDTH), F32)
            hist_c = jnp.zeros((batch, HIST_C, C_WIDTH), F32)
        else:
            ck, cv, sb, sc = past
            n_past = ck.shape[2]
            bias_c, bias_n = _sample_bias_tables(prm["rel_bias"][i], t, n_past)
            ya = _sample_attention(u, ck[i].reshape(batch, n_past, A_WIDTH),
                                   cv[i].reshape(batch, n_past, A_WIDTH), bias_c, bias_n, batch, t)
            hist_b = jnp.pad(sb[i], ((0, 0), (HIST_B - (B_KERNEL - 1), 0), (0, 0)))
            hist_c = jnp.pad(sc[i], ((0, 0), (HIST_C - (C_KERNEL - 1), 0), (0, 0)))
        wb = jnp.pad(prm["w_dw_b"][i], ((0, HIST_B - B_KERNEL), (0, 0)))
        wc = jnp.pad(prm["w_dw_c"][i], ((0, HIST_C - C_KERNEL), (0, 0)))
        mix_args = (h, ya, u, hist_b, hist_c, wb, row(prm["b_dw_b"][i]), row(prm["ln_g_b"][i]),
                    row(prm["ln_b_b"][i]), wc, row(prm["g_out"][i]), prm["w_out"][i])
        final = i == depth - 1
        tail = (p, i, row(prm["norm_ple"][i]), prm["w_ple_gate"][i], prm["w_ple_proj"][i],
                row(prm["norm_final"]), final)
        j = i // 2
        ffn_args = (row(prm["norm_ffn"][i]), prm["w_ff_gate"], prm["w_ff_up"], prm["w_ff_down"], j)
        if i % 2 == 0 and tok_tile == mix_tile:
            h, nb, nc = _mix_ffn(*mix_args, *ffn_args, *tail, batch, t, mix_tile)
        else:
            h, nb, nc = _mix(*mix_args, batch, t, mix_tile)
            if i % 2 == 0:
                h = _ffn(h, *ffn_args, *tail, tok_tile)
            else:
                h = _moe(h, row(prm["norm_ffn"][i]), prm["w_router"][j].T, prm["b_router"][j].reshape(-1, 1),
                         prm["w_ex_gate"][j], prm["w_ex_up"][j], prm["w_ex_down"][j], *tail, tok_tile)
        new_b.append(nb[:, HIST_B - (B_KERNEL - 1):])
        new_c.append(nc[:, HIST_C - (C_KERNEL - 1):])
    return (h.reshape(batch, t, d), jnp.stack(new_k), jnp.stack(new_v), jnp.stack(new_b), jnp.stack(new_c))


def kernel(x_prompt, x_sample, p_prompt, p_sample, cache_attn_k, cache_attn_v, state_conv_b, state_conv_c,
           w_in, rel_bias, w_dw_b, b_dw_b, ln_g_b, ln_b_b, w_dw_c, g_out, w_out, norm_mix, norm_ffn,
           w_ff_gate, w_ff_up, w_ff_down, w_router, b_router, w_ex_gate, w_ex_up, w_ex_down,
           norm_ple, w_ple_gate, w_ple_proj, norm_final):
    prm = dict(
        w_in=w_in.astype(BF16), rel_bias=rel_bias, w_dw_b=w_dw_b, b_dw_b=b_dw_b, ln_g_b=ln_g_b,
        ln_b_b=ln_b_b, w_dw_c=w_dw_c, g_out=g_out, w_out=w_out.astype(BF16), norm_mix=norm_mix,
        norm_ffn=norm_ffn, w_ff_gate=w_ff_gate.astype(BF16), w_ff_up=w_ff_up.astype(BF16),
        w_ff_down=w_ff_down.astype(BF16), w_router=w_router, b_router=b_router,
        w_ex_gate=w_ex_gate.astype(BF16), w_ex_up=w_ex_up.astype(BF16), w_ex_down=w_ex_down.astype(BF16),
        norm_ple=norm_ple, w_ple_gate=w_ple_gate.astype(BF16), w_ple_proj=w_ple_proj.astype(BF16),
        norm_final=norm_final)
    dec_t = x_sample.shape[1]
    y_p, k_p, v_p, b_p, c_p = _trunk(x_prompt, p_prompt, None, prm, 512, 512)
    y_s, k_s, v_s, b_s, c_s = _trunk(x_sample, p_sample,
                                     (cache_attn_k, cache_attn_v, state_conv_b, state_conv_c),
                                     prm, x_sample.shape[0] * dec_t, dec_t)
    return (y_p, y_s, k_p, v_p, b_p, c_p, k_s, v_s, b_s, c_s)
```

```python
import functools

import jax
import jax.numpy as jnp
from jax import lax
from jax.experimental import pallas as pl
from jax.experimental.pallas import tpu as pltpu

F32 = jnp.float32
BF16 = jnp.bfloat16

CHUNK = 64
BAND = 512
HEADS = 8
HEAD_DIM = 64
A_WIDTH = HEADS * HEAD_DIM
B_WIDTH = 256
C_WIDTH = 256
B_KERNEL = 31
C_KERNEL = 3
REL_CLIP = 128
N_EXPERTS = 8
EPS = 1e-6
NEG = -1e30
LOG2E = 1.4426950408889634

LANES = 128
SUBLANES = 8
ROW_SPLIT = 8
QTILE = 512
PAIR_ROWS = 2 * CHUNK
WIN = BAND + PAIR_ROWS
HIST_B = 32
HIST_C = 8
EXPERT_TILE = 512
DISPATCH_TILE = 1024
FF_CHUNK = 512
MIX_SHIFT_BREAKS = (3, 7)
MIX_TAP_BREAKS = (7, 15, 23, 30)
VMEM_LIMIT = 56 * 1024 * 1024
FUSED_VMEM_LIMIT = 60 * 1024 * 1024


def _params(*sem, vmem=VMEM_LIMIT):
    return pltpu.CompilerParams(dimension_semantics=sem, vmem_limit_bytes=vmem)


def _rms(x, g):
    return x * lax.rsqrt(jnp.mean(x * x, axis=-1, keepdims=True) + EPS) * g


def _sigmoid(x):
    return 1.0 / (1.0 + jnp.exp(-x))


def _chunk(s, n):
    return pl.ds(s, n, stride=ROW_SPLIT)


def _row_tile(r):
    return pl.ds(pl.multiple_of(r * ROW_SPLIT, ROW_SPLIT), ROW_SPLIT)


def _inproj_kernel(x_ref, g_ref, w_ref, u_ref, kv_ref):
    hn = _rms(x_ref[...], g_ref[...]).astype(BF16)
    q = jnp.dot(hn, w_ref[:, :A_WIDTH], preferred_element_type=F32)
    u_ref[:, :A_WIDTH] = (q * (HEAD_DIM ** -0.5 * LOG2E)).astype(BF16)
    kv = jnp.dot(hn, w_ref[:, A_WIDTH:3 * A_WIDTH], preferred_element_type=F32)
    kv_ref[...] = kv
    u_ref[:, A_WIDTH:3 * A_WIDTH] = kv.astype(BF16)
    rest = jnp.dot(hn, w_ref[:, 3 * A_WIDTH:], preferred_element_type=F32)
    u_ref[:, 3 * A_WIDTH:] = rest.astype(BF16)


def _inproj(x, g, w, tm, t, keep):
    n, d = x.shape
    wi = w.shape[1]
    if keep == t:
        kv_index = lambda i: (i, 0)
    else:
        nt, ktiles = t // tm, keep // tm
        kv_index = lambda i: ((i // nt) * ktiles + jnp.maximum(i % nt - (nt - ktiles), 0), 0)
    return pl.pallas_call(
        _inproj_kernel,
        grid=(n // tm,),
        in_specs=[pl.BlockSpec((tm, d), lambda i: (i, 0)),
                  pl.BlockSpec((1, d), lambda i: (0, 0)),
                  pl.BlockSpec((d, wi), lambda i: (0, 0))],
        out_specs=[pl.BlockSpec((tm, wi), lambda i: (i, 0)),
                   pl.BlockSpec((tm, 2 * A_WIDTH), kv_index)],
        out_shape=[jax.ShapeDtypeStruct((n, wi), BF16),
                   jax.ShapeDtypeStruct((n // t * keep, 2 * A_WIDTH), F32)],
        compiler_params=_params("arbitrary"),
    )(x, g, w)


def _attn_kernel(q_ref, kp_ref, kc_ref, vp_ref, vc_ref, bias_ref, o_ref, k_scr, vt_scr, s_scr):
    i = pl.program_id(1)
    k_scr[0:QTILE, :] = kp_ref[...]
    k_scr[QTILE:2 * QTILE, :] = kc_ref[...]
    vt_scr[:, 0:QTILE] = vp_ref[...].astype(F32).T.astype(BF16)
    vt_scr[:, QTILE:2 * QTILE] = vc_ref[...].astype(F32).T.astype(BF16)
    lane = lax.broadcasted_iota(jnp.int32, (PAIR_ROWS, LANES), 1)
    row = lax.broadcasted_iota(jnp.int32, (WIN, 2 * PAIR_ROWS), 0)
    nblk = QTILE // PAIR_ROWS
    npair = HEADS // 2

    def scores(p):
        cols = slice(p * LANES, (p + 1) * LANES)
        for c2 in range(nblk):
            r0 = c2 * PAIR_ROWS
            kwin = k_scr[r0:r0 + WIN, cols]
            q2 = q_ref[r0:r0 + PAIR_ROWS, cols]
            zero = jnp.zeros_like(q2)
            qq = jnp.concatenate([jnp.where(lane < HEAD_DIM, q2, zero),
                                  jnp.where(lane >= HEAD_DIM, q2, zero)], axis=0)
            s_scr[(p % 2) * nblk + c2] = lax.dot_general(
                kwin, qq, (((1,), (1,)), ((), ())), preferred_element_type=F32)

    def tile(first):
        scores(0)
        for p in range(npair):
            if p + 1 < npair:
                scores(p + 1)
            cols = slice(p * LANES, (p + 1) * LANES)
            for c2 in range(nblk):
                r0 = c2 * PAIR_ROWS
                vwin = vt_scr[cols, r0:r0 + WIN]
                s = s_scr[(p % 2) * nblk + c2] + bias_ref[p]
                if first:
                    s = jnp.where(row < BAND - r0, NEG, s)
                m = jnp.max(s, axis=0, keepdims=True)
                e = jnp.exp2(s - m)
                l = jnp.sum(e, axis=0, keepdims=True)
                ot = jnp.dot(vwin, e.astype(BF16), preferred_element_type=F32)
                ot = ot * (1.0 / l)
                o = jnp.concatenate([ot[:HEAD_DIM, :PAIR_ROWS], ot[HEAD_DIM:, PAIR_ROWS:]], axis=0).T
                o_ref[r0:r0 + PAIR_ROWS, cols] = o.astype(o_ref.dtype)

    @pl.when(i == 0)
    def _():
        tile(True)

    @pl.when(i > 0)
    def _():
        tile(False)


def _prompt_attention(u, bias_t, batch, seq):
    nt = seq // QTILE
    prev = lambda b, i: b * nt + jnp.maximum(i - 1, 0)
    return pl.pallas_call(
        _attn_kernel,
        grid=(batch, nt),
        in_specs=[
            pl.BlockSpec((QTILE, A_WIDTH), lambda b, i: (b * nt + i, 0)),
            pl.BlockSpec((QTILE, A_WIDTH), lambda b, i: (prev(b, i), 1)),
            pl.BlockSpec((QTILE, A_WIDTH), lambda b, i: (b * nt + i, 1)),
            pl.BlockSpec((QTILE, A_WIDTH), lambda b, i: (prev(b, i), 2)),
            pl.BlockSpec((QTILE, A_WIDTH), lambda b, i: (b * nt + i, 2)),
            pl.BlockSpec((HEADS // 2, WIN, 2 * PAIR_ROWS), lambda b, i: (0, 0, 0)),
        ],
        out_specs=pl.BlockSpec((QTILE, A_WIDTH), lambda b, i: (b * nt + i, 0)),
        out_shape=jax.ShapeDtypeStruct((batch * seq, A_WIDTH), BF16),
        scratch_shapes=[pltpu.VMEM((2 * QTILE, A_WIDTH), BF16),
                        pltpu.VMEM((A_WIDTH, 2 * QTILE), BF16),
                        pltpu.VMEM((2 * QTILE // PAIR_ROWS, WIN, 2 * PAIR_ROWS), F32)],
        compiler_params=_params("arbitrary", "arbitrary"),
    )(u, u, u, u, u, bias_t)


def _hankel(v, rows, cols):
    heads, length = v.shape
    flat = jnp.tile(v, (1, rows + 1))[:, :rows * (length + 1)]
    return flat.reshape(heads, rows, length + 1)[:, :, :cols]


def _prompt_bias_table(rel_bias):
    by_dist = jnp.concatenate(
        [jnp.broadcast_to(rel_bias[:, :1], (HEADS, BAND)), rel_bias[:, :2 * REL_CLIP]], axis=1).astype(F32)
    by_sum = _hankel(by_dist, PAIR_ROWS, WIN + 1)
    table = jnp.transpose(by_sum[:, ::-1, 1:], (0, 2, 1))
    j = jnp.arange(WIN, dtype=jnp.int32)[:, None]
    lo = (jnp.arange(PAIR_ROWS, dtype=jnp.int32)[None, :] // CHUNK) * CHUNK
    valid = (j >= lo) & (j < lo + BAND + CHUNK)
    table = jnp.where(valid[None], table * LOG2E, NEG)
    table = table.reshape(HEADS // 2, 2, WIN, PAIR_ROWS)
    return jnp.transpose(table, (0, 2, 1, 3)).reshape(HEADS // 2, WIN, 2 * PAIR_ROWS)


def _sample_attn_kernel(q_ref, kn_ref, vn_ref, ck_ref, cv_ref, bc_ref, bn_ref, o_ref):
    t = q_ref.shape[0]
    lane = lax.broadcasted_iota(jnp.int32, (t, LANES), 1)
    kc = ck_ref[0].astype(BF16)
    vc = cv_ref[0].astype(BF16)
    for p in range(HEADS // 2):
        cols = slice(p * LANES, (p + 1) * LANES)
        q2 = q_ref[:, cols]
        acc = None
        for hh in range(2):
            h = 2 * p + hh
            sel = (lane >= HEAD_DIM) == bool(hh)
            qm = jnp.where(sel, q2, jnp.zeros_like(q2))
            nt_dims = (((1,), (1,)), ((), ()))
            s_c = lax.dot_general(qm, kc[:, cols], nt_dims, preferred_element_type=F32) + bc_ref[h]
            s_n = lax.dot_general(qm, kn_ref[:, cols], nt_dims, preferred_element_type=F32) + bn_ref[h]
            m = jnp.maximum(jnp.max(s_c, axis=-1, keepdims=True), jnp.max(s_n, axis=-1, keepdims=True))
            e_c = jnp.exp2(s_c - m)
            e_n = jnp.exp2(s_n - m)
            l = jnp.sum(e_c, axis=-1, keepdims=True) + jnp.sum(e_n, axis=-1, keepdims=True)
            pv = (jnp.dot(e_c.astype(BF16), vc[:, cols], preferred_element_type=F32)
                  + jnp.dot(e_n.astype(BF16), vn_ref[:, cols], preferred_element_type=F32)) * (1.0 / l)
            acc = pv if acc is None else jnp.where(sel, pv, acc)
        o_ref[:, cols] = acc.astype(o_ref.dtype)


def _sample_attention(u, cache_k, cache_v, bias_c, bias_n, batch, t):
    n_past = cache_k.shape[1]
    return pl.pallas_call(
        _sample_attn_kernel,
        grid=(batch,),
        in_specs=[
            pl.BlockSpec((t, A_WIDTH), lambda b: (b, 0)),
            pl.BlockSpec((t, A_WIDTH), lambda b: (b, 1)),
            pl.BlockSpec((t, A_WIDTH), lambda b: (b, 2)),
            pl.BlockSpec((1, n_past, A_WIDTH), lambda b: (b, 0, 0)),
            pl.BlockSpec((1, n_past, A_WIDTH), lambda b: (b, 0, 0)),
            pl.BlockSpec((HEADS, t, n_past), lambda b: (0, 0, 0)),
            pl.BlockSpec((HEADS, t, t), lambda b: (0, 0, 0)),
        ],
        out_specs=pl.BlockSpec((t, A_WIDTH), lambda b: (b, 0)),
        out_shape=jax.ShapeDtypeStruct((batch * t, A_WIDTH), BF16),
        compiler_params=_params("arbitrary"),
    )(u, u, u, cache_k, cache_v, bias_c, bias_n)


def _sample_bias_tables(rel_bias, t, n_past):
    def by_dist(lo, hi):
        dist = jnp.arange(lo, hi + 1, dtype=jnp.int32)
        return rel_bias[:, jnp.clip(dist, -REL_CLIP, REL_CLIP) + REL_CLIP].astype(F32) * LOG2E

    cache = by_dist(-n_past - t + 1, -1)
    new = by_dist(-t + 1, t - 1)
    return _hankel(cache, t, n_past)[:, ::-1], _hankel(new, t, t)[:, ::-1]


def _mix_stages(x_ref, ya_ref, ab_ref, gc_ref, gb_ref, hc_ref, wb_ref, bb_ref, lg_ref, lb_ref, wc_ref, go_ref,
                wo_ref, cb, cc, sh):
    tm = x_ref.shape[0]
    ab = ab_ref[...].astype(F32)
    glu = ab[:, :B_WIDTH] * _sigmoid(ab[:, B_WIDTH:])
    pre = gc_ref[...].astype(F32) * hc_ref[...].astype(F32)
    cb[HIST_B:HIST_B + tm, :] = glu
    cc[HIST_C:HIST_C + tm, :] = pre
    yield _token(glu) + _token(pre)
    span = tm + HIST_B - SUBLANES
    moved = 0.0
    for r in range(1, SUBLANES):
        shifted = cb[pl.ds(r, span), :]
        sh[r - 1, 0:span, :] = shifted
        moved = moved + _token(shifted)
        if r in MIX_SHIFT_BREAKS:
            yield moved
            moved = 0.0
    z = jnp.zeros((tm, B_WIDTH), F32) + bb_ref[...]
    for j in range(B_KERNEL):
        m = j + HIST_B - (B_KERNEL - 1)
        a = m // SUBLANES * SUBLANES
        src = cb[a:a + tm, :] if m == a else sh[m - a - 1, a:a + tm, :]
        z = z + wb_ref[j:j + 1, :] * src
        if j in MIX_TAP_BREAKS:
            yield _token(z)
    mu = jnp.mean(z, axis=-1, keepdims=True)
    zc = z - mu
    var = jnp.mean(zc * zc, axis=-1, keepdims=True)
    yb = zc * lax.rsqrt(var + EPS) * lg_ref[...] + lb_ref[...]
    yb = yb * _sigmoid(yb)

    conv = jnp.zeros((tm, C_WIDTH), F32)
    for j in range(C_KERNEL):
        conv = conv + wc_ref[j:j + 1, :] * cc[pl.ds(j + HIST_C - (C_KERNEL - 1), tm), :]
    yc = gb_ref[...].astype(F32) * conv
    yield _token(yb) + _token(yc)

    go = go_ref[...]
    na = _rms(ya_ref[...].astype(F32), go[:, :A_WIDTH]).astype(BF16)
    nb = _rms(yb, go[:, A_WIDTH:A_WIDTH + B_WIDTH]).astype(BF16)
    nc = _rms(yc, go[:, A_WIDTH + B_WIDTH:]).astype(BF16)
    return (x_ref[...]
            + jnp.dot(na, wo_ref[0:A_WIDTH, :], preferred_element_type=F32)
            + jnp.dot(nb, wo_ref[A_WIDTH:A_WIDTH + B_WIDTH, :], preferred_element_type=F32)
            + jnp.dot(nc, wo_ref[A_WIDTH + B_WIDTH:, :], preferred_element_type=F32))


def _token(v):
    t = jnp.sum(v, axis=0, keepdims=True)
    return sum(t[:, c:c + LANES] for c in range(0, v.shape[1], LANES))


def _finish(stages):
    while True:
        try:
            next(stages)
        except StopIteration as done:
            return done.value


def _mix_carry(tm, cb, cc, nb_ref, nc_ref):
    tail_b = cb[tm:tm + HIST_B, :]
    tail_c = cc[tm:tm + HIST_C, :]
    nb_ref[0] = tail_b
    nc_ref[0] = tail_c
    cb[0:HIST_B, :] = tail_b
    cc[0:HIST_C, :] = tail_c


def _mix_kernel(x_ref, ya_ref, ab_ref, gc_ref, gb_ref, hc_ref, hb0_ref, hc0_ref,
                wb_ref, bb_ref, lg_ref, lb_ref, wc_ref, go_ref, wo_ref,
                h_ref, nb_ref, nc_ref, cb, cc, sh):
    @pl.when(pl.program_id(1) == 0)
    def _():
        cb[0:HIST_B, :] = hb0_ref[0]
        cc[0:HIST_C, :] = hc0_ref[0]

    h_ref[...] = _finish(_mix_stages(x_ref, ya_ref, ab_ref, gc_ref, gb_ref, hc_ref, wb_ref, bb_ref, lg_ref,
                                     lb_ref, wc_ref, go_ref, wo_ref, cb, cc, sh))
    _mix_carry(x_ref.shape[0], cb, cc, nb_ref, nc_ref)


def _mix(x, ya, u, hist_b, hist_c, wb, bb, lg, lb, wc, go, wo, batch, t, tm):
    n, d = x.shape
    nt = t // tm
    row = lambda b, i: b * nt + i
    ab_blk = 3 * A_WIDTH // (2 * B_WIDTH)
    c_blk = (3 * A_WIDTH + 2 * B_WIDTH) // C_WIDTH
    const = lambda b, i: (0, 0)
    return pl.pallas_call(
        _mix_kernel,
        grid=(batch, nt),
        in_specs=[
            pl.BlockSpec((tm, d), lambda b, i: (row(b, i), 0)),
            pl.BlockSpec((tm, A_WIDTH), lambda b, i: (row(b, i), 0)),
            pl.BlockSpec((tm, 2 * B_WIDTH), lambda b, i: (row(b, i), ab_blk)),
            pl.BlockSpec((tm, C_WIDTH), lambda b, i: (row(b, i), c_blk)),
            pl.BlockSpec((tm, C_WIDTH), lambda b, i: (row(b, i), c_blk + 1)),
            pl.BlockSpec((tm, C_WIDTH), lambda b, i: (row(b, i), c_blk + 2)),
            pl.BlockSpec((1, HIST_B, B_WIDTH), lambda b, i: (b, 0, 0)),
            pl.BlockSpec((1, HIST_C, C_WIDTH), lambda b, i: (b, 0, 0)),
            pl.BlockSpec(wb.shape, const), pl.BlockSpec(bb.shape, const),
            pl.BlockSpec(lg.shape, const), pl.BlockSpec(lb.shape, const),
            pl.BlockSpec(wc.shape, const), pl.BlockSpec(go.shape, const),
            pl.BlockSpec(wo.shape, const),
        ],
        out_specs=[pl.BlockSpec((tm, d), lambda b, i: (row(b, i), 0)),
                   pl.BlockSpec((1, HIST_B, B_WIDTH), lambda b, i: (b, 0, 0)),
                   pl.BlockSpec((1, HIST_C, C_WIDTH), lambda b, i: (b, 0, 0))],
        out_shape=[jax.ShapeDtypeStruct((n, d), F32),
                   jax.ShapeDtypeStruct((batch, HIST_B, B_WIDTH), F32),
                   jax.ShapeDtypeStruct((batch, HIST_C, C_WIDTH), F32)],
        scratch_shapes=[pltpu.VMEM((HIST_B + tm, B_WIDTH), F32),
                        pltpu.VMEM((HIST_C + tm, C_WIDTH), F32),
                        pltpu.VMEM((SUBLANES - 1, HIST_B + tm - SUBLANES, B_WIDTH), F32)],
        compiler_params=_params("arbitrary", "arbitrary"),
    )(x, ya, u, u, u, u, hist_b, hist_c, wb, bb, lg, lb, wc, go, wo)


def _ple_epilogue(h, p, gp, wpg, wpp, gf):
    gate = _sigmoid(jnp.dot(_rms(h, gp).astype(BF16), wpg, preferred_element_type=F32))
    h = h + gate * jnp.dot(p.astype(BF16), wpp, preferred_element_type=F32)
    return h if gf is None else _rms(h, gf)


def _swiglu_chunks(x, wg_ref, wu_ref, wd_ref, acc, between=None):
    f = wg_ref.shape[-1]
    for c in range(f // FF_CHUNK):
        cols = slice(c * FF_CHUNK, (c + 1) * FF_CHUNK)
        a = jnp.dot(x, wg_ref[0, :, cols], preferred_element_type=F32)
        b = jnp.dot(x, wu_ref[0, :, cols], preferred_element_type=F32)
        mid = (a * _sigmoid(a) * b).astype(BF16)
        part = jnp.dot(mid, wd_ref[0, cols, :], preferred_element_type=F32)
        if c == 0:
            acc[...] = part
        else:
            acc[...] += part
        if between is not None:
            done = next(between)
            acc[0:SUBLANES, 0:LANES] += jnp.where(done != done, done, 0.0)


def _ffn_kernel(final, h_ref, g_ref, wg_ref, wu_ref, wd_ref, p_ref, gp_ref, wpg_ref, wpp_ref, gf_ref,
                o_ref, acc):
    xn = _rms(h_ref[...], g_ref[...]).astype(BF16)
    _swiglu_chunks(xn, wg_ref, wu_ref, wd_ref, acc)
    h = h_ref[...] + acc[...]
    o_ref[...] = _ple_epilogue(h, p_ref[0], gp_ref[...], wpg_ref[...], wpp_ref[...],
                               gf_ref[...] if final else None)


def _resident(shape):
    return pl.BlockSpec(shape, lambda *_: (0,) * len(shape), pipeline_mode=pl.Buffered(1))


def _ffn(h, g, wg, wu, wd, j, p, layer, gp, wpg, wpp, gf, final, tm):
    n, d = h.shape
    f = wg.shape[-1]
    const = lambda i: (0, 0)
    return pl.pallas_call(
        functools.partial(_ffn_kernel, final),
        grid=(n // tm,),
        in_specs=[
            pl.BlockSpec((tm, d), lambda i: (i, 0)),
            pl.BlockSpec((1, d), const),
            pl.BlockSpec((1, d, f), lambda i: (j, 0, 0), pipeline_mode=pl.Buffered(1)),
            pl.BlockSpec((1, d, f), lambda i: (j, 0, 0), pipeline_mode=pl.Buffered(1)),
            pl.BlockSpec((1, f, d), lambda i: (j, 0, 0), pipeline_mode=pl.Buffered(1)),
            pl.BlockSpec((1, tm, p.shape[-1]), lambda i: (layer, i, 0)),
            pl.BlockSpec((1, d), const),
            _resident(wpg.shape),
            _resident(wpp.shape),
            pl.BlockSpec((1, d), const),
        ],
        out_specs=pl.BlockSpec((tm, d), lambda i: (i, 0)),
        out_shape=jax.ShapeDtypeStruct((n, d), F32),
        scratch_shapes=[pltpu.VMEM((tm, d), F32)],
        compiler_params=_params("arbitrary"),
    )(h, g, wg, wu, wd, p, gp, wpg, wpp, gf)


def _mix_ffn_kernel(final, nt,
                    x_ref, ya_ref, ab_ref, gc_ref, gb_ref, hc_ref, hb0_ref, hc0_ref,
                    wb_ref, bb_ref, lg_ref, lb_ref, wc_ref, go_ref, wo_ref,
                    g_ref, wg_ref, wu_ref, wd_ref, p_ref, gp_ref, wpg_ref, wpp_ref, gf_ref,
                    o_ref, nb_ref, nc_ref, cb, cc, sh, h1a, h1b, acc):
    g = pl.program_id(0)
    tiles = pl.num_programs(0) - 1
    tm = x_ref.shape[0]

    @pl.when(g == 0)
    def _():
        h1b[...] = jnp.zeros_like(h1b)

    @pl.when(jnp.minimum(g, tiles - 1) % nt == 0)
    def _():
        cb[0:HIST_B, :] = hb0_ref[0]
        cc[0:HIST_C, :] = hc0_ref[0]

    def step(h_new, h_old):
        mixers = _mix_stages(x_ref, ya_ref, ab_ref, gc_ref, gb_ref, hc_ref, wb_ref, bb_ref, lg_ref, lb_ref,
                             wc_ref, go_ref, wo_ref, cb, cc, sh)
        _swiglu_chunks(_rms(h_old[...], g_ref[...]).astype(BF16), wg_ref, wu_ref, wd_ref, acc, mixers)
        h_new[...] = _finish(mixers)
        o_ref[...] = _ple_epilogue(h_old[...] + acc[...], p_ref[0], gp_ref[...], wpg_ref[...], wpp_ref[...],
                                   gf_ref[...] if final else None)

    @pl.when(g % 2 == 0)
    def _():
        step(h1a, h1b)

    @pl.when(g % 2 == 1)
    def _():
        step(h1b, h1a)

    @pl.when(g < tiles)
    def _():
        _mix_carry(tm, cb, cc, nb_ref, nc_ref)


def _mix_ffn(x, ya, u, hist_b, hist_c, wb, bb, lg, lb, wc, go, wo,
             g, wg, wu, wd, j, p, layer, gp, wpg, wpp, gf, final, batch, t, tm):
    n, d = x.shape
    f = wg.shape[-1]
    nt = t // tm
    tiles = n // tm
    cur = lambda i: jnp.minimum(i, tiles - 1)
    old = lambda i: jnp.maximum(i - 1, 0)
    ab_blk = 3 * A_WIDTH // (2 * B_WIDTH)
    c_blk = (3 * A_WIDTH + 2 * B_WIDTH) // C_WIDTH
    const = lambda i: (0, 0)
    state = lambda rows, width: pl.BlockSpec((1, rows, width), lambda i: (cur(i) // nt, 0, 0))
    weights = lambda: pl.BlockSpec((1,) + wg.shape[1:], lambda i: (j, 0, 0), pipeline_mode=pl.Buffered(1))
    return pl.pallas_call(
        functools.partial(_mix_ffn_kernel, final, nt),
        grid=(tiles + 1,),
        in_specs=[
            pl.BlockSpec((tm, d), lambda i: (cur(i), 0)),
            pl.BlockSpec((tm, A_WIDTH), lambda i: (cur(i), 0)),
            pl.BlockSpec((tm, 2 * B_WIDTH), lambda i: (cur(i), ab_blk)),
            pl.BlockSpec((tm, C_WIDTH), lambda i: (cur(i), c_blk)),
            pl.BlockSpec((tm, C_WIDTH), lambda i: (cur(i), c_blk + 1)),
            pl.BlockSpec((tm, C_WIDTH), lambda i: (cur(i), c_blk + 2)),
            state(HIST_B, B_WIDTH), state(HIST_C, C_WIDTH),
            pl.BlockSpec(wb.shape, const), pl.BlockSpec(bb.shape, const),
            pl.BlockSpec(lg.shape, const), pl.BlockSpec(lb.shape, const),
            pl.BlockSpec(wc.shape, const), pl.BlockSpec(go.shape, const),
            _resident(wo.shape),
            pl.BlockSpec((1, d), const),
            weights(), weights(),
            pl.BlockSpec((1, f, d), lambda i: (j, 0, 0), pipeline_mode=pl.Buffered(1)),
            pl.BlockSpec((1, tm, p.shape[-1]), lambda i: (layer, old(i), 0)),
            pl.BlockSpec((1, d), const),
            _resident(wpg.shape),
            _resident(wpp.shape),
            pl.BlockSpec((1, d), const),
        ],
        out_specs=[pl.BlockSpec((tm, d), lambda i: (old(i), 0)),
                   state(HIST_B, B_WIDTH), state(HIST_C, C_WIDTH)],
        out_shape=[jax.ShapeDtypeStruct((n, d), F32),
                   jax.ShapeDtypeStruct((batch, HIST_B, B_WIDTH), F32),
                   jax.ShapeDtypeStruct((batch, HIST_C, C_WIDTH), F32)],
        scratch_shapes=[pltpu.VMEM((HIST_B + tm, B_WIDTH), F32),
                        pltpu.VMEM((HIST_C + tm, C_WIDTH), F32),
                        pltpu.VMEM((SUBLANES - 1, HIST_B + tm - SUBLANES, B_WIDTH), F32),
                        pltpu.VMEM((tm, d), F32),
                        pltpu.VMEM((tm, d), F32),
                        pltpu.VMEM((tm, d), F32)],
        compiler_params=_params("arbitrary", vmem=FUSED_VMEM_LIMIT),
    )(x, ya, u, u, u, u, hist_b, hist_c, wb, bb, lg, lb, wc, go, wo,
      g, wg, wu, wd, p, gp, wpg, wpp, gf)


def _router_kernel(h_ref, g_ref, wr_ref, br_ref, x3_ref, meta_ref, cnt_ref, carry):
    tm = h_ref.shape[0]

    @pl.when(pl.program_id(0) == 0)
    def _():
        carry[...] = jnp.zeros_like(carry)

    xn = _rms(h_ref[...], g_ref[...])
    for s in range(ROW_SPLIT):
        x3_ref[_chunk(s, tm), :] = xn[:, s * LANES:(s + 1) * LANES]
    logits = lax.dot_general(wr_ref[...], xn, (((1,), (1,)), ((), ())),
                             precision=lax.Precision.HIGHEST,
                             preferred_element_type=F32) + br_ref[...]
    eidx = lax.broadcasted_iota(jnp.int32, (N_EXPERTS, tm), 0)
    m1 = jnp.max(logits, axis=0, keepdims=True)
    i1 = jnp.min(jnp.where(logits == m1, eidx, N_EXPERTS), axis=0, keepdims=True)
    rest = jnp.where(eidx == i1, -jnp.inf, logits)
    m2 = jnp.max(rest, axis=0, keepdims=True)
    i2 = jnp.min(jnp.where(rest == m2, eidx, N_EXPERTS), axis=0, keepdims=True)
    e2 = jnp.exp(m2 - m1)
    g1 = 1.0 / (1.0 + e2)
    g2 = e2 / (1.0 + e2)
    hit1 = eidx == i1
    hit2 = eidx == i2
    chosen = jnp.where(hit1 | hit2, 1.0, 0.0)
    src = lax.broadcasted_iota(jnp.int32, (tm, tm), 0)
    dst = lax.broadcasted_iota(jnp.int32, (tm, tm), 1)
    before = jnp.where(src < dst, 1.0, 0.0).astype(BF16)
    rank = jnp.dot(chosen.astype(BF16), before, preferred_element_type=F32) + carry[:, 0:1]
    r1 = jnp.sum(jnp.where(hit1, rank, 0.0), axis=0, keepdims=True)
    r2 = jnp.sum(jnp.where(hit2, rank, 0.0), axis=0, keepdims=True)
    zero = jnp.zeros_like(g1)
    meta_ref[...] = jnp.concatenate(
        [i1.astype(F32), i2.astype(F32), g1, g2, r1, r2, zero, zero], axis=0)
    carry[...] = carry[...] + jnp.sum(chosen, axis=1, keepdims=True)
    cnt_ref[...] = carry[...]


def _router(h, g, wr_t, br, tm):
    n, d = h.shape
    return pl.pallas_call(
        _router_kernel,
        grid=(n // tm,),
        in_specs=[pl.BlockSpec((tm, d), lambda i: (i, 0)),
                  pl.BlockSpec((1, d), lambda i: (0, 0)),
                  pl.BlockSpec((N_EXPERTS, d), lambda i: (0, 0)),
                  pl.BlockSpec((N_EXPERTS, 1), lambda i: (0, 0))],
        out_specs=[pl.BlockSpec((tm * ROW_SPLIT, LANES), lambda i: (i, 0)),
                   pl.BlockSpec((8, tm), lambda i: (0, i)),
                   pl.BlockSpec((N_EXPERTS, LANES), lambda i: (0, 0))],
        out_shape=[jax.ShapeDtypeStruct((n * ROW_SPLIT, LANES), F32),
                   jax.ShapeDtypeStruct((8, n), F32),
                   jax.ShapeDtypeStruct((N_EXPERTS, LANES), F32)],
        scratch_shapes=[pltpu.VMEM((N_EXPERTS, LANES), F32)],
        compiler_params=_params("arbitrary"),
    )(h, g, wr_t, br)


def _dispatch_kernel(d1_ref, d2_ref, x_ref, o_hbm, sem):
    tm = x_ref.shape[0] // ROW_SPLIT

    for t in range(tm):
        pltpu.make_async_copy(x_ref.at[_row_tile(t)], o_hbm.at[_row_tile(d1_ref[0, 0, t])], sem).start(0)
        pltpu.make_async_copy(x_ref.at[_row_tile(t)], o_hbm.at[_row_tile(d2_ref[0, 0, t])], sem).start(1)
    for _ in range(2):
        pltpu.make_async_copy(x_ref, o_hbm.at[pl.ds(0, tm * ROW_SPLIT)], sem).wait()


def _dispatch(x3, d1, d2, rows_out, tm):
    n = x3.shape[0] // ROW_SPLIT
    steps = d1.shape[0] // tm
    last = n // tm - 1
    idx_spec = pl.BlockSpec((1, 1, tm), lambda i: (i, 0, 0), memory_space=pltpu.SMEM)
    return pl.pallas_call(
        _dispatch_kernel,
        grid=(steps,),
        in_specs=[idx_spec, idx_spec,
                  pl.BlockSpec((tm * ROW_SPLIT, LANES), lambda i: (jnp.minimum(i, last), 0))],
        out_specs=pl.BlockSpec(memory_space=pl.ANY),
        out_shape=jax.ShapeDtypeStruct((rows_out * ROW_SPLIT, LANES), x3.dtype),
        scratch_shapes=[pltpu.SemaphoreType.DMA(())],
        compiler_params=_params("arbitrary"),
    )(d1.reshape(steps, 1, tm), d2.reshape(steps, 1, tm), x3)


def _expert_kernel(texp_ref, nact_ref, x_ref, wg_ref, wu_ref, wd_ref, y_ref, x2, acc):
    del texp_ref

    @pl.when(pl.program_id(0) < nact_ref[0])
    def _():
        for s in range(ROW_SPLIT):
            x2[:, s * LANES:(s + 1) * LANES] = x_ref[_chunk(s, EXPERT_TILE), :].astype(BF16)
        _swiglu_chunks(x2[...], wg_ref, wu_ref, wd_ref, acc)
        for s in range(ROW_SPLIT):
            y_ref[_chunk(s, EXPERT_TILE), :] = acc[:, s * LANES:(s + 1) * LANES]

    @pl.when(pl.program_id(0) >= nact_ref[0])
    def _():
        y_ref[...] = jnp.zeros_like(y_ref)


def _experts(xs, tile_exp, n_active, wg, wu, wd):
    ntile = tile_exp.shape[0]
    d, f = wg.shape[1], wg.shape[2]
    grid_spec = pltpu.PrefetchScalarGridSpec(
        num_scalar_prefetch=2,
        grid=(ntile,),
        in_specs=[
            pl.BlockSpec((EXPERT_TILE * ROW_SPLIT, LANES), lambda t, te, na: (t, 0)),
            pl.BlockSpec((1, d, f), lambda t, te, na: (te[t], 0, 0)),
            pl.BlockSpec((1, d, f), lambda t, te, na: (te[t], 0, 0)),
            pl.BlockSpec((1, f, d), lambda t, te, na: (te[t], 0, 0)),
        ],
        out_specs=pl.BlockSpec((EXPERT_TILE * ROW_SPLIT, LANES), lambda t, te, na: (t, 0)),
        scratch_shapes=[pltpu.VMEM((EXPERT_TILE, d), BF16), pltpu.VMEM((EXPERT_TILE, d), F32)],
    )
    return pl.pallas_call(
        _expert_kernel,
        grid_spec=grid_spec,
        out_shape=jax.ShapeDtypeStruct(xs.shape, F32),
        compiler_params=_params("arbitrary"),
    )(tile_exp, n_active, xs, wg, wu, wd)


def _combine_kernel(final, d1a, d2a, d1b, d2b, d1na, d2na, d1nb, d2nb, ye_hbm, h_ref, gate_ref, p_ref, gp_ref,
                    wpg_ref, wpp_ref, gf_ref, o_ref, a1, a2, b1, b2, na1, na2, nb1, nb2, moe, sems):
    j = pl.program_id(0)
    tm = h_ref.shape[0] // 2

    def gather(d1_ref, d2_ref, y1, y2, sem):
        for t in range(tm):
            pltpu.make_async_copy(ye_hbm.at[_row_tile(d1_ref[0, 0, t])], y1.at[_row_tile(t)], sem).start(0)
            pltpu.make_async_copy(ye_hbm.at[_row_tile(d2_ref[0, 0, t])], y2.at[_row_tile(t)], sem).start(1)

    def wait(y1, y2, sem):
        pltpu.make_async_copy(ye_hbm.at[pl.ds(0, tm * ROW_SPLIT)], y1, sem).wait()
        pltpu.make_async_copy(ye_hbm.at[pl.ds(0, tm * ROW_SPLIT)], y2, sem).wait()

    def combine(y1, y2, rows):
        g1 = gate_ref[rows, 0:1]
        g2 = gate_ref[rows, 1:2]
        for s in range(ROW_SPLIT):
            moe[:, s * LANES:(s + 1) * LANES] = g1 * y1[_chunk(s, tm), :] + g2 * y2[_chunk(s, tm), :]
        h = h_ref[rows, :] + moe[...]
        o_ref[rows, :] = _ple_epilogue(h, p_ref[0, rows, :], gp_ref[...], wpg_ref[...], wpp_ref[...],
                                       gf_ref[...] if final else None)

    def step(cur, nxt, base):
        ca1, ca2, cb1, cb2 = cur
        xa1, xa2, xb1, xb2 = nxt
        wait(ca1, ca2, sems.at[base])
        gather(d1na, d2na, xa1, xa2, sems.at[2 - base])
        combine(ca1, ca2, slice(0, tm))
        wait(cb1, cb2, sems.at[base + 1])
        gather(d1nb, d2nb, xb1, xb2, sems.at[3 - base])
        combine(cb1, cb2, slice(tm, 2 * tm))

        @pl.when(j == pl.num_programs(0) - 1)
        def _():
            wait(xa1, xa2, sems.at[2 - base])
            wait(xb1, xb2, sems.at[3 - base])

    even = (a1, a2, b1, b2)
    odd = (na1, na2, nb1, nb2)

    @pl.when(j == 0)
    def _():
        gather(d1a, d2a, a1, a2, sems.at[0])
        gather(d1b, d2b, b1, b2, sems.at[1])

    @pl.when(j % 2 == 0)
    def _():
        step(even, odd, 0)

    @pl.when(j % 2 == 1)
    def _():
        step(odd, even, 2)


def _combine(ye, d1, d2, h, gates, p, layer, gp, wpg, wpp, gf, final, tm):
    n, d = h.shape
    steps = n // (2 * tm)
    tiles = n // tm
    idx_spec = lambda f: pl.BlockSpec((1, 1, tm), lambda i: (jnp.minimum(f(i), tiles - 1), 0, 0),
                                      memory_space=pltpu.SMEM)
    tile_a, tile_b = idx_spec(lambda i: 2 * i), idx_spec(lambda i: 2 * i + 1)
    next_a, next_b = idx_spec(lambda i: 2 * i + 2), idx_spec(lambda i: 2 * i + 3)
    const = lambda i: (0, 0)
    d1 = d1.reshape(tiles, 1, tm)
    d2 = d2.reshape(tiles, 1, tm)
    row_buf = pltpu.VMEM((tm * ROW_SPLIT, LANES), F32)
    return pl.pallas_call(
        functools.partial(_combine_kernel, final),
        grid=(steps,),
        in_specs=[tile_a, tile_a, tile_b, tile_b, next_a, next_a, next_b, next_b,
                  pl.BlockSpec(memory_space=pl.ANY),
                  pl.BlockSpec((2 * tm, d), lambda i: (i, 0)),
                  pl.BlockSpec((2 * tm, 2), lambda i: (i, 0)),
                  pl.BlockSpec((1, 2 * tm, p.shape[-1]), lambda i: (layer, i, 0)),
                  pl.BlockSpec((1, d), const),
                  _resident(wpg.shape),
                  _resident(wpp.shape),
                  pl.BlockSpec((1, d), const)],
        out_specs=pl.BlockSpec((2 * tm, d), lambda i: (i, 0)),
        out_shape=jax.ShapeDtypeStruct((n, d), F32),
        scratch_shapes=[row_buf] * 8 + [pltpu.VMEM((tm, d), F32), pltpu.SemaphoreType.DMA((4,))],
        compiler_params=_params("arbitrary"),
    )(d1, d2, d1, d2, d1, d2, d1, d2, ye, h, gates, p, gp, wpg, wpp, gf)


def _moe(h, g, wr_t, br, wg, wu, wd, p, layer, gp, wpg, wpp, gf, final, tm):
    n = h.shape[0]
    x3, meta, cnt = _router(h, g, wr_t, br, tm)
    counts = cnt[:, 0].astype(jnp.int32)
    tiles = (counts + EXPERT_TILE - 1) // EXPERT_TILE
    padded = tiles * EXPERT_TILE
    ends = jnp.cumsum(padded)
    offs = ends - padded
    expert = jnp.arange(N_EXPERTS, dtype=jnp.int32)[:, None]
    start_of = lambda idx: jnp.sum(jnp.where(idx[None, :] == expert, offs[:, None], 0), axis=0)
    d1 = start_of(meta[0].astype(jnp.int32)) + meta[4].astype(jnp.int32)
    d2 = start_of(meta[1].astype(jnp.int32)) + meta[5].astype(jnp.int32)
    gates = jnp.stack([meta[2], meta[3]], axis=1)
    ntile = 2 * n // EXPERT_TILE + N_EXPERTS
    r = jnp.arange(EXPERT_TILE, dtype=jnp.int32)[None, :]
    in_group = counts[:, None] + r < padded[:, None]
    spare = ends[-1] + (jnp.cumsum(jnp.where(in_group, 0, 1).reshape(-1)) - 1).reshape(in_group.shape)
    fill = jnp.where(in_group, offs[:, None] + counts[:, None] + r, spare).reshape(-1).astype(jnp.int32)
    half = fill.shape[0] // 2
    xs = _dispatch(x3, jnp.concatenate([d1, fill[:half]]), jnp.concatenate([d2, fill[half:]]),
                   ntile * EXPERT_TILE, DISPATCH_TILE if n % DISPATCH_TILE == 0 else tm)
    n_active = jnp.sum(tiles).astype(jnp.int32)
    tile_start = jnp.minimum(jnp.arange(ntile, dtype=jnp.int32), n_active - 1) * EXPERT_TILE
    tile_exp = jnp.minimum(jnp.sum(jnp.where(tile_start[:, None] >= ends[None, :], 1, 0), axis=1),
                           N_EXPERTS - 1).astype(jnp.int32)
    ye = _experts(xs, tile_exp, n_active.reshape(1), wg, wu, wd)
    return _combine(ye, d1, d2, h, gates, p, layer, gp, wpg, wpp, gf, final, min(tm, n // 2))


def _trunk(x, p, past, prm, tok_tile, mix_tile):
    batch, t, d = x.shape
    n = batch * t
    depth = prm["w_in"].shape[0]
    h = x.reshape(n, d)
    new_k, new_v, new_b, new_c = [], [], [], []
    row = lambda a: a.reshape(1, -1)
    p = p.reshape(depth, n, -1)
    keep = min(BAND, t) if past is None else t
    for i in range(depth):
        u, kv = _inproj(h, row(prm["norm_mix"][i]), prm["w_in"][i], tok_tile, t, keep)
        new_k.append(kv[:, :A_WIDTH].reshape(batch, keep, HEADS, HEAD_DIM))
        new_v.append(kv[:, A_WIDTH:].reshape(batch, keep, HEADS, HEAD_DIM))
        if past is None:
            ya = _prompt_attention(u, _prompt_bias_table(prm["rel_bias"][i]), batch, t)
            hist_b = jnp.zeros((batch, HIST_B, B_WIDTH), F32)
            hist_c = jnp.zeros((batch, HIST_C, C_WIDTH), F32)
        else:
            ck, cv, sb, sc = past
            n_past = ck.shape[2]
            bias_c, bias_n = _sample_bias_tables(prm["rel_bias"][i], t, n_past)
            ya = _sample_attention(u, ck[i].reshape(batch, n_past, A_WIDTH),
                                   cv[i].reshape(batch, n_past, A_WIDTH), bias_c, bias_n, batch, t)
            hist_b = jnp.pad(sb[i], ((0, 0), (HIST_B - (B_KERNEL - 1), 0), (0, 0)))
            hist_c = jnp.pad(sc[i], ((0, 0), (HIST_C - (C_KERNEL - 1), 0), (0, 0)))
        wb = jnp.pad(prm["w_dw_b"][i], ((0, HIST_B - B_KERNEL), (0, 0)))
        wc = jnp.pad(prm["w_dw_c"][i], ((0, HIST_C - C_KERNEL), (0, 0)))
        mix_args = (h, ya, u, hist_b, hist_c, wb, row(prm["b_dw_b"][i]), row(prm["ln_g_b"][i]),
                    row(prm["ln_b_b"][i]), wc, row(prm["g_out"][i]), prm["w_out"][i])
        final = i == depth - 1
        tail = (p, i, row(prm["norm_ple"][i]), prm["w_ple_gate"][i], prm["w_ple_proj"][i],
                row(prm["norm_final"]), final)
        j = i // 2
        ffn_args = (row(prm["norm_ffn"][i]), prm["w_ff_gate"], prm["w_ff_up"], prm["w_ff_down"], j)
        if i % 2 == 0 and tok_tile == mix_tile:
            h, nb, nc = _mix_ffn(*mix_args, *ffn_args, *tail, batch, t, mix_tile)
        else:
            h, nb, nc = _mix(*mix_args, batch, t, mix_tile)
            if i % 2 == 0:
                h = _ffn(h, *ffn_args, *tail, tok_tile)
            else:
                h = _moe(h, row(prm["norm_ffn"][i]), prm["w_router"][j].T, prm["b_router"][j].reshape(-1, 1),
                         prm["w_ex_gate"][j], prm["w_ex_up"][j], prm["w_ex_down"][j], *tail, tok_tile)
        new_b.append(nb[:, HIST_B - (B_KERNEL - 1):])
        new_c.append(nc[:, HIST_C - (C_KERNEL - 1):])
    return (h.reshape(batch, t, d), jnp.stack(new_k), jnp.stack(new_v), jnp.stack(new_b), jnp.stack(new_c))


def kernel(x_prompt, x_sample, p_prompt, p_sample, cache_attn_k, cache_attn_v, state_conv_b, state_conv_c,
           w_in, rel_bias, w_dw_b, b_dw_b, ln_g_b, ln_b_b, w_dw_c, g_out, w_out, norm_mix, norm_ffn,
           w_ff_gate, w_ff_up, w_ff_down, w_router, b_router, w_ex_gate, w_ex_up, w_ex_down,
           norm_ple, w_ple_gate, w_ple_proj, norm_final):
    prm = dict(
        w_in=w_in.astype(BF16), rel_bias=rel_bias, w_dw_b=w_dw_b, b_dw_b=b_dw_b, ln_g_b=ln_g_b,
        ln_b_b=ln_b_b, w_dw_c=w_dw_c, g_out=g_out, w_out=w_out.astype(BF16), norm_mix=norm_mix,
        norm_ffn=norm_ffn, w_ff_gate=w_ff_gate.astype(BF16), w_ff_up=w_ff_up.astype(BF16),
        w_ff_down=w_ff_down.astype(BF16), w_router=w_router, b_router=b_router,
        w_ex_gate=w_ex_gate.astype(BF16), w_ex_up=w_ex_up.astype(BF16), w_ex_down=w_ex_down.astype(BF16),
        norm_ple=norm_ple, w_ple_gate=w_ple_gate.astype(BF16), w_ple_proj=w_ple_proj.astype(BF16),
        norm_final=norm_final)
    dec_t = x_sample.shape[1]
    y_p, k_p, v_p, b_p, c_p = _trunk(x_prompt, p_prompt, None, prm, 512, 512)
    y_s, k_s, v_s, b_s, c_s = _trunk(x_sample, p_sample,
                                     (cache_attn_k, cache_attn_v, state_conv_b, state_conv_c),
                                     prm, x_sample.shape[0] * dec_t, dec_t)
    return (y_p, y_s, k_p, v_p, b_p, c_p, k_s, v_s, b_s, c_s)
```

```python
import functools

import jax
import jax.numpy as jnp
from jax import lax
from jax.experimental import pallas as pl
from jax.experimental.pallas import tpu as pltpu

F32 = jnp.float32
BF16 = jnp.bfloat16

CHUNK = 64
BAND = 512
HEADS = 8
HEAD_DIM = 64
A_WIDTH = HEADS * HEAD_DIM
B_WIDTH = 256
C_WIDTH = 256
B_KERNEL = 31
C_KERNEL = 3
REL_CLIP = 128
N_EXPERTS = 8
EPS = 1e-6
NEG = -1e30
LOG2E = 1.4426950408889634

LANES = 128
SUBLANES = 8
ROW_SPLIT = 8
QTILE = 512
PAIR_ROWS = 2 * CHUNK
WIN = BAND + PAIR_ROWS
HIST_B = 32
HIST_C = 8
EXPERT_TILE = 512
DISPATCH_TILE = 1024
FF_CHUNK = 512
MIX_SHIFT_BREAKS = (3, 7)
MIX_TAP_BREAKS = (7, 15, 23, 30)
VMEM_LIMIT = 56 * 1024 * 1024
FUSED_VMEM_LIMIT = 60 * 1024 * 1024


def _params(*sem, vmem=VMEM_LIMIT):
    return pltpu.CompilerParams(dimension_semantics=sem, vmem_limit_bytes=vmem)


def _rms(x, g):
    return x * lax.rsqrt(jnp.mean(x * x, axis=-1, keepdims=True) + EPS) * g


def _sigmoid(x):
    return 1.0 / (1.0 + jnp.exp(-x))


def _chunk(s, n):
    return pl.ds(s, n, stride=ROW_SPLIT)


def _row_tile(r):
    return pl.ds(pl.multiple_of(r * ROW_SPLIT, ROW_SPLIT), ROW_SPLIT)


def _inproj_kernel(x_ref, g_ref, w_ref, u_ref, kv_ref):
    hn = _rms(x_ref[...], g_ref[...]).astype(BF16)
    q = jnp.dot(hn, w_ref[:, :A_WIDTH], preferred_element_type=F32)
    u_ref[:, :A_WIDTH] = (q * (HEAD_DIM ** -0.5 * LOG2E)).astype(BF16)
    kv = jnp.dot(hn, w_ref[:, A_WIDTH:3 * A_WIDTH], preferred_element_type=F32)
    kv_ref[...] = kv
    u_ref[:, A_WIDTH:3 * A_WIDTH] = kv.astype(BF16)
    rest = jnp.dot(hn, w_ref[:, 3 * A_WIDTH:], preferred_element_type=F32)
    u_ref[:, 3 * A_WIDTH:] = rest.astype(BF16)


def _inproj(x, g, w, tm, t, keep):
    n, d = x.shape
    wi = w.shape[1]
    if keep == t:
        kv_index = lambda i: (i, 0)
    else:
        nt, ktiles = t // tm, keep // tm
        kv_index = lambda i: ((i // nt) * ktiles + jnp.maximum(i % nt - (nt - ktiles), 0), 0)
    return pl.pallas_call(
        _inproj_kernel,
        grid=(n // tm,),
        in_specs=[pl.BlockSpec((tm, d), lambda i: (i, 0)),
                  pl.BlockSpec((1, d), lambda i: (0, 0)),
                  pl.BlockSpec((d, wi), lambda i: (0, 0))],
        out_specs=[pl.BlockSpec((tm, wi), lambda i: (i, 0)),
                   pl.BlockSpec((tm, 2 * A_WIDTH), kv_index)],
        out_shape=[jax.ShapeDtypeStruct((n, wi), BF16),
                   jax.ShapeDtypeStruct((n // t * keep, 2 * A_WIDTH), F32)],
        compiler_params=_params("arbitrary"),
    )(x, g, w)


def _attn_kernel(q_ref, kp_ref, kc_ref, vp_ref, vc_ref, bias_ref, o_ref, k_scr, vt_scr, s_scr):
    i = pl.program_id(1)

    @pl.when(i == 0)
    def _():
        k_scr[0:QTILE, :] = kp_ref[...]
        vt_scr[:, 0:QTILE] = vp_ref[...].astype(F32).T.astype(BF16)

    @pl.when(i > 0)
    def _():
        k_scr[0:QTILE, :] = k_scr[QTILE:2 * QTILE, :]
        vt_scr[:, 0:QTILE] = vt_scr[:, QTILE:2 * QTILE]

    k_scr[QTILE:2 * QTILE, :] = kc_ref[...]
    vt_scr[:, QTILE:2 * QTILE] = vc_ref[...].astype(F32).T.astype(BF16)
    lane = lax.broadcasted_iota(jnp.int32, (PAIR_ROWS, LANES), 1)
    row = lax.broadcasted_iota(jnp.int32, (WIN, 2 * PAIR_ROWS), 0)
    nblk = QTILE // PAIR_ROWS
    npair = HEADS // 2

    def scores(p):
        cols = slice(p * LANES, (p + 1) * LANES)
        for c2 in range(nblk):
            r0 = c2 * PAIR_ROWS
            kwin = k_scr[r0:r0 + WIN, cols]
            q2 = q_ref[r0:r0 + PAIR_ROWS, cols]
            zero = jnp.zeros_like(q2)
            qq = jnp.concatenate([jnp.where(lane < HEAD_DIM, q2, zero),
                                  jnp.where(lane >= HEAD_DIM, q2, zero)], axis=0)
            s_scr[(p % 2) * nblk + c2] = lax.dot_general(
                kwin, qq, (((1,), (1,)), ((), ())), preferred_element_type=F32)

    def tile(first):
        scores(0)
        for p in range(npair):
            if p + 1 < npair:
                scores(p + 1)
            cols = slice(p * LANES, (p + 1) * LANES)
            for c2 in range(nblk):
                r0 = c2 * PAIR_ROWS
                vwin = vt_scr[cols, r0:r0 + WIN]
                s = s_scr[(p % 2) * nblk + c2] + bias_ref[p]
                if first:
                    s = jnp.where(row < BAND - r0, NEG, s)
                m = jnp.max(s, axis=0, keepdims=True)
                e = jnp.exp2(s - m)
                l = jnp.sum(e, axis=0, keepdims=True)
                ot = jnp.dot(vwin, e.astype(BF16), preferred_element_type=F32)
                ot = ot * (1.0 / l)
                o = jnp.concatenate([ot[:HEAD_DIM, :PAIR_ROWS], ot[HEAD_DIM:, PAIR_ROWS:]], axis=0).T
                o_ref[r0:r0 + PAIR_ROWS, cols] = o.astype(o_ref.dtype)

    @pl.when(i == 0)
    def _():
        tile(True)

    @pl.when(i > 0)
    def _():
        tile(False)


def _prompt_attention(u, bias_t, batch, seq):
    nt = seq // QTILE
    prev = lambda b, i: b * nt + jnp.maximum(i - 1, 0)
    return pl.pallas_call(
        _attn_kernel,
        grid=(batch, nt),
        in_specs=[
            pl.BlockSpec((QTILE, A_WIDTH), lambda b, i: (b * nt + i, 0)),
            pl.BlockSpec((QTILE, A_WIDTH), lambda b, i: (prev(b, i), 1)),
            pl.BlockSpec((QTILE, A_WIDTH), lambda b, i: (b * nt + i, 1)),
            pl.BlockSpec((QTILE, A_WIDTH), lambda b, i: (prev(b, i), 2)),
            pl.BlockSpec((QTILE, A_WIDTH), lambda b, i: (b * nt + i, 2)),
            pl.BlockSpec((HEADS // 2, WIN, 2 * PAIR_ROWS), lambda b, i: (0, 0, 0)),
        ],
        out_specs=pl.BlockSpec((QTILE, A_WIDTH), lambda b, i: (b * nt + i, 0)),
        out_shape=jax.ShapeDtypeStruct((batch * seq, A_WIDTH), BF16),
        scratch_shapes=[pltpu.VMEM((2 * QTILE, A_WIDTH), BF16),
                        pltpu.VMEM((A_WIDTH, 2 * QTILE), BF16),
                        pltpu.VMEM((2 * QTILE // PAIR_ROWS, WIN, 2 * PAIR_ROWS), F32)],
        compiler_params=_params("arbitrary", "arbitrary"),
    )(u, u, u, u, u, bias_t)


def _hankel(v, rows, cols):
    heads, length = v.shape
    flat = jnp.tile(v, (1, rows + 1))[:, :rows * (length + 1)]
    return flat.reshape(heads, rows, length + 1)[:, :, :cols]


def _prompt_bias_table(rel_bias):
    by_dist = jnp.concatenate(
        [jnp.broadcast_to(rel_bias[:, :1], (HEADS, BAND)), rel_bias[:, :2 * REL_CLIP]], axis=1).astype(F32)
    by_sum = _hankel(by_dist, PAIR_ROWS, WIN + 1)
    table = jnp.transpose(by_sum[:, ::-1, 1:], (0, 2, 1))
    j = jnp.arange(WIN, dtype=jnp.int32)[:, None]
    lo = (jnp.arange(PAIR_ROWS, dtype=jnp.int32)[None, :] // CHUNK) * CHUNK
    valid = (j >= lo) & (j < lo + BAND + CHUNK)
    table = jnp.where(valid[None], table * LOG2E, NEG)
    table = table.reshape(HEADS // 2, 2, WIN, PAIR_ROWS)
    return jnp.transpose(table, (0, 2, 1, 3)).reshape(HEADS // 2, WIN, 2 * PAIR_ROWS)


def _sample_attn_kernel(q_ref, kn_ref, vn_ref, ck_ref, cv_ref, bc_ref, bn_ref, o_ref):
    t = q_ref.shape[0]
    lane = lax.broadcasted_iota(jnp.int32, (t, LANES), 1)
    kc = ck_ref[0].astype(BF16)
    vc = cv_ref[0].astype(BF16)
    for p in range(HEADS // 2):
        cols = slice(p * LANES, (p + 1) * LANES)
        q2 = q_ref[:, cols]
        acc = None
        for hh in range(2):
            h = 2 * p + hh
            sel = (lane >= HEAD_DIM) == bool(hh)
            qm = jnp.where(sel, q2, jnp.zeros_like(q2))
            nt_dims = (((1,), (1,)), ((), ()))
            s_c = lax.dot_general(qm, kc[:, cols], nt_dims, preferred_element_type=F32) + bc_ref[h]
            s_n = lax.dot_general(qm, kn_ref[:, cols], nt_dims, preferred_element_type=F32) + bn_ref[h]
            m = jnp.maximum(jnp.max(s_c, axis=-1, keepdims=True), jnp.max(s_n, axis=-1, keepdims=True))
            e_c = jnp.exp2(s_c - m)
            e_n = jnp.exp2(s_n - m)
            l = jnp.sum(e_c, axis=-1, keepdims=True) + jnp.sum(e_n, axis=-1, keepdims=True)
            pv = (jnp.dot(e_c.astype(BF16), vc[:, cols], preferred_element_type=F32)
                  + jnp.dot(e_n.astype(BF16), vn_ref[:, cols], preferred_element_type=F32)) * (1.0 / l)
            acc = pv if acc is None else jnp.where(sel, pv, acc)
        o_ref[:, cols] = acc.astype(o_ref.dtype)


def _sample_attention(u, cache_k, cache_v, bias_c, bias_n, batch, t):
    n_past = cache_k.shape[1]
    return pl.pallas_call(
        _sample_attn_kernel,
        grid=(batch,),
        in_specs=[
            pl.BlockSpec((t, A_WIDTH), lambda b: (b, 0)),
            pl.BlockSpec((t, A_WIDTH), lambda b: (b, 1)),
            pl.BlockSpec((t, A_WIDTH), lambda b: (b, 2)),
            pl.BlockSpec((1, n_past, A_WIDTH), lambda b: (b, 0, 0)),
            pl.BlockSpec((1, n_past, A_WIDTH), lambda b: (b, 0, 0)),
            pl.BlockSpec((HEADS, t, n_past), lambda b: (0, 0, 0)),
            pl.BlockSpec((HEADS, t, t), lambda b: (0, 0, 0)),
        ],
        out_specs=pl.BlockSpec((t, A_WIDTH), lambda b: (b, 0)),
        out_shape=jax.ShapeDtypeStruct((batch * t, A_WIDTH), BF16),
        compiler_params=_params("arbitrary"),
    )(u, u, u, cache_k, cache_v, bias_c, bias_n)


def _sample_bias_tables(rel_bias, t, n_past):
    def by_dist(lo, hi):
        dist = jnp.arange(lo, hi + 1, dtype=jnp.int32)
        return rel_bias[:, jnp.clip(dist, -REL_CLIP, REL_CLIP) + REL_CLIP].astype(F32) * LOG2E

    cache = by_dist(-n_past - t + 1, -1)
    new = by_dist(-t + 1, t - 1)
    return _hankel(cache, t, n_past)[:, ::-1], _hankel(new, t, t)[:, ::-1]


def _mix_stages(x_ref, ya_ref, ab_ref, gc_ref, gb_ref, hc_ref, wb_ref, bb_ref, lg_ref, lb_ref, wc_ref, go_ref,
                wo_ref, cb, cc, sh):
    tm = x_ref.shape[0]
    ab = ab_ref[...].astype(F32)
    glu = ab[:, :B_WIDTH] * _sigmoid(ab[:, B_WIDTH:])
    pre = gc_ref[...].astype(F32) * hc_ref[...].astype(F32)
    cb[HIST_B:HIST_B + tm, :] = glu
    cc[HIST_C:HIST_C + tm, :] = pre
    yield _token(glu) + _token(pre)
    span = tm + HIST_B - SUBLANES
    moved = 0.0
    for r in range(1, SUBLANES):
        shifted = cb[pl.ds(r, span), :]
        sh[r - 1, 0:span, :] = shifted
        moved = moved + _token(shifted)
        if r in MIX_SHIFT_BREAKS:
            yield moved
            moved = 0.0
    z = jnp.zeros((tm, B_WIDTH), F32) + bb_ref[...]
    for j in range(B_KERNEL):
        m = j + HIST_B - (B_KERNEL - 1)
        a = m // SUBLANES * SUBLANES
        src = cb[a:a + tm, :] if m == a else sh[m - a - 1, a:a + tm, :]
        z = z + wb_ref[j:j + 1, :] * src
        if j in MIX_TAP_BREAKS:
            yield _token(z)
    mu = jnp.mean(z, axis=-1, keepdims=True)
    zc = z - mu
    var = jnp.mean(zc * zc, axis=-1, keepdims=True)
    yb = zc * lax.rsqrt(var + EPS) * lg_ref[...] + lb_ref[...]
    yb = yb * _sigmoid(yb)

    conv = jnp.zeros((tm, C_WIDTH), F32)
    for j in range(C_KERNEL):
        conv = conv + wc_ref[j:j + 1, :] * cc[pl.ds(j + HIST_C - (C_KERNEL - 1), tm), :]
    yc = gb_ref[...].astype(F32) * conv
    yield _token(yb) + _token(yc)

    go = go_ref[...]
    na = _rms(ya_ref[...].astype(F32), go[:, :A_WIDTH]).astype(BF16)
    nb = _rms(yb, go[:, A_WIDTH:A_WIDTH + B_WIDTH]).astype(BF16)
    nc = _rms(yc, go[:, A_WIDTH + B_WIDTH:]).astype(BF16)
    return (x_ref[...]
            + jnp.dot(na, wo_ref[0:A_WIDTH, :], preferred_element_type=F32)
            + jnp.dot(nb, wo_ref[A_WIDTH:A_WIDTH + B_WIDTH, :], preferred_element_type=F32)
            + jnp.dot(nc, wo_ref[A_WIDTH + B_WIDTH:, :], preferred_element_type=F32))


def _token(v):
    t = jnp.sum(v, axis=0, keepdims=True)
    return sum(t[:, c:c + LANES] for c in range(0, v.shape[1], LANES))


def _finish(stages):
    while True:
        try:
            next(stages)
        except StopIteration as done:
            return done.value


def _mix_carry(tm, cb, cc, nb_ref, nc_ref):
    tail_b = cb[tm:tm + HIST_B, :]
    tail_c = cc[tm:tm + HIST_C, :]
    nb_ref[0] = tail_b
    nc_ref[0] = tail_c
    cb[0:HIST_B, :] = tail_b
    cc[0:HIST_C, :] = tail_c


def _mix_kernel(x_ref, ya_ref, ab_ref, gc_ref, gb_ref, hc_ref, hb0_ref, hc0_ref,
                wb_ref, bb_ref, lg_ref, lb_ref, wc_ref, go_ref, wo_ref,
                h_ref, nb_ref, nc_ref, cb, cc, sh):
    @pl.when(pl.program_id(1) == 0)
    def _():
        cb[0:HIST_B, :] = hb0_ref[0]
        cc[0:HIST_C, :] = hc0_ref[0]

    h_ref[...] = _finish(_mix_stages(x_ref, ya_ref, ab_ref, gc_ref, gb_ref, hc_ref, wb_ref, bb_ref, lg_ref,
                                     lb_ref, wc_ref, go_ref, wo_ref, cb, cc, sh))
    _mix_carry(x_ref.shape[0], cb, cc, nb_ref, nc_ref)


def _mix(x, ya, u, hist_b, hist_c, wb, bb, lg, lb, wc, go, wo, batch, t, tm):
    n, d = x.shape
    nt = t // tm
    row = lambda b, i: b * nt + i
    ab_blk = 3 * A_WIDTH // (2 * B_WIDTH)
    c_blk = (3 * A_WIDTH + 2 * B_WIDTH) // C_WIDTH
    const = lambda b, i: (0, 0)
    return pl.pallas_call(
        _mix_kernel,
        grid=(batch, nt),
        in_specs=[
            pl.BlockSpec((tm, d), lambda b, i: (row(b, i), 0)),
            pl.BlockSpec((tm, A_WIDTH), lambda b, i: (row(b, i), 0)),
            pl.BlockSpec((tm, 2 * B_WIDTH), lambda b, i: (row(b, i), ab_blk)),
            pl.BlockSpec((tm, C_WIDTH), lambda b, i: (row(b, i), c_blk)),
            pl.BlockSpec((tm, C_WIDTH), lambda b, i: (row(b, i), c_blk + 1)),
            pl.BlockSpec((tm, C_WIDTH), lambda b, i: (row(b, i), c_blk + 2)),
            pl.BlockSpec((1, HIST_B, B_WIDTH), lambda b, i: (b, 0, 0)),
            pl.BlockSpec((1, HIST_C, C_WIDTH), lambda b, i: (b, 0, 0)),
            pl.BlockSpec(wb.shape, const), pl.BlockSpec(bb.shape, const),
            pl.BlockSpec(lg.shape, const), pl.BlockSpec(lb.shape, const),
            pl.BlockSpec(wc.shape, const), pl.BlockSpec(go.shape, const),
            pl.BlockSpec(wo.shape, const),
        ],
        out_specs=[pl.BlockSpec((tm, d), lambda b, i: (row(b, i), 0)),
                   pl.BlockSpec((1, HIST_B, B_WIDTH), lambda b, i: (b, 0, 0)),
                   pl.BlockSpec((1, HIST_C, C_WIDTH), lambda b, i: (b, 0, 0))],
        out_shape=[jax.ShapeDtypeStruct((n, d), F32),
                   jax.ShapeDtypeStruct((batch, HIST_B, B_WIDTH), F32),
                   jax.ShapeDtypeStruct((batch, HIST_C, C_WIDTH), F32)],
        scratch_shapes=[pltpu.VMEM((HIST_B + tm, B_WIDTH), F32),
                        pltpu.VMEM((HIST_C + tm, C_WIDTH), F32),
                        pltpu.VMEM((SUBLANES - 1, HIST_B + tm - SUBLANES, B_WIDTH), F32)],
        compiler_params=_params("arbitrary", "arbitrary"),
    )(x, ya, u, u, u, u, hist_b, hist_c, wb, bb, lg, lb, wc, go, wo)


def _ple_epilogue(h, p, gp, wpg, wpp, gf):
    gate = _sigmoid(jnp.dot(_rms(h, gp).astype(BF16), wpg, preferred_element_type=F32))
    h = h + gate * jnp.dot(p.astype(BF16), wpp, preferred_element_type=F32)
    return h if gf is None else _rms(h, gf)


def _swiglu_chunks(x, wg_ref, wu_ref, wd_ref, acc, between=None):
    f = wg_ref.shape[-1]
    for c in range(f // FF_CHUNK):
        cols = slice(c * FF_CHUNK, (c + 1) * FF_CHUNK)
        a = jnp.dot(x, wg_ref[0, :, cols], preferred_element_type=F32)
        b = jnp.dot(x, wu_ref[0, :, cols], preferred_element_type=F32)
        mid = (a * _sigmoid(a) * b).astype(BF16)
        part = jnp.dot(mid, wd_ref[0, cols, :], preferred_element_type=F32)
        if c == 0:
            acc[...] = part
        else:
            acc[...] += part
        if between is not None:
            done = next(between)
            acc[0:SUBLANES, 0:LANES] += jnp.where(done != done, done, 0.0)


def _ffn_kernel(final, h_ref, g_ref, wg_ref, wu_ref, wd_ref, p_ref, gp_ref, wpg_ref, wpp_ref, gf_ref,
                o_ref, acc):
    xn = _rms(h_ref[...], g_ref[...]).astype(BF16)
    _swiglu_chunks(xn, wg_ref, wu_ref, wd_ref, acc)
    h = h_ref[...] + acc[...]
    o_ref[...] = _ple_epilogue(h, p_ref[0], gp_ref[...], wpg_ref[...], wpp_ref[...],
                               gf_ref[...] if final else None)


def _resident(shape):
    return pl.BlockSpec(shape, lambda *_: (0,) * len(shape), pipeline_mode=pl.Buffered(1))


def _ffn(h, g, wg, wu, wd, j, p, layer, gp, wpg, wpp, gf, final, tm):
    n, d = h.shape
    f = wg.shape[-1]
    const = lambda i: (0, 0)
    return pl.pallas_call(
        functools.partial(_ffn_kernel, final),
        grid=(n // tm,),
        in_specs=[
            pl.BlockSpec((tm, d), lambda i: (i, 0)),
            pl.BlockSpec((1, d), const),
            pl.BlockSpec((1, d, f), lambda i: (j, 0, 0), pipeline_mode=pl.Buffered(1)),
            pl.BlockSpec((1, d, f), lambda i: (j, 0, 0), pipeline_mode=pl.Buffered(1)),
            pl.BlockSpec((1, f, d), lambda i: (j, 0, 0), pipeline_mode=pl.Buffered(1)),
            pl.BlockSpec((1, tm, p.shape[-1]), lambda i: (layer, i, 0)),
            pl.BlockSpec((1, d), const),
            _resident(wpg.shape),
            _resident(wpp.shape),
            pl.BlockSpec((1, d), const),
        ],
        out_specs=pl.BlockSpec((tm, d), lambda i: (i, 0)),
        out_shape=jax.ShapeDtypeStruct((n, d), F32),
        scratch_shapes=[pltpu.VMEM((tm, d), F32)],
        compiler_params=_params("arbitrary"),
    )(h, g, wg, wu, wd, p, gp, wpg, wpp, gf)


def _mix_ffn_kernel(final, nt,
                    x_ref, ya_ref, ab_ref, gc_ref, gb_ref, hc_ref, hb0_ref, hc0_ref,
                    wb_ref, bb_ref, lg_ref, lb_ref, wc_ref, go_ref, wo_ref,
                    g_ref, wg_ref, wu_ref, wd_ref, p_ref, gp_ref, wpg_ref, wpp_ref, gf_ref,
                    o_ref, nb_ref, nc_ref, cb, cc, sh, h1a, h1b, acc):
    g = pl.program_id(0)
    tiles = pl.num_programs(0) - 1
    tm = x_ref.shape[0]

    @pl.when(g == 0)
    def _():
        h1b[...] = jnp.zeros_like(h1b)

    @pl.when(jnp.minimum(g, tiles - 1) % nt == 0)
    def _():
        cb[0:HIST_B, :] = hb0_ref[0]
        cc[0:HIST_C, :] = hc0_ref[0]

    def step(h_new, h_old):
        mixers = _mix_stages(x_ref, ya_ref, ab_ref, gc_ref, gb_ref, hc_ref, wb_ref, bb_ref, lg_ref, lb_ref,
                             wc_ref, go_ref, wo_ref, cb, cc, sh)
        _swiglu_chunks(_rms(h_old[...], g_ref[...]).astype(BF16), wg_ref, wu_ref, wd_ref, acc, mixers)
        h_new[...] = _finish(mixers)
        o_ref[...] = _ple_epilogue(h_old[...] + acc[...], p_ref[0], gp_ref[...], wpg_ref[...], wpp_ref[...],
                                   gf_ref[...] if final else None)

    @pl.when(g % 2 == 0)
    def _():
        step(h1a, h1b)

    @pl.when(g % 2 == 1)
    def _():
        step(h1b, h1a)

    @pl.when(g < tiles)
    def _():
        _mix_carry(tm, cb, cc, nb_ref, nc_ref)


def _mix_ffn(x, ya, u, hist_b, hist_c, wb, bb, lg, lb, wc, go, wo,
             g, wg, wu, wd, j, p, layer, gp, wpg, wpp, gf, final, batch, t, tm):
    n, d = x.shape
    f = wg.shape[-1]
    nt = t // tm
    tiles = n // tm
    cur = lambda i: jnp.minimum(i, tiles - 1)
    old = lambda i: jnp.maximum(i - 1, 0)
    ab_blk = 3 * A_WIDTH // (2 * B_WIDTH)
    c_blk = (3 * A_WIDTH + 2 * B_WIDTH) // C_WIDTH
    const = lambda i: (0, 0)
    state = lambda rows, width: pl.BlockSpec((1, rows, width), lambda i: (cur(i) // nt, 0, 0))
    weights = lambda: pl.BlockSpec((1,) + wg.shape[1:], lambda i: (j, 0, 0), pipeline_mode=pl.Buffered(1))
    return pl.pallas_call(
        functools.partial(_mix_ffn_kernel, final, nt),
        grid=(tiles + 1,),
        in_specs=[
            pl.BlockSpec((tm, d), lambda i: (cur(i), 0)),
            pl.BlockSpec((tm, A_WIDTH), lambda i: (cur(i), 0)),
            pl.BlockSpec((tm, 2 * B_WIDTH), lambda i: (cur(i), ab_blk)),
            pl.BlockSpec((tm, C_WIDTH), lambda i: (cur(i), c_blk)),
            pl.BlockSpec((tm, C_WIDTH), lambda i: (cur(i), c_blk + 1)),
            pl.BlockSpec((tm, C_WIDTH), lambda i: (cur(i), c_blk + 2)),
            state(HIST_B, B_WIDTH), state(HIST_C, C_WIDTH),
            pl.BlockSpec(wb.shape, const), pl.BlockSpec(bb.shape, const),
            pl.BlockSpec(lg.shape, const), pl.BlockSpec(lb.shape, const),
            pl.BlockSpec(wc.shape, const), pl.BlockSpec(go.shape, const),
            _resident(wo.shape),
            pl.BlockSpec((1, d), const),
            weights(), weights(),
            pl.BlockSpec((1, f, d), lambda i: (j, 0, 0), pipeline_mode=pl.Buffered(1)),
            pl.BlockSpec((1, tm, p.shape[-1]), lambda i: (layer, old(i), 0)),
            pl.BlockSpec((1, d), const),
            _resident(wpg.shape),
            _resident(wpp.shape),
            pl.BlockSpec((1, d), const),
        ],
        out_specs=[pl.BlockSpec((tm, d), lambda i: (old(i), 0)),
                   state(HIST_B, B_WIDTH), state(HIST_C, C_WIDTH)],
        out_shape=[jax.ShapeDtypeStruct((n, d), F32),
                   jax.ShapeDtypeStruct((batch, HIST_B, B_WIDTH), F32),
                   jax.ShapeDtypeStruct((batch, HIST_C, C_WIDTH), F32)],
        scratch_shapes=[pltpu.VMEM((HIST_B + tm, B_WIDTH), F32),
                        pltpu.VMEM((HIST_C + tm, C_WIDTH), F32),
                        pltpu.VMEM((SUBLANES - 1, HIST_B + tm - SUBLANES, B_WIDTH), F32),
                        pltpu.VMEM((tm, d), F32),
                        pltpu.VMEM((tm, d), F32),
                        pltpu.VMEM((tm, d), F32)],
        compiler_params=_params("arbitrary", vmem=FUSED_VMEM_LIMIT),
    )(x, ya, u, u, u, u, hist_b, hist_c, wb, bb, lg, lb, wc, go, wo,
      g, wg, wu, wd, p, gp, wpg, wpp, gf)


def _router_kernel(h_ref, g_ref, wr_ref, br_ref, x3_ref, meta_ref, cnt_ref, carry):
    tm = h_ref.shape[0]

    @pl.when(pl.program_id(0) == 0)
    def _():
        carry[...] = jnp.zeros_like(carry)

    xn = _rms(h_ref[...], g_ref[...])
    for s in range(ROW_SPLIT):
        x3_ref[_chunk(s, tm), :] = xn[:, s * LANES:(s + 1) * LANES]
    logits = lax.dot_general(wr_ref[...], xn, (((1,), (1,)), ((), ())),
                             precision=lax.Precision.HIGHEST,
                             preferred_element_type=F32) + br_ref[...]
    eidx = lax.broadcasted_iota(jnp.int32, (N_EXPERTS, tm), 0)
    m1 = jnp.max(logits, axis=0, keepdims=True)
    i1 = jnp.min(jnp.where(logits == m1, eidx, N_EXPERTS), axis=0, keepdims=True)
    rest = jnp.where(eidx == i1, -jnp.inf, logits)
    m2 = jnp.max(rest, axis=0, keepdims=True)
    i2 = jnp.min(jnp.where(rest == m2, eidx, N_EXPERTS), axis=0, keepdims=True)
    e2 = jnp.exp(m2 - m1)
    g1 = 1.0 / (1.0 + e2)
    g2 = e2 / (1.0 + e2)
    hit1 = eidx == i1
    hit2 = eidx == i2
    chosen = jnp.where(hit1 | hit2, 1.0, 0.0)
    src = lax.broadcasted_iota(jnp.int32, (tm, tm), 0)
    dst = lax.broadcasted_iota(jnp.int32, (tm, tm), 1)
    before = jnp.where(src < dst, 1.0, 0.0).astype(BF16)
    rank = jnp.dot(chosen.astype(BF16), before, preferred_element_type=F32) + carry[:, 0:1]
    r1 = jnp.sum(jnp.where(hit1, rank, 0.0), axis=0, keepdims=True)
    r2 = jnp.sum(jnp.where(hit2, rank, 0.0), axis=0, keepdims=True)
    zero = jnp.zeros_like(g1)
    meta_ref[...] = jnp.concatenate(
        [i1.astype(F32), i2.astype(F32), g1, g2, r1, r2, zero, zero], axis=0)
    carry[...] = carry[...] + jnp.sum(chosen, axis=1, keepdims=True)
    cnt_ref[...] = carry[...]


def _router(h, g, wr_t, br, tm):
    n, d = h.shape
    return pl.pallas_call(
        _router_kernel,
        grid=(n // tm,),
        in_specs=[pl.BlockSpec((tm, d), lambda i: (i, 0)),
                  pl.BlockSpec((1, d), lambda i: (0, 0)),
                  pl.BlockSpec((N_EXPERTS, d), lambda i: (0, 0)),
                  pl.BlockSpec((N_EXPERTS, 1), lambda i: (0, 0))],
        out_specs=[pl.BlockSpec((tm * ROW_SPLIT, LANES), lambda i: (i, 0)),
                   pl.BlockSpec((8, tm), lambda i: (0, i)),
                   pl.BlockSpec((N_EXPERTS, LANES), lambda i: (0, 0))],
        out_shape=[jax.ShapeDtypeStruct((n * ROW_SPLIT, LANES), F32),
                   jax.ShapeDtypeStruct((8, n), F32),
                   jax.ShapeDtypeStruct((N_EXPERTS, LANES), F32)],
        scratch_shapes=[pltpu.VMEM((N_EXPERTS, LANES), F32)],
        compiler_params=_params("arbitrary"),
    )(h, g, wr_t, br)


def _dispatch_kernel(d1_ref, d2_ref, x_ref, o_hbm, sem):
    tm = x_ref.shape[0] // ROW_SPLIT

    for t in range(tm):
        pltpu.make_async_copy(x_ref.at[_row_tile(t)], o_hbm.at[_row_tile(d1_ref[0, 0, t])], sem).start(0)
        pltpu.make_async_copy(x_ref.at[_row_tile(t)], o_hbm.at[_row_tile(d2_ref[0, 0, t])], sem).start(1)
    for _ in range(2):
        pltpu.make_async_copy(x_ref, o_hbm.at[pl.ds(0, tm * ROW_SPLIT)], sem).wait()


def _dispatch(x3, d1, d2, rows_out, tm):
    n = x3.shape[0] // ROW_SPLIT
    steps = d1.shape[0] // tm
    last = n // tm - 1
    idx_spec = pl.BlockSpec((1, 1, tm), lambda i: (i, 0, 0), memory_space=pltpu.SMEM)
    return pl.pallas_call(
        _dispatch_kernel,
        grid=(steps,),
        in_specs=[idx_spec, idx_spec,
                  pl.BlockSpec((tm * ROW_SPLIT, LANES), lambda i: (jnp.minimum(i, last), 0))],
        out_specs=pl.BlockSpec(memory_space=pl.ANY),
        out_shape=jax.ShapeDtypeStruct((rows_out * ROW_SPLIT, LANES), x3.dtype),
        scratch_shapes=[pltpu.SemaphoreType.DMA(())],
        compiler_params=_params("arbitrary"),
    )(d1.reshape(steps, 1, tm), d2.reshape(steps, 1, tm), x3)


def _expert_kernel(texp_ref, nact_ref, x_ref, wg_ref, wu_ref, wd_ref, y_ref, x2, acc):
    del texp_ref

    @pl.when(pl.program_id(0) < nact_ref[0])
    def _():
        for s in range(ROW_SPLIT):
            x2[:, s * LANES:(s + 1) * LANES] = x_ref[_chunk(s, EXPERT_TILE), :].astype(BF16)
        _swiglu_chunks(x2[...], wg_ref, wu_ref, wd_ref, acc)
        for s in range(ROW_SPLIT):
            y_ref[_chunk(s, EXPERT_TILE), :] = acc[:, s * LANES:(s + 1) * LANES]

    @pl.when(pl.program_id(0) >= nact_ref[0])
    def _():
        y_ref[...] = jnp.zeros_like(y_ref)


def _experts(xs, tile_exp, n_active, wg, wu, wd):
    ntile = tile_exp.shape[0]
    d, f = wg.shape[1], wg.shape[2]
    grid_spec = pltpu.PrefetchScalarGridSpec(
        num_scalar_prefetch=2,
        grid=(ntile,),
        in_specs=[
            pl.BlockSpec((EXPERT_TILE * ROW_SPLIT, LANES), lambda t, te, na: (t, 0)),
            pl.BlockSpec((1, d, f), lambda t, te, na: (te[t], 0, 0)),
            pl.BlockSpec((1, d, f), lambda t, te, na: (te[t], 0, 0)),
            pl.BlockSpec((1, f, d), lambda t, te, na: (te[t], 0, 0)),
        ],
        out_specs=pl.BlockSpec((EXPERT_TILE * ROW_SPLIT, LANES), lambda t, te, na: (t, 0)),
        scratch_shapes=[pltpu.VMEM((EXPERT_TILE, d), BF16), pltpu.VMEM((EXPERT_TILE, d), F32)],
    )
    return pl.pallas_call(
        _expert_kernel,
        grid_spec=grid_spec,
        out_shape=jax.ShapeDtypeStruct(xs.shape, F32),
        compiler_params=_params("arbitrary"),
    )(tile_exp, n_active, xs, wg, wu, wd)


def _combine_kernel(final, d1a, d2a, d1b, d2b, d1na, d2na, d1nb, d2nb, ye_hbm, h_ref, gate_ref, p_ref, gp_ref,
                    wpg_ref, wpp_ref, gf_ref, o_ref, a1, a2, b1, b2, na1, na2, nb1, nb2, moe, sems):
    j = pl.program_id(0)
    tm = h_ref.shape[0] // 2

    def gather(d1_ref, d2_ref, y1, y2, sem):
        for t in range(tm):
            pltpu.make_async_copy(ye_hbm.at[_row_tile(d1_ref[0, 0, t])], y1.at[_row_tile(t)], sem).start(0)
            pltpu.make_async_copy(ye_hbm.at[_row_tile(d2_ref[0, 0, t])], y2.at[_row_tile(t)], sem).start(1)

    def wait(y1, y2, sem):
        pltpu.make_async_copy(ye_hbm.at[pl.ds(0, tm * ROW_SPLIT)], y1, sem).wait()
        pltpu.make_async_copy(ye_hbm.at[pl.ds(0, tm * ROW_SPLIT)], y2, sem).wait()

    def combine(y1, y2, rows):
        g1 = gate_ref[rows, 0:1]
        g2 = gate_ref[rows, 1:2]
        for s in range(ROW_SPLIT):
            moe[:, s * LANES:(s + 1) * LANES] = g1 * y1[_chunk(s, tm), :] + g2 * y2[_chunk(s, tm), :]
        h = h_ref[rows, :] + moe[...]
        o_ref[rows, :] = _ple_epilogue(h, p_ref[0, rows, :], gp_ref[...], wpg_ref[...], wpp_ref[...],
                                       gf_ref[...] if final else None)

    def step(cur, nxt, base):
        ca1, ca2, cb1, cb2 = cur
        xa1, xa2, xb1, xb2 = nxt
        wait(ca1, ca2, sems.at[base])
        gather(d1na, d2na, xa1, xa2, sems.at[2 - base])
        combine(ca1, ca2, slice(0, tm))
        wait(cb1, cb2, sems.at[base + 1])
        gather(d1nb, d2nb, xb1, xb2, sems.at[3 - base])
        combine(cb1, cb2, slice(tm, 2 * tm))

        @pl.when(j == pl.num_programs(0) - 1)
        def _():
            wait(xa1, xa2, sems.at[2 - base])
            wait(xb1, xb2, sems.at[3 - base])

    even = (a1, a2, b1, b2)
    odd = (na1, na2, nb1, nb2)

    @pl.when(j == 0)
    def _():
        gather(d1a, d2a, a1, a2, sems.at[0])
        gather(d1b, d2b, b1, b2, sems.at[1])

    @pl.when(j % 2 == 0)
    def _():
        step(even, odd, 0)

    @pl.when(j % 2 == 1)
    def _():
        step(odd, even, 2)


def _combine(ye, d1, d2, h, gates, p, layer, gp, wpg, wpp, gf, final, tm):
    n, d = h.shape
    steps = n // (2 * tm)
    tiles = n // tm
    idx_spec = lambda f: pl.BlockSpec((1, 1, tm), lambda i: (jnp.minimum(f(i), tiles - 1), 0, 0),
                                      memory_space=pltpu.SMEM)
    tile_a, tile_b = idx_spec(lambda i: 2 * i), idx_spec(lambda i: 2 * i + 1)
    next_a, next_b = idx_spec(lambda i: 2 * i + 2), idx_spec(lambda i: 2 * i + 3)
    const = lambda i: (0, 0)
    d1 = d1.reshape(tiles, 1, tm)
    d2 = d2.reshape(tiles, 1, tm)
    row_buf = pltpu.VMEM((tm * ROW_SPLIT, LANES), F32)
    return pl.pallas_call(
        functools.partial(_combine_kernel, final),
        grid=(steps,),
        in_specs=[tile_a, tile_a, tile_b, tile_b, next_a, next_a, next_b, next_b,
                  pl.BlockSpec(memory_space=pl.ANY),
                  pl.BlockSpec((2 * tm, d), lambda i: (i, 0)),
                  pl.BlockSpec((2 * tm, 2), lambda i: (i, 0)),
                  pl.BlockSpec((1, 2 * tm, p.shape[-1]), lambda i: (layer, i, 0)),
                  pl.BlockSpec((1, d), const),
                  _resident(wpg.shape),
                  _resident(wpp.shape),
                  pl.BlockSpec((1, d), const)],
        out_specs=pl.BlockSpec((2 * tm, d), lambda i: (i, 0)),
        out_shape=jax.ShapeDtypeStruct((n, d), F32),
        scratch_shapes=[row_buf] * 8 + [pltpu.VMEM((tm, d), F32), pltpu.SemaphoreType.DMA((4,))],
        compiler_params=_params("arbitrary"),
    )(d1, d2, d1, d2, d1, d2, d1, d2, ye, h, gates, p, gp, wpg, wpp, gf)


def _moe(h, g, wr_t, br, wg, wu, wd, p, layer, gp, wpg, wpp, gf, final, tm):
    n = h.shape[0]
    x3, meta, cnt = _router(h, g, wr_t, br, tm)
    counts = cnt[:, 0].astype(jnp.int32)
    tiles = (counts + EXPERT_TILE - 1) // EXPERT_TILE
    padded = tiles * EXPERT_TILE
    ends = jnp.cumsum(padded)
    offs = ends - padded
    expert = jnp.arange(N_EXPERTS, dtype=jnp.int32)[:, None]
    start_of = lambda idx: jnp.sum(jnp.where(idx[None, :] == expert, offs[:, None], 0), axis=0)
    d1 = start_of(meta[0].astype(jnp.int32)) + meta[4].astype(jnp.int32)
    d2 = start_of(meta[1].astype(jnp.int32)) + meta[5].astype(jnp.int32)
    gates = jnp.stack([meta[2], meta[3]], axis=1)
    ntile = 2 * n // EXPERT_TILE + N_EXPERTS
    r = jnp.arange(EXPERT_TILE, dtype=jnp.int32)[None, :]
    in_group = counts[:, None] + r < padded[:, None]
    spare = ends[-1] + (jnp.cumsum(jnp.where(in_group, 0, 1).reshape(-1)) - 1).reshape(in_group.shape)
    fill = jnp.where(in_group, offs[:, None] + counts[:, None] + r, spare).reshape(-1).astype(jnp.int32)
    half = fill.shape[0] // 2
    xs = _dispatch(x3, jnp.concatenate([d1, fill[:half]]), jnp.concatenate([d2, fill[half:]]),
                   ntile * EXPERT_TILE, DISPATCH_TILE if n % DISPATCH_TILE == 0 else tm)
    n_active = jnp.sum(tiles).astype(jnp.int32)
    tile_start = jnp.minimum(jnp.arange(ntile, dtype=jnp.int32), n_active - 1) * EXPERT_TILE
    tile_exp = jnp.minimum(jnp.sum(jnp.where(tile_start[:, None] >= ends[None, :], 1, 0), axis=1),
                           N_EXPERTS - 1).astype(jnp.int32)
    ye = _experts(xs, tile_exp, n_active.reshape(1), wg, wu, wd)
    return _combine(ye, d1, d2, h, gates, p, layer, gp, wpg, wpp, gf, final, min(tm, n // 2))


def _trunk(x, p, past, prm, tok_tile, mix_tile):
    batch, t, d = x.shape
    n = batch * t
    depth = prm["w_in"].shape[0]
    h = x.reshape(n, d)
    new_k, new_v, new_b, new_c = [], [], [], []
    row = lambda a: a.reshape(1, -1)
    p = p.reshape(depth, n, -1)
    keep = min(BAND, t) if past is None else t
    for i in range(depth):
        u, kv = _inproj(h, row(prm["norm_mix"][i]), prm["w_in"][i], tok_tile, t, keep)
        new_k.append(kv[:, :A_WIDTH].reshape(batch, keep, HEADS, HEAD_DIM))
        new_v.append(kv[:, A_WIDTH:].reshape(batch, keep, HEADS, HEAD_DIM))
        if past is None:
            ya = _prompt_attention(u, _prompt_bias_table(prm["rel_bias"][i]), batch, t)
            hist_b = jnp.zeros((batch, HIST_B, B_WIDTH), F32)
            hist_c = jnp.zeros((batch, HIST_C, C_WIDTH), F32)
        else:
            ck, cv, sb, sc = past
            n_past = ck.shape[2]
            bias_c, bias_n = _sample_bias_tables(prm["rel_bias"][i], t, n_past)
            ya = _sample_attention(u, ck[i].reshape(batch, n_past, A_WIDTH),
                                   cv[i].reshape(batch, n_past, A_WIDTH), bias_c, bias_n, batch, t)
            hist_b = jnp.pad(sb[i], ((0, 0), (HIST_B - (B_KERNEL - 1), 0), (0, 0)))
            hist_c = jnp.pad(sc[i], ((0, 0), (HIST_C - (C_KERNEL - 1), 0), (0, 0)))
        wb = jnp.pad(prm["w_dw_b"][i], ((0, HIST_B - B_KERNEL), (0, 0)))
        wc = jnp.pad(prm["w_dw_c"][i], ((0, HIST_C - C_KERNEL), (0, 0)))
        mix_args = (h, ya, u, hist_b, hist_c, wb, row(prm["b_dw_b"][i]), row(prm["ln_g_b"][i]),
                    row(prm["ln_b_b"][i]), wc, row(prm["g_out"][i]), prm["w_out"][i])
        final = i == depth - 1
        tail = (p, i, row(prm["norm_ple"][i]), prm["w_ple_gate"][i], prm["w_ple_proj"][i],
                row(prm["norm_final"]), final)
        j = i // 2
        ffn_args = (row(prm["norm_ffn"][i]), prm["w_ff_gate"], prm["w_ff_up"], prm["w_ff_down"], j)
        if i % 2 == 0 and tok_tile == mix_tile:
            h, nb, nc = _mix_ffn(*mix_args, *ffn_args, *tail, batch, t, mix_tile)
        else:
            h, nb, nc = _mix(*mix_args, batch, t, mix_tile)
            if i % 2 == 0:
                h = _ffn(h, *ffn_args, *tail, tok_tile)
            else:
                h = _moe(h, row(prm["norm_ffn"][i]), prm["w_router"][j].T, prm["b_router"][j].reshape(-1, 1),
                         prm["w_ex_gate"][j], prm["w_ex_up"][j], prm["w_ex_down"][j], *tail, tok_tile)
        new_b.append(nb[:, HIST_B - (B_KERNEL - 1):])
        new_c.append(nc[:, HIST_C - (C_KERNEL - 1):])
    return (h.reshape(batch, t, d), jnp.stack(new_k), jnp.stack(new_v), jnp.stack(new_b), jnp.stack(new_c))


def kernel(x_prompt, x_sample, p_prompt, p_sample, cache_attn_k, cache_attn_v, state_conv_b, state_conv_c,
           w_in, rel_bias, w_dw_b, b_dw_b, ln_g_b, ln_b_b, w_dw_c, g_out, w_out, norm_mix, norm_ffn,
           w_ff_gate, w_ff_up, w_ff_down, w_router, b_router, w_ex_gate, w_ex_up, w_ex_down,
           norm_ple, w_ple_gate, w_ple_proj, norm_final):
    prm = dict(
        w_in=w_in.astype(BF16), rel_bias=rel_bias, w_dw_b=w_dw_b, b_dw_b=b_dw_b, ln_g_b=ln_g_b,
        ln_b_b=ln_b_b, w_dw_c=w_dw_c, g_out=g_out, w_out=w_out.astype(BF16), norm_mix=norm_mix,
        norm_ffn=norm_ffn, w_ff_gate=w_ff_gate.astype(BF16), w_ff_up=w_ff_up.astype(BF16),
        w_ff_down=w_ff_down.astype(BF16), w_router=w_router, b_router=b_router,
        w_ex_gate=w_ex_gate.astype(BF16), w_ex_up=w_ex_up.astype(BF16), w_ex_down=w_ex_down.astype(BF16),
        norm_ple=norm_ple, w_ple_gate=w_ple_gate.astype(BF16), w_ple_proj=w_ple_proj.astype(BF16),
        norm_final=norm_final)
    dec_t = x_sample.shape[1]
    y_p, k_p, v_p, b_p, c_p = _trunk(x_prompt, p_prompt, None, prm, 512, 512)
    y_s, k_s, v_s, b_s, c_s = _trunk(x_sample, p_sample,
                                     (cache_attn_k, cache_attn_v, state_conv_b, state_conv_c),
                                     prm, x_sample.shape[0] * dec_t, dec_t)
    return (y_p, y_s, k_p, v_p, b_p, c_p, k_s, v_s, b_s, c_s)
```
